```python
import math
import jax, jax.numpy as jnp
from jax import lax
import numpy as np

D_MODEL = 4096
BATCH = 2
SEQ = 8192
DEPTH = 2

DN_HEAD_DIM = 128
DN_K_HEADS = D_MODEL // 256
DN_V_HEADS = D_MODEL // 128
DN_CONV = 4
DN_CHUNK = 64
DN_KEY_DIM = DN_K_HEADS * DN_HEAD_DIM
DN_VAL_DIM = DN_V_HEADS * DN_HEAD_DIM
DN_IN_DIM = 2 * DN_KEY_DIM + 2 * DN_VAL_DIM + 2 * DN_V_HEADS
MB_HEAD_DIM = 128
MB_HEADS = D_MODEL // 128
MB_BLOCK = 256
MB_TOPK = 3
MB_Q_CHUNK = 32
MB_WIDTH = MB_HEADS * MB_HEAD_DIM
N_EXPERTS = 32
TOP_K = 4
D_EXPERT = 768
SWIGLU_LIMIT = 7.0
SWIGLU_ALPHA = 1.702
MOE_BLOCK = 128
LN_EPS = 1e-5
NORM_EPS = 1e-6
DEEPNORM_ALPHA = (2 * DEPTH) ** 0.25
DEEPNORM_BETA = (8 * DEPTH) ** -0.25
NEG_INF = -1e30

kernel_name = 'hybrid_deltanet_moba_moe_deepnorm'


def layer_norm(x, g, b):
    xf = x.astype(jnp.float32)
    mu = jnp.mean(xf, axis=-1, keepdims=True)
    xc = xf - mu
    var = jnp.mean(xc * xc, axis=-1, keepdims=True)
    return (xc * lax.rsqrt(var + LN_EPS) * g.astype(jnp.float32) + b.astype(jnp.float32)).astype(x.dtype)


def l2_normalize(x):
    return x * lax.rsqrt(jnp.sum(x * x, axis=-1, keepdims=True) + NORM_EPS)


def causal_depthwise_conv(x, w):
    k_w, c = w.shape
    return lax.conv_general_dilated(x, w[:, None, :].astype(x.dtype), window_strides=(1,),
                                    padding=((k_w - 1, 0),), dimension_numbers=('NWC', 'WIO', 'NWC'),
                                    feature_group_count=c)


def chunked_gated_delta_rule(q, k, v, g, beta):
    bsz, h, t_len, dk = q.shape
    dv = v.shape[-1]
    c = DN_CHUNK
    n = t_len // c
    q = q.reshape(bsz, h, n, c, dk)
    k = k.reshape(bsz, h, n, c, dk)
    v = v.reshape(bsz, h, n, c, dv)
    g = jnp.cumsum(g.reshape(bsz, h, n, c), axis=-1)
    beta = beta.reshape(bsz, h, n, c)
    causal = jnp.tril(jnp.ones((c, c), dtype=bool))
    strict = jnp.tril(jnp.ones((c, c), dtype=bool), -1)
    diff = g[..., :, None] - g[..., None, :]
    decay = jnp.where(causal, jnp.exp(jnp.where(causal, diff, 0.0)), 0.0)
    k_beta = k * beta[..., None]
    v_beta = v * beta[..., None]
    lower = jnp.where(strict, jnp.einsum('bhnid,bhnjd->bhnij', k_beta, k) * decay, 0.0)
    a_mat = lower + jnp.eye(c, dtype=jnp.float32)
    u = lax.linalg.triangular_solve(a_mat, v_beta, left_side=True, lower=True, unit_diagonal=True)
    w = lax.linalg.triangular_solve(a_mat, k_beta * jnp.exp(g)[..., None], left_side=True, lower=True,
                                    unit_diagonal=True)
    intra = jnp.where(causal, jnp.einsum('bhnid,bhnjd->bhnij', q, k) * decay, 0.0)
    q_dec = q * jnp.exp(g)[..., None]
    k_dec = k * jnp.exp(g[..., -1:] - g)[..., None]
    g_tail = jnp.exp(g[..., -1])

    def step(state, inp):
        u_c, w_c, qd_c, kd_c, a_c, gt_c = inp
        v_new = u_c - jnp.einsum('bhck,bhkv->bhcv', w_c, state)
        o_c = jnp.einsum('bhck,bhkv->bhcv', qd_c, state) + jnp.einsum('bhij,bhjv->bhiv', a_c, v_new)
        state = state * gt_c[..., None, None] + jnp.einsum('bhck,bhcv->bhkv', kd_c, v_new)
        return state, o_c

    xs = (jnp.moveaxis(u, 2, 0), jnp.moveaxis(w, 2, 0), jnp.moveaxis(q_dec, 2, 0),
          jnp.moveaxis(k_dec, 2, 0), jnp.moveaxis(intra, 2, 0), jnp.moveaxis(g_tail, 2, 0))
    s0 = jnp.zeros((bsz, h, dk, dv), jnp.float32)
    _, o = lax.scan(step, s0, xs)
    return jnp.moveaxis(o, 0, 2).reshape(bsz, h, t_len, dv)


def gated_deltanet_mixer(x, w_in, conv_w, a_log, dt_bias, norm_w, w_out):
    bsz, t_len, _ = x.shape
    kd, vd = DN_KEY_DIM, DN_VAL_DIM
    proj = x @ w_in
    qkv = jax.nn.silu(causal_depthwise_conv(proj[..., :2 * kd + vd], conv_w))
    z = proj[..., 2 * kd + vd:2 * kd + 2 * vd]
    b_raw = proj[..., 2 * kd + 2 * vd:2 * kd + 2 * vd + DN_V_HEADS]
    a_raw = proj[..., 2 * kd + 2 * vd + DN_V_HEADS:]
    rep = DN_V_HEADS // DN_K_HEADS
    q = qkv[..., :kd].reshape(bsz, t_len, DN_K_HEADS, DN_HEAD_DIM).astype(jnp.float32)
    k = qkv[..., kd:2 * kd].reshape(bsz, t_len, DN_K_HEADS, DN_HEAD_DIM).astype(jnp.float32)
    v = qkv[..., 2 * kd:].reshape(bsz, t_len, DN_V_HEADS, DN_HEAD_DIM).astype(jnp.float32)
    q = jnp.repeat(l2_normalize(q), rep, axis=2) * (DN_HEAD_DIM ** -0.5)
    k = jnp.repeat(l2_normalize(k), rep, axis=2)
    beta = jax.nn.sigmoid(b_raw.astype(jnp.float32))
    g = -jnp.exp(a_log.astype(jnp.float32)) * jax.nn.softplus(a_raw.astype(jnp.float32) + dt_bias.astype(jnp.float32))
    o = chunked_gated_delta_rule(jnp.transpose(q, (0, 2, 1, 3)), jnp.transpose(k, (0, 2, 1, 3)),
                                 jnp.transpose(v, (0, 2, 1, 3)), jnp.transpose(g, (0, 2, 1)),
                                 jnp.transpose(beta, (0, 2, 1)))
    o = jnp.transpose(o, (0, 2, 1, 3))
    zf = z.reshape(bsz, t_len, DN_V_HEADS, DN_HEAD_DIM).astype(jnp.float32)
    o = o * lax.rsqrt(jnp.mean(o * o, axis=-1, keepdims=True) + NORM_EPS) * norm_w.astype(jnp.float32) * jax.nn.silu(zf)
    return o.reshape(bsz, t_len, vd).astype(x.dtype) @ w_out


def alibi_slopes(n_heads):
    return jnp.exp2(-8.0 * jnp.arange(1, n_heads + 1, dtype=jnp.float32) / n_heads)


def moba_attention(q, k, v):
    bsz, h, t_len, dh = q.shape
    nb = t_len // MB_BLOCK
    n_sel = min(MB_TOPK, nb)
    nq = t_len // MB_Q_CHUNK
    scale = dh ** -0.5
    kb = k.reshape(bsz, h, nb, MB_BLOCK, dh)
    vb = v.reshape(bsz, h, nb, MB_BLOCK, dh)
    k_mean = jnp.mean(kb.astype(jnp.float32), axis=3)
    slopes = alibi_slopes(h)[None, :, None]
    bi = jnp.arange(bsz)[:, None, None, None]
    hi = jnp.arange(h)[None, :, None, None]
    in_blk = jnp.arange(MB_BLOCK, dtype=jnp.int32)
    blk_ids = jnp.arange(nb, dtype=jnp.int32)
    q_chunks = jnp.moveaxis(q.reshape(bsz, h, nq, MB_Q_CHUNK, dh), 2, 0)

    def attend(args):
        c, qc = args
        qf = qc.astype(jnp.float32)
        t = c * MB_Q_CHUNK + jnp.arange(MB_Q_CHUNK, dtype=jnp.int32)
        own = (c * MB_Q_CHUNK) // MB_BLOCK
        gate = jnp.einsum('bhqd,bhnd->bhqn', qf, k_mean)
        gate = jnp.where(blk_ids < own, gate, NEG_INF)
        _, sel = lax.top_k(gate, n_sel)
        valid = sel < own
        k_sel = kb[bi, hi, sel].astype(jnp.float32)
        v_sel = vb[bi, hi, sel].astype(jnp.float32)
        s_sel = jnp.einsum('bhqd,bhqjsd->bhqjs', qf, k_sel) * scale
        dist_sel = (t[:, None, None] - (sel[..., None] * MB_BLOCK + in_blk)).astype(jnp.float32)
        s_sel = s_sel - slopes[..., None, None] * dist_sel
        s_sel = jnp.where(valid[..., None], s_sel, NEG_INF)
        k_own = lax.dynamic_index_in_dim(kb, own, axis=2, keepdims=False).astype(jnp.float32)
        v_own = lax.dynamic_index_in_dim(vb, own, axis=2, keepdims=False).astype(jnp.float32)
        dist_own = t[:, None] - (own * MB_BLOCK + in_blk)[None, :]
        s_own = jnp.einsum('bhqd,bhsd->bhqs', qf, k_own) * scale - slopes[..., None] * dist_own.astype(jnp.float32)
        s_own = jnp.where(dist_own >= 0, s_own, NEG_INF)
        scores = jnp.concatenate([s_sel.reshape(bsz, h, MB_Q_CHUNK, n_sel * MB_BLOCK), s_own], axis=-1)
        p = jax.nn.softmax(scores, axis=-1)
        p_sel = p[..., :n_sel * MB_BLOCK].reshape(bsz, h, MB_Q_CHUNK, n_sel, MB_BLOCK)
        p_own = p[..., n_sel * MB_BLOCK:]
        o = jnp.einsum('bhqjs,bhqjsd->bhqd', p_sel, v_sel) + jnp.einsum('bhqs,bhsd->bhqd', p_own, v_own)
        return o.astype(q.dtype)

    o = lax.map(attend, (jnp.arange(nq, dtype=jnp.int32), q_chunks))
    return jnp.moveaxis(o, 0, 2).reshape(bsz, h, t_len, dh)


def moba_mixer(x, w_in, w_out):
    bsz, t_len, _ = x.shape
    proj = x @ w_in
    heads = lambda y: jnp.transpose(y.reshape(bsz, t_len, MB_HEADS, MB_HEAD_DIM), (0, 2, 1, 3))
    q = heads(proj[..., :MB_WIDTH])
    k = heads(proj[..., MB_WIDTH:2 * MB_WIDTH])
    v = heads(proj[..., 2 * MB_WIDTH:])
    t_pad = -(-t_len // MB_BLOCK) * MB_BLOCK
    pad = ((0, 0), (0, 0), (0, t_pad - t_len), (0, 0))
    o = moba_attention(jnp.pad(q, pad), jnp.pad(k, pad), jnp.pad(v, pad))[:, :, :t_len]
    return jnp.transpose(o, (0, 2, 1, 3)).reshape(bsz, t_len, MB_WIDTH) @ w_out


def clamped_swiglu(h_gate, h_up):
    h_gate = jnp.minimum(h_gate, SWIGLU_LIMIT)
    h_up = jnp.clip(h_up, -SWIGLU_LIMIT, SWIGLU_LIMIT)
    return (h_up + 1.0) * (h_gate * jax.nn.sigmoid(SWIGLU_ALPHA * h_gate))


def moe_ffn(x2d, router_w, router_b, w_gate, b_gate, w_up, b_up, w_down, b_down):
    n_tok, d = x2d.shape
    logits = x2d.astype(jnp.float32) @ router_w.astype(jnp.float32) + router_b.astype(jnp.float32)
    top_vals, top_idx = lax.top_k(logits, TOP_K)
    gates = jax.nn.softmax(top_vals, axis=-1)
    n_assign = n_tok * TOP_K
    flat_e = top_idx.reshape(-1).astype(jnp.int32)
    flat_tok = jnp.repeat(jnp.arange(n_tok, dtype=jnp.int32), TOP_K)
    flat_g = gates.reshape(-1)
    order = jnp.argsort(flat_e)
    sorted_e = flat_e[order]
    counts = jnp.bincount(flat_e, length=N_EXPERTS).astype(jnp.int32)
    starts = jnp.cumsum(counts) - counts
    padded = (counts + MOE_BLOCK - 1) // MOE_BLOCK * MOE_BLOCK
    pends = jnp.cumsum(padded)
    pstarts = pends - padded
    dest = pstarts[sorted_e] + (jnp.arange(n_assign, dtype=jnp.int32) - starts[sorted_e])
    n_blocks = -(-n_assign // MOE_BLOCK) + N_EXPERTS
    rows = jnp.full((n_blocks * MOE_BLOCK,), n_tok, jnp.int32).at[dest].set(flat_tok[order])
    row_gate = jnp.zeros((n_blocks * MOE_BLOCK,), jnp.float32).at[dest].set(flat_g[order])
    block_e = jnp.minimum(jnp.searchsorted(pends, jnp.arange(n_blocks, dtype=jnp.int32) * MOE_BLOCK, side='right'),
                          N_EXPERTS - 1).astype(jnp.int32)
    x_pad = jnp.concatenate([x2d, jnp.zeros((1, d), x2d.dtype)], axis=0)

    def step(acc, inp):
        r, gt, e = inp
        xb = x_pad[r]
        h = clamped_swiglu(xb @ w_gate[e] + b_gate[e], xb @ w_up[e] + b_up[e])
        y = h @ w_down[e] + b_down[e]
        return acc.at[r].add(y.astype(jnp.float32) * gt[:, None]), None

    acc0 = jnp.zeros((n_tok + 1, d), jnp.float32)
    acc, _ = lax.scan(step, acc0, (rows.reshape(n_blocks, MOE_BLOCK), row_gate.reshape(n_blocks, MOE_BLOCK), block_e))
    return acc[:n_tok].astype(x2d.dtype)


def setup_inputs(seed: int = 0) -> dict:
    key = jax.random.key(seed)
    ks = jax.random.split(key, 24)
    n_a = (DEPTH + 1) // 2
    n_b = DEPTH // 2
    conv_ch = 2 * DN_KEY_DIM + DN_VAL_DIM
    nrm = lambda k, shape, scale: jax.random.normal(k, shape, jnp.float32) * scale
    return {
        'x': jax.random.normal(ks[0], (BATCH, SEQ, D_MODEL), jnp.float32),
        'dn_w_in': nrm(ks[1], (n_a, D_MODEL, DN_IN_DIM), D_MODEL ** -0.5),
        'dn_conv_w': nrm(ks[2], (n_a, DN_CONV, conv_ch), DN_CONV ** -0.5),
        'dn_a_log': jnp.log(jax.random.uniform(ks[3], (n_a, DN_V_HEADS), jnp.float32, 1.0, 16.0)),
        'dn_dt_bias': nrm(ks[4], (n_a, DN_V_HEADS), 0.1),
        'dn_norm_w': 1.0 + nrm(ks[5], (n_a, DN_HEAD_DIM), 0.02),
        'dn_w_out': nrm(ks[6], (n_a, DN_VAL_DIM, D_MODEL), DEEPNORM_BETA * DN_VAL_DIM ** -0.5),
        'mb_w_in': nrm(ks[7], (n_b, D_MODEL, 3 * MB_WIDTH), D_MODEL ** -0.5),
        'mb_w_out': nrm(ks[8], (n_b, MB_WIDTH, D_MODEL), DEEPNORM_BETA * MB_WIDTH ** -0.5),
        'ln_g': 1.0 + nrm(ks[9], (DEPTH, 2, D_MODEL), 0.02),
        'ln_b': nrm(ks[10], (DEPTH, 2, D_MODEL), 0.01),
        'router_w': nrm(ks[11], (DEPTH, D_MODEL, N_EXPERTS), D_MODEL ** -0.5),
        'router_b': nrm(ks[12], (DEPTH, N_EXPERTS), 0.01),
        'w_gate': nrm(ks[13], (DEPTH, N_EXPERTS, D_MODEL, D_EXPERT), D_MODEL ** -0.5),
        'b_gate': nrm(ks[14], (DEPTH, N_EXPERTS, D_EXPERT), 0.01),
        'w_up': nrm(ks[15], (DEPTH, N_EXPERTS, D_MODEL, D_EXPERT), D_MODEL ** -0.5),
        'b_up': nrm(ks[16], (DEPTH, N_EXPERTS, D_EXPERT), 0.01),
        'w_down': nrm(ks[17], (DEPTH, N_EXPERTS, D_EXPERT, D_MODEL), DEEPNORM_BETA * D_EXPERT ** -0.5),
        'b_down': nrm(ks[18], (DEPTH, N_EXPERTS, D_MODEL), 0.01),
    }


def reference(x, dn_w_in, dn_conv_w, dn_a_log, dn_dt_bias, dn_norm_w, dn_w_out, mb_w_in, mb_w_out,
              ln_g, ln_b, router_w, router_b, w_gate, b_gate, w_up, b_up, w_down, b_down):
    bsz, t_len, d = x.shape
    for i in range(DEPTH):
        j = i // 2
        if i % 2 == 0:
            h = gated_deltanet_mixer(x, dn_w_in[j], dn_conv_w[j], dn_a_log[j], dn_dt_bias[j], dn_norm_w[j], dn_w_out[j])
        else:
            h = moba_mixer(x, mb_w_in[j], mb_w_out[j])
        x = layer_norm(DEEPNORM_ALPHA * x + h, ln_g[i, 0], ln_b[i, 0])
        f = moe_ffn(x.reshape(bsz * t_len, d), router_w[i], router_b[i], w_gate[i], b_gate[i],
                    w_up[i], b_up[i], w_down[i], b_down[i]).reshape(bsz, t_len, d)
        x = layer_norm(DEEPNORM_ALPHA * x + f, ln_g[i, 1], ln_b[i, 1])
    return x
```

```python
import functools

import jax
import jax.numpy as jnp
from jax import lax
from jax.experimental import pallas as pl
from jax.experimental.pallas import tpu as pltpu

HEAD_DIM = 128
DN_CONV = 4
DN_CHUNK = 64
MB_BLOCK = 256
MB_TOPK = 3
TOP_K = 4
SWIGLU_LIMIT = 7.0
SWIGLU_ALPHA = 1.702
LN_EPS = 1e-5
NORM_EPS = 1e-6
NEG_INF = -1e30
LANES = 128
HALO = 16
VMEM_LIMIT = 56 * 1024 * 1024

_HI = lax.Precision.HIGHEST


def _cparams(sem):
    return pltpu.CompilerParams(dimension_semantics=sem, vmem_limit_bytes=VMEM_LIMIT)


def _sigmoid(v):
    return 1.0 / (1.0 + jnp.exp(-v))


def _silu(v):
    return v * _sigmoid(v)


def _mm_kernel(x_ref, w_ref, o_ref):
    o_ref[...] = jnp.dot(x_ref[...], w_ref[...], preferred_element_type=jnp.float32).astype(o_ref.dtype)


def _matmul(x, w, out_dtype, tm, tn):
    m, k = x.shape
    n = w.shape[1]
    tm, tn = min(tm, m), min(tn, n)
    return pl.pallas_call(
        _mm_kernel,
        grid=(m // tm, n // tn),
        in_specs=[pl.BlockSpec((tm, k), lambda i, j: (i, 0)),
                  pl.BlockSpec((k, tn), lambda i, j: (0, j))],
        out_specs=pl.BlockSpec((tm, tn), lambda i, j: (i, j)),
        out_shape=jax.ShapeDtypeStruct((m, n), out_dtype),
        compiler_params=_cparams(("parallel", "arbitrary")),
        name="dense_matmul",
    )(x, w)


def _dn_gates_kernel(x_ref, w_ref, alog_ref, dtb_ref, o_ref, *, n_heads):
    tm = x_ref.shape[0]
    ba = jnp.dot(x_ref[...], w_ref[...], precision=_HI, preferred_element_type=jnp.float32)
    lane = lax.broadcasted_iota(jnp.int32, ba.shape, 1)
    beta = _sigmoid(ba)
    v = ba + dtb_ref[...]
    softplus = jnp.maximum(v, 0.0) + jnp.log(1.0 + jnp.exp(-jnp.abs(v)))
    g = -jnp.exp(alog_ref[...]) * softplus
    g = jnp.where((lane >= n_heads) & (lane < 2 * n_heads), g, 0.0)
    ri = lax.broadcasted_iota(jnp.int32, (tm, tm), 0)
    ci = lax.broadcasted_iota(jnp.int32, (tm, tm), 1)
    shift = DN_CHUNK.bit_length() - 1
    tril = jnp.where((ci <= ri) & ((ri >> shift) == (ci >> shift)), 1.0, 0.0)
    gc = jnp.dot(tril, g, precision=_HI, preferred_element_type=jnp.float32)
    o_ref[...] = jnp.where(lane < n_heads, beta, gc)


def _dn_gates(x2d, w_ba, a_log, dt_bias, n_heads):
    n, d = x2d.shape
    tm = min(256, n)
    pad = LANES - 2 * n_heads
    w_p = jnp.pad(w_ba, ((0, 0), (0, pad)))
    alog_p = jnp.pad(a_log, (n_heads, pad))[None, :]
    dtb_p = jnp.pad(dt_bias, (n_heads, pad))[None, :]
    return pl.pallas_call(
        functools.partial(_dn_gates_kernel, n_heads=n_heads),
        grid=(n // tm,),
        in_specs=[pl.BlockSpec((tm, d), lambda i: (i, 0)),
                  pl.BlockSpec((d, LANES), lambda i: (0, 0)),
                  pl.BlockSpec((1, LANES), lambda i: (0, 0)),
                  pl.BlockSpec((1, LANES), lambda i: (0, 0))],
        out_specs=pl.BlockSpec((tm, LANES), lambda i: (i, 0)),
        out_shape=jax.ShapeDtypeStruct((n, LANES), jnp.float32),
        compiler_params=_cparams(("parallel",)),
        name="dn_gates",
    )(x2d, w_p, alog_p, dtb_p)


def _causal_conv_silu(x_ref, halo_ref, w_ref, first):
    tb = x_ref.shape[0]
    halo = jnp.where(first, 0.0, halo_ref[...].astype(jnp.float32))
    xcat = jnp.concatenate([halo, x_ref[...].astype(jnp.float32)], axis=0)
    w = w_ref[...]
    acc = xcat[HALO:] * w[DN_CONV - 1:DN_CONV, :]
    for s in range(1, DN_CONV):
        acc = acc + pltpu.roll(xcat, s, 0)[HALO:] * w[DN_CONV - 1 - s:DN_CONV - s, :]
    del tb
    return _silu(acc)


def _l2norm(v):
    return v * lax.rsqrt(jnp.sum(v * v, axis=-1, keepdims=True) + NORM_EPS)


def _unit_lower_inverse(a):
    c = a.shape[0]
    ri = lax.broadcasted_iota(jnp.int32, (c, c), 0)
    ci = lax.broadcasted_iota(jnp.int32, (c, c), 1)
    eye = jnp.where(ri == ci, 1.0, 0.0)
    x = eye - jnp.where((ri >> 1) == (ci >> 1), a, 0.0)
    s, log_s = 2, 1
    while s < c:
        same = (ri >> (log_s + 1)) == (ci >> (log_s + 1))
        off = jnp.where(same & ((ri & s) != 0) & ((ci & s) == 0), a, 0.0)
        t = jnp.dot(off, x, precision=_HI, preferred_element_type=jnp.float32)
        x = x - jnp.dot(x, t, precision=_HI, preferred_element_type=jnp.float32)
        s, log_s = 2 * s, log_s + 1
    return x


def _dn_kernel(q_ref, k_ref, v_ref, z_ref, qh_ref, kh_ref, vh_ref, g_ref, gt_ref,
               wq_ref, wk_ref, wv_ref, nw_ref, o_ref, state_ref, *, n_vheads):
    hk = pl.program_id(1)
    t = pl.program_id(2)
    first = t == 0
    tb = q_ref.shape[0]
    rep = v_ref.shape[1] // HEAD_DIM
    c = DN_CHUNK

    @pl.when(first)
    def _():
        state_ref[...] = jnp.zeros_like(state_ref)

    q = _l2norm(_causal_conv_silu(q_ref, qh_ref, wq_ref, first)) * (HEAD_DIM ** -0.5)
    k = _l2norm(_causal_conv_silu(k_ref, kh_ref, wk_ref, first))
    v = _causal_conv_silu(v_ref, vh_ref, wv_ref, first)
    gates = g_ref[...]
    lane = lax.broadcasted_iota(jnp.int32, gates.shape, 1)
    ri = lax.broadcasted_iota(jnp.int32, (c, c), 0)
    ci = lax.broadcasted_iota(jnp.int32, (c, c), 1)
    causal = ci <= ri
    strict = ci < ri
    nt = (((1,), (1,)), ((), ()))
    q_b = q.astype(jnp.bfloat16)
    k_b = k.astype(jnp.bfloat16)

    outs = []
    for j in range(rep):
        hv = hk * rep + j
        beta_col = jnp.sum(jnp.where(lane == hv, gates, 0.0), axis=-1, keepdims=True)
        gc_col = jnp.sum(jnp.where(lane == n_vheads + hv, gates, 0.0), axis=-1, keepdims=True)
        gc_row = gt_ref[pl.ds(n_vheads + hv, 1), :]
        vj = v[:, j * HEAD_DIM:(j + 1) * HEAD_DIM]
        state = state_ref[j]
        o_chunks = []
        for ch in range(tb // c):
            sl = slice(ch * c, (ch + 1) * c)
            kc, qc, vc = k[sl], q[sl], vj[sl]
            kcb, qcb = k_b[sl], q_b[sl]
            beta, gcol, grow = beta_col[sl], gc_col[sl], gc_row[:, sl]
            decay = jnp.where(causal, jnp.exp(jnp.where(causal, gcol - grow, 0.0)), 0.0)
            kk = lax.dot_general(kcb, kcb, nt, preferred_element_type=jnp.float32)
            qk = lax.dot_general(qcb, kcb, nt, preferred_element_type=jnp.float32)
            a_mat = jnp.where(strict, beta * kk * decay, 0.0)
            inv = _unit_lower_inverse(a_mat)
            eg = jnp.exp(gcol)
            rhs = jnp.concatenate([vc * beta, kc * (beta * eg)], axis=1)
            uw = jnp.dot(inv, rhs, precision=_HI, preferred_element_type=jnp.float32)
            u, w = uw[:, :HEAD_DIM], uw[:, HEAD_DIM:]
            intra = jnp.where(causal, qk * decay, 0.0)
            g_last = grow[:, c - 1:c]
            lhs = jnp.concatenate([w, qc * eg], axis=0).astype(jnp.bfloat16)
            ws = jnp.dot(lhs, state.astype(jnp.bfloat16), preferred_element_type=jnp.float32)
            v_new = u - ws[:c]
            v_new_b = v_new.astype(jnp.bfloat16)
            o_chunks.append(ws[c:] + jnp.dot(intra.astype(jnp.bfloat16), v_new_b,
                                             preferred_element_type=jnp.float32))
            k_dec_t = (kc * jnp.exp(g_last - gcol)).T.astype(jnp.bfloat16)
            state = state * jnp.exp(g_last) + jnp.dot(k_dec_t, v_new_b, preferred_element_type=jnp.float32)
        state_ref[j] = state
        o = jnp.concatenate(o_chunks, axis=0)
        zj = z_ref[:, j * HEAD_DIM:(j + 1) * HEAD_DIM].astype(jnp.float32)
        o = o * lax.rsqrt(jnp.mean(o * o, axis=-1, keepdims=True) + NORM_EPS) * nw_ref[...] * _silu(zj)
        outs.append(o)
    o_ref[...] = jnp.concatenate(outs, axis=1).astype(o_ref.dtype)


def _dn_delta_rule(proj, gates, gates_t, conv_w, norm_w, bsz, t_len, d_model):
    n_kheads = d_model // (2 * HEAD_DIM)
    n_vheads = d_model // HEAD_DIM
    rep = n_vheads // n_kheads
    vw = rep * HEAD_DIM
    tb = min(256, t_len)
    nt = t_len // tb
    hb = tb // HALO
    k_off = n_kheads
    v_off = 2 * n_kheads * HEAD_DIM // vw
    z_off = v_off + n_vheads * HEAD_DIM // vw

    def row(b, h, t):
        return b * nt + t

    def halo_row(b, h, t):
        return jnp.maximum((b * nt + t) * hb - 1, 0)

    in_specs = [
        pl.BlockSpec((tb, HEAD_DIM), lambda b, h, t: (row(b, h, t), h)),
        pl.BlockSpec((tb, HEAD_DIM), lambda b, h, t: (row(b, h, t), k_off + h)),
        pl.BlockSpec((tb, vw), lambda b, h, t: (row(b, h, t), v_off + h)),
        pl.BlockSpec((tb, vw), lambda b, h, t: (row(b, h, t), z_off + h)),
        pl.BlockSpec((HALO, HEAD_DIM), lambda b, h, t: (halo_row(b, h, t), h)),
        pl.BlockSpec((HALO, HEAD_DIM), lambda b, h, t: (halo_row(b, h, t), k_off + h)),
        pl.BlockSpec((HALO, vw), lambda b, h, t: (halo_row(b, h, t), v_off + h)),
        pl.BlockSpec((tb, LANES), lambda b, h, t: (row(b, h, t), 0)),
        pl.BlockSpec((None, LANES, tb), lambda b, h, t: (b, 0, t)),
        pl.BlockSpec((DN_CONV, HEAD_DIM), lambda b, h, t: (0, h)),
        pl.BlockSpec((DN_CONV, HEAD_DIM), lambda b, h, t: (0, k_off + h)),
        pl.BlockSpec((DN_CONV, vw), lambda b, h, t: (0, v_off + h)),
        pl.BlockSpec((1, HEAD_DIM), lambda b, h, t: (0, 0)),
    ]
    return pl.pallas_call(
        functools.partial(_dn_kernel, n_vheads=n_vheads),
        grid=(bsz, n_kheads, nt),
        in_specs=in_specs,
        out_specs=pl.BlockSpec((tb, vw), lambda b, h, t: (row(b, h, t), h)),
        out_shape=jax.ShapeDtypeStruct((bsz * t_len, n_vheads * HEAD_DIM), jnp.bfloat16),
        scratch_shapes=[pltpu.VMEM((rep, HEAD_DIM, HEAD_DIM), jnp.float32)],
        compiler_params=_cparams(("parallel", "parallel", "arbitrary")),
        name="dn_delta_rule",
    )(proj, proj, proj, proj, proj, proj, proj, gates, gates_t, conv_w, conv_w, conv_w, norm_w[None, :])


def _moba_kernel(slope_ref, q_ref, k_ref, v_ref, o_ref, kmean_ref):
    h = pl.program_id(1)
    own = pl.program_id(2)
    tq = q_ref.shape[0]
    t_len = k_ref.shape[0]
    nb = t_len // MB_BLOCK
    nt = (((1,), (1,)), ((), ()))

    @pl.when(own == 0)
    def _():
        kf = k_ref[...].astype(jnp.float32).reshape(nb, MB_BLOCK, HEAD_DIM)
        kmean_ref[...] = jnp.sum(kf, axis=1) * (1.0 / MB_BLOCK)

    q = q_ref[...]
    gate = lax.dot_general(q.astype(jnp.float32), kmean_ref[...], nt, precision=_HI,
                           preferred_element_type=jnp.float32)
    blk = lax.broadcasted_iota(jnp.int32, gate.shape, 1)
    gate = jnp.where(blk < own, gate, -jnp.inf)
    sels = []
    for r in range(MB_TOPK):
        mx = jnp.max(gate, axis=-1, keepdims=True)
        idx = jnp.min(jnp.where(gate == mx, blk, nb), axis=-1, keepdims=True)
        sels.append(jnp.where(r < own, idx, -1))
        gate = jnp.where(blk == idx, -jnp.inf, gate)

    slope = slope_ref[h]
    scale = HEAD_DIM ** -0.5
    rq = lax.broadcasted_iota(jnp.int32, (tq, MB_BLOCK), 0)
    ck = lax.broadcasted_iota(jnp.int32, (tq, MB_BLOCK), 1)
    rel = (rq - ck).astype(jnp.float32)

    def attend(n, mask, carry):
        m, l, acc = carry
        start = pl.multiple_of(n * MB_BLOCK, MB_BLOCK)
        kn = k_ref[pl.ds(start, MB_BLOCK), :]
        vn = v_ref[pl.ds(start, MB_BLOCK), :]
        s = lax.dot_general(q, kn, nt, preferred_element_type=jnp.float32) * scale
        dist = rel + ((own - n) * MB_BLOCK).astype(jnp.float32)
        s = jnp.where(mask, s - slope * dist, NEG_INF)
        m_new = jnp.maximum(m, jnp.max(s, axis=-1, keepdims=True))
        p = jnp.where(mask, jnp.exp(s - m_new), 0.0)
        alpha = jnp.exp(m - m_new)
        l = alpha * l + jnp.sum(p, axis=-1, keepdims=True)
        acc = alpha * acc + jnp.dot(p.astype(jnp.bfloat16), vn, preferred_element_type=jnp.float32)
        return m_new, l, acc

    def past_block(n, carry):
        return attend(n, (sels[0] == n) | (sels[1] == n) | (sels[2] == n), carry)

    m0 = jnp.full((tq, 1), NEG_INF, jnp.float32)
    l0 = jnp.zeros((tq, 1), jnp.float32)
    a0 = jnp.zeros((tq, HEAD_DIM), jnp.float32)
    carry = lax.fori_loop(0, own, past_block, (m0, l0, a0))
    _, l, acc = attend(own, rq >= ck, carry)
    o_ref[...] = (acc / l).astype(o_ref.dtype)


def _moba_attention(proj, bsz, t_len, n_heads):
    tq = MB_BLOCK
    nq = t_len // tq
    nb = t_len // MB_BLOCK
    slopes = jnp.exp2(-8.0 * jnp.arange(1, n_heads + 1, dtype=jnp.float32) / n_heads)
    grid_spec = pltpu.PrefetchScalarGridSpec(
        num_scalar_prefetch=1,
        grid=(bsz, n_heads, nq),
        in_specs=[pl.BlockSpec((tq, HEAD_DIM), lambda b, h, i, s: (b * nq + i, h)),
                  pl.BlockSpec((t_len, HEAD_DIM), lambda b, h, i, s: (b, n_heads + h)),
                  pl.BlockSpec((t_len, HEAD_DIM), lambda b, h, i, s: (b, 2 * n_heads + h))],
        out_specs=pl.BlockSpec((tq, HEAD_DIM), lambda b, h, i, s: (b * nq + i, h)),
        scratch_shapes=[pltpu.VMEM((nb, HEAD_DIM), jnp.float32)],
    )
    return pl.pallas_call(
        _moba_kernel,
        grid_spec=grid_spec,
        out_shape=jax.ShapeDtypeStruct((bsz * t_len, n_heads * HEAD_DIM), jnp.bfloat16),
        compiler_params=_cparams(("parallel", "parallel", "arbitrary")),
        name="moba_attention",
    )(slopes, proj, proj, proj)


def _layer_norm_rows(y, g, b):
    mu = jnp.mean(y, axis=-1, keepdims=True)
    yc = y - mu
    var = jnp.mean(yc * yc, axis=-1, keepdims=True)
    return yc * lax.rsqrt(var + LN_EPS) * g + b


def _post_mixer_kernel(h_ref, w_ref, xres_ref, g_ref, b_ref, rw_ref, rb_ref,
                       x1_ref, x1b_ref, topi_ref, topg_ref, *, alpha, n_experts):
    j = pl.program_id(1)
    tn = w_ref.shape[1]
    col = pl.multiple_of(j * tn, tn)
    x1_ref[:, pl.ds(col, tn)] = jnp.dot(h_ref[...], w_ref[...], preferred_element_type=jnp.float32)

    @pl.when(j == pl.num_programs(1) - 1)
    def _():
        x1 = _layer_norm_rows(alpha * xres_ref[...] + x1_ref[...], g_ref[...], b_ref[...])
        x1_ref[...] = x1
        x1b_ref[...] = x1.astype(x1b_ref.dtype)
        logits = jnp.dot(x1, rw_ref[...], precision=_HI, preferred_element_type=jnp.float32) + rb_ref[...]
        lane = lax.broadcasted_iota(jnp.int32, logits.shape, 1)
        logits = jnp.where(lane < n_experts, logits, -jnp.inf)
        topi = jnp.zeros(logits.shape, jnp.int32)
        topv = jnp.full(logits.shape, -jnp.inf, jnp.float32)
        for r in range(TOP_K):
            mx = jnp.max(logits, axis=-1, keepdims=True)
            idx = jnp.min(jnp.where(logits == mx, lane, LANES), axis=-1, keepdims=True)
            topi = jnp.where(lane == r, idx, topi)
            topv = jnp.where(lane == r, mx, topv)
            logits = jnp.where(lane == idx, -jnp.inf, logits)
        e = jnp.exp(topv - jnp.max(topv, axis=-1, keepdims=True))
        topi_ref[...] = topi
        topg_ref[...] = e / jnp.sum(e, axis=-1, keepdims=True)


def _post_mixer(h, w_out, xres, ln_g, ln_b, router_w, router_b, alpha):
    n, kdim = h.shape
    d = w_out.shape[1]
    n_experts = router_w.shape[1]
    tm, tn = min(256, n), min(512, d)
    rw = jnp.pad(router_w, ((0, 0), (0, LANES - n_experts)))
    rb = jnp.pad(router_b, (0, LANES - n_experts))[None, :]
    row = lambda i, j: (i, 0)
    fixed = lambda i, j: (0, 0)
    return pl.pallas_call(
        functools.partial(_post_mixer_kernel, alpha=alpha, n_experts=n_experts),
        grid=(n // tm, d // tn),
        in_specs=[pl.BlockSpec((tm, kdim), row),
                  pl.BlockSpec((kdim, tn), lambda i, j: (0, j)),
                  pl.BlockSpec((tm, d), row),
                  pl.BlockSpec((1, d), fixed),
                  pl.BlockSpec((1, d), fixed),
                  pl.BlockSpec((d, LANES), fixed),
                  pl.BlockSpec((1, LANES), fixed)],
        out_specs=[pl.BlockSpec((tm, d), row),
                   pl.BlockSpec((tm, d), row),
                   pl.BlockSpec((tm, LANES), row),
                   pl.BlockSpec((tm, LANES), row)],
        out_shape=[jax.ShapeDtypeStruct((n, d), jnp.float32),
                   jax.ShapeDtypeStruct((n, d), jnp.bfloat16),
                   jax.ShapeDtypeStruct((n, LANES), jnp.int32),
                   jax.ShapeDtypeStruct((n, LANES), jnp.float32)],
        compiler_params=_cparams(("parallel", "arbitrary")),
        name="post_mixer",
    )(h, w_out, xres, ln_g[None, :], ln_b[None, :], rw, rb)


def _moe_kernel(be_ref, nu_ref, xs_ref, wg_ref, wu_ref, wd_ref, bg_ref, bu_ref, bd_ref, y_ref, acc_ref):
    i = pl.program_id(0)
    j = pl.program_id(1)
    last = pl.num_programs(1) - 1
    used = i < nu_ref[0]

    @pl.when(used)
    def _():
        xs = xs_ref[...]
        h_gate = jnp.dot(xs, wg_ref[...], preferred_element_type=jnp.float32) + bg_ref[...]
        h_up = jnp.dot(xs, wu_ref[...], preferred_element_type=jnp.float32) + bu_ref[...]
        h_gate = jnp.minimum(h_gate, SWIGLU_LIMIT)
        h_up = jnp.clip(h_up, -SWIGLU_LIMIT, SWIGLU_LIMIT)
        hid = (h_up + 1.0) * (h_gate * _sigmoid(SWIGLU_ALPHA * h_gate))
        part = jnp.dot(hid.astype(jnp.bfloat16), wd_ref[...], preferred_element_type=jnp.float32)

        @pl.when(j == 0)
        def _():
            acc_ref[...] = part

        @pl.when(j > 0)
        def _():
            acc_ref[...] += part

        @pl.when(j == last)
        def _():
            y_ref[...] = (acc_ref[...] + bd_ref[...]).astype(y_ref.dtype)

    @pl.when(jnp.logical_not(used) & (j == last))
    def _():
        y_ref[...] = jnp.zeros_like(y_ref)


def _moe_experts(xs, block_e, n_used, w_gate, w_up, w_down, b_gate, b_up, b_down, tm):
    n_slots, d = xs.shape
    n_experts, _, f = w_gate.shape
    tf = min(256, f)
    nf = f // tf
    n_blocks = n_slots // tm

    def jj(i, j, nu):
        return jnp.where(i < nu[0], j, nf - 1)

    grid_spec = pltpu.PrefetchScalarGridSpec(
        num_scalar_prefetch=2,
        grid=(n_blocks, nf),
        in_specs=[pl.BlockSpec((tm, d), lambda i, j, be, nu: (jnp.minimum(i, nu[0] - 1), 0)),
                  pl.BlockSpec((None, d, tf), lambda i, j, be, nu: (be[i], 0, jj(i, j, nu))),
                  pl.BlockSpec((None, d, tf), lambda i, j, be, nu: (be[i], 0, jj(i, j, nu))),
                  pl.BlockSpec((None, tf, d), lambda i, j, be, nu: (be[i], jj(i, j, nu), 0)),
                  pl.BlockSpec((None, 1, tf), lambda i, j, be, nu: (be[i], 0, jj(i, j, nu))),
                  pl.BlockSpec((None, 1, tf), lambda i, j, be, nu: (be[i], 0, jj(i, j, nu))),
                  pl.BlockSpec((None, 1, d), lambda i, j, be, nu: (be[i], 0, 0))],
        out_specs=pl.BlockSpec((tm, d), lambda i, j, be, nu: (i, 0)),
        scratch_shapes=[pltpu.VMEM((tm, d), jnp.float32)],
    )
    return pl.pallas_call(
        _moe_kernel,
        grid_spec=grid_spec,
        out_shape=jax.ShapeDtypeStruct((n_slots, d), jnp.bfloat16),
        compiler_params=_cparams(("arbitrary", "arbitrary")),
        name="moe_experts",
    )(block_e, n_used, xs, w_gate, w_up, w_down, b_gate[:, None, :], b_up[:, None, :], b_down[:, None, :])


def _combine_kernel(yg_ref, gate_ref, x1_ref, g_ref, b_ref, o_ref, ob_ref, *, alpha):
    gates = gate_ref[...]
    f = yg_ref[0].astype(jnp.float32) * gates[:, 0:1]
    for r in range(1, TOP_K):
        f = f + yg_ref[r].astype(jnp.float32) * gates[:, r:r + 1]
    out = _layer_norm_rows(alpha * x1_ref[...] + f, g_ref[...], b_ref[...])
    o_ref[...] = out
    ob_ref[...] = out.astype(ob_ref.dtype)


def _moe_combine(yg, gates, x1, ln_g, ln_b, alpha):
    n, d = x1.shape
    tm = min(256, n)
    return pl.pallas_call(
        functools.partial(_combine_kernel, alpha=alpha),
        grid=(n // tm,),
        in_specs=[pl.BlockSpec((TOP_K, tm, d), lambda i: (0, i, 0)),
                  pl.BlockSpec((tm, LANES), lambda i: (i, 0)),
                  pl.BlockSpec((tm, d), lambda i: (i, 0)),
                  pl.BlockSpec((1, d), lambda i: (0, 0)),
                  pl.BlockSpec((1, d), lambda i: (0, 0))],
        out_specs=[pl.BlockSpec((tm, d), lambda i: (i, 0)),
                   pl.BlockSpec((tm, d), lambda i: (i, 0))],
        out_shape=[jax.ShapeDtypeStruct((n, d), jnp.float32),
                   jax.ShapeDtypeStruct((n, d), jnp.bfloat16)],
        compiler_params=_cparams(("parallel",)),
        name="moe_combine",
    )(yg, gates, x1, ln_g[None, :], ln_b[None, :])


def _moe_layer(x1, x1b, topi, topg, ln_g, ln_b, w_gate, b_gate, w_up, b_up, w_down, b_down, alpha):
    n_tok, d = x1.shape
    n_experts = w_gate.shape[0]
    n_assign = n_tok * TOP_K
    tm = min(512, n_assign // n_experts)
    flat_e = topi[:, :TOP_K].reshape(-1)
    onehot = (flat_e[:, None] == jnp.arange(n_experts, dtype=jnp.int32)[None, :]).astype(jnp.int32)
    csum = jnp.cumsum(onehot, axis=0)
    rank = jnp.take_along_axis(csum, flat_e[:, None], axis=1)[:, 0] - 1
    counts = csum[-1]
    padded = (counts + tm - 1) // tm * tm
    pends = jnp.cumsum(padded)
    dest = (pends - padded)[flat_e] + rank
    n_blocks = n_assign // tm + n_experts
    flat_tok = jnp.arange(n_assign, dtype=jnp.int32) // TOP_K
    rows = jnp.zeros((n_blocks * tm,), jnp.int32).at[dest].set(flat_tok)
    block_e = jnp.minimum(jnp.searchsorted(pends, jnp.arange(n_blocks, dtype=jnp.int32) * tm, side='right'),
                          n_experts - 1).astype(jnp.int32)
    n_used = (pends[-1:] // tm).astype(jnp.int32)
    xs = jnp.take(x1b, rows, axis=0)
    y = _moe_experts(xs, block_e, n_used, w_gate.astype(jnp.bfloat16), w_up.astype(jnp.bfloat16),
                     w_down.astype(jnp.bfloat16), b_gate, b_up, b_down, tm)
    yg = jnp.take(y, dest.reshape(n_tok, TOP_K).T, axis=0)
    return _moe_combine(yg, topg, x1, ln_g, ln_b, alpha)


def kernel(x, dn_w_in, dn_conv_w, dn_a_log, dn_dt_bias, dn_norm_w, dn_w_out, mb_w_in, mb_w_out, ln_g, ln_b,
           router_w, router_b, w_gate, b_gate, w_up, b_up, w_down, b_down):
    bsz, t_len, d = x.shape
    depth = ln_g.shape[0]
    alpha = float((2 * depth) ** 0.25)
    n = bsz * t_len
    xf = x.reshape(n, d)
    xb = xf.astype(jnp.bfloat16)
    for i in range(depth):
        j = i // 2
        if i % 2 == 0:
            n_vheads = d // HEAD_DIM
            main = dn_w_in.shape[2] - 2 * n_vheads
            proj = _matmul(xb, dn_w_in[j][:, :main].astype(jnp.bfloat16), jnp.bfloat16, 1024, 512)
            gates = _dn_gates(xf, dn_w_in[j][:, main:], dn_a_log[j], dn_dt_bias[j], n_vheads)
            gates_t = jnp.transpose(gates.reshape(bsz, t_len, LANES), (0, 2, 1))
            h = _dn_delta_rule(proj, gates, gates_t, dn_conv_w[j], dn_norm_w[j], bsz, t_len, d)
            w_out = dn_w_out[j]
        else:
            proj = _matmul(xb, mb_w_in[j].astype(jnp.bfloat16), jnp.bfloat16, 1024, 512)
            h = _moba_attention(proj, bsz, t_len, d // HEAD_DIM)
            w_out = mb_w_out[j]
        x1, x1b, topi, topg = _post_mixer(h, w_out.astype(jnp.bfloat16), xf, ln_g[i, 0], ln_b[i, 0],
                                          router_w[i], router_b[i], alpha)
        xf, xb = _moe_layer(x1, x1b, topi, topg, ln_g[i, 1], ln_b[i, 1], w_gate[i], b_gate[i], w_up[i], b_up[i],
                            w_down[i], b_down[i], alpha)
    return xf.reshape(bsz, t_len, d)
```

```python
import functools

import jax
import jax.numpy as jnp
import numpy as np
from jax import lax
from jax.experimental import pallas as pl
from jax.experimental.pallas import tpu as pltpu

HEAD_DIM = 128
DN_CONV = 4
DN_CHUNK = 256
MB_BLOCK = 256
MB_TOPK = 3
TOP_K = 4
SWIGLU_LIMIT = 7.0
SWIGLU_ALPHA = 1.702
LN_EPS = 1e-5
NORM_EPS = 1e-6
NEG_INF = -1e30
LANES = 128
HALO = 16
VMEM_LIMIT = 56 * 1024 * 1024

_HI = lax.Precision.HIGHEST


def _cparams(sem):
    return pltpu.CompilerParams(dimension_semantics=sem, vmem_limit_bytes=VMEM_LIMIT)


def _sigmoid(v):
    return 1.0 / (1.0 + jnp.exp(-v))


def _silu(v):
    return v * _sigmoid(v)


def _mm_kernel(x_ref, w_ref, o_ref):
    o_ref[...] = jnp.dot(x_ref[...], w_ref[...], preferred_element_type=jnp.float32).astype(o_ref.dtype)


def _matmul(x, w, out_dtype, tm, tn):
    m, k = x.shape
    n = w.shape[1]
    tm, tn = min(tm, m), min(tn, n)
    return pl.pallas_call(
        _mm_kernel,
        grid=(m // tm, n // tn),
        in_specs=[pl.BlockSpec((tm, k), lambda i, j: (i, 0)),
                  pl.BlockSpec((k, tn), lambda i, j: (0, j))],
        out_specs=pl.BlockSpec((tm, tn), lambda i, j: (i, j)),
        out_shape=jax.ShapeDtypeStruct((m, n), out_dtype),
        compiler_params=_cparams(("parallel", "arbitrary")),
        name="dense_matmul",
    )(x, w)


def _dn_gates_kernel(x_ref, w_ref, alog_ref, dtb_ref, o_ref, *, n_heads):
    tm = x_ref.shape[0]
    ba = jnp.dot(x_ref[...], w_ref[...], precision=_HI, preferred_element_type=jnp.float32)
    lane = lax.broadcasted_iota(jnp.int32, ba.shape, 1)
    beta = _sigmoid(ba)
    v = ba + dtb_ref[...]
    softplus = jnp.maximum(v, 0.0) + jnp.log(1.0 + jnp.exp(-jnp.abs(v)))
    g = -jnp.exp(alog_ref[...]) * softplus
    g = jnp.where((lane >= n_heads) & (lane < 2 * n_heads), g, 0.0)
    ri = lax.broadcasted_iota(jnp.int32, (tm, tm), 0)
    ci = lax.broadcasted_iota(jnp.int32, (tm, tm), 1)
    shift = DN_CHUNK.bit_length() - 1
    tril = jnp.where((ci <= ri) & ((ri >> shift) == (ci >> shift)), 1.0, 0.0)
    gc = jnp.dot(tril, g, precision=_HI, preferred_element_type=jnp.float32)
    o_ref[...] = jnp.where(lane < n_heads, beta, gc)


def _dn_gates(x2d, w_ba, a_log, dt_bias, n_heads):
    n, d = x2d.shape
    tm = min(256, n)
    pad = LANES - 2 * n_heads
    w_p = jnp.pad(w_ba, ((0, 0), (0, pad)))
    alog_p = jnp.pad(a_log, (n_heads, pad))[None, :]
    dtb_p = jnp.pad(dt_bias, (n_heads, pad))[None, :]
    return pl.pallas_call(
        functools.partial(_dn_gates_kernel, n_heads=n_heads),
        grid=(n // tm,),
        in_specs=[pl.BlockSpec((tm, d), lambda i: (i, 0)),
                  pl.BlockSpec((d, LANES), lambda i: (0, 0)),
                  pl.BlockSpec((1, LANES), lambda i: (0, 0)),
                  pl.BlockSpec((1, LANES), lambda i: (0, 0))],
        out_specs=pl.BlockSpec((tm, LANES), lambda i: (i, 0)),
        out_shape=jax.ShapeDtypeStruct((n, LANES), jnp.float32),
        compiler_params=_cparams(("parallel",)),
        name="dn_gates",
    )(x2d, w_p, alog_p, dtb_p)


def _causal_conv_silu(x_ref, halo_ref, w_ref, first):
    tb = x_ref.shape[0]
    halo = jnp.where(first, 0.0, halo_ref[...].astype(jnp.float32))
    xcat = jnp.concatenate([halo, x_ref[...].astype(jnp.float32)], axis=0)
    w = w_ref[...]
    acc = xcat[HALO:] * w[DN_CONV - 1:DN_CONV, :]
    for s in range(1, DN_CONV):
        acc = acc + pltpu.roll(xcat, s, 0)[HALO:] * w[DN_CONV - 1 - s:DN_CONV - s, :]
    del tb
    return _silu(acc)


def _l2norm(v):
    return v * lax.rsqrt(jnp.sum(v * v, axis=-1, keepdims=True) + NORM_EPS)


def _inverse_level_masks(c):
    ri = np.arange(c)[:, None]
    ci = np.arange(c)[None, :]
    masks = [((ri >> 1) == (ci >> 1)) & (ri > ci)]
    s = 2
    while s < c:
        masks.append(((ri // (2 * s)) == (ci // (2 * s))) & ((ri & s) != 0) & ((ci & s) == 0))
        s *= 2
    return jnp.asarray(np.stack(masks), jnp.bfloat16)


def _dn_kernel(q_ref, k_ref, v_ref, z_ref, qh_ref, kh_ref, vh_ref, g_ref, gt_ref,
               wq_ref, wk_ref, wv_ref, nw_ref, lm_ref, negc_ref, o_ref, state_ref, *, n_vheads):
    hk = pl.program_id(1)
    t = pl.program_id(2)
    first = t == 0
    tb = q_ref.shape[0]
    rep = v_ref.shape[1] // HEAD_DIM
    bf16, f32 = jnp.bfloat16, jnp.float32

    @pl.when(first)
    def _():
        state_ref[...] = jnp.zeros_like(state_ref)

    q = _l2norm(_causal_conv_silu(q_ref, qh_ref, wq_ref, first)) * (HEAD_DIM ** -0.5)
    k = _l2norm(_causal_conv_silu(k_ref, kh_ref, wk_ref, first))
    v = _causal_conv_silu(v_ref, vh_ref, wv_ref, first)
    gates = g_ref[...]
    lane = lax.broadcasted_iota(jnp.int32, gates.shape, 1)
    eye = jnp.where(lax.broadcasted_iota(jnp.int32, (tb, tb), 0) == lax.broadcasted_iota(jnp.int32, (tb, tb), 1),
                    1.0, 0.0)
    nt = (((1,), (1,)), ((), ()))
    q_b = q.astype(bf16)
    k_b = k.astype(bf16)
    kk = lax.dot_general(k_b, k_b, nt, preferred_element_type=f32)
    qk = lax.dot_general(q_b, k_b, nt, preferred_element_type=f32)

    heads = range(rep)
    beta, gcol, g_last, decay, a_b, inv = [], [], [], [], [], []
    for j in heads:
        hv = hk * rep + j
        beta.append(jnp.sum(jnp.where(lane == hv, gates, 0.0), axis=-1, keepdims=True))
        gcol.append(jnp.sum(jnp.where(lane == n_vheads + hv, gates, 0.0), axis=-1, keepdims=True))
        grow = gt_ref[pl.ds(n_vheads + hv, 1), :]
        g_last.append(grow[:, tb - 1:tb])
        decay.append(jnp.exp(gcol[j] - grow + negc_ref[...]))
        a_b.append((kk * beta[j] * decay[j]).astype(bf16))
        inv.append(eye - (a_b[j] * lm_ref[0]).astype(f32))
    for lvl in range(1, lm_ref.shape[0]):
        tmp = [jnp.dot(a_b[j] * lm_ref[lvl], inv[j].astype(bf16), preferred_element_type=f32) for j in heads]
        inv = [inv[j] - jnp.dot(inv[j].astype(bf16), tmp[j].astype(bf16), preferred_element_type=f32)
               for j in heads]
    eg = [jnp.exp(gcol[j]) for j in heads]
    uw = [jnp.dot(inv[j].astype(bf16),
                  jnp.concatenate([v[:, j * HEAD_DIM:(j + 1) * HEAD_DIM] * beta[j], k * (beta[j] * eg[j])],
                                  axis=1).astype(bf16), preferred_element_type=f32) for j in heads]
    state = [state_ref[j] for j in heads]
    ws = [jnp.dot(jnp.concatenate([uw[j][:, HEAD_DIM:], q * eg[j]], axis=0).astype(bf16), state[j].astype(bf16),
                  preferred_element_type=f32) for j in heads]
    v_new_b = [(uw[j][:, :HEAD_DIM] - ws[j][:tb]).astype(bf16) for j in heads]
    outs = []
    for j in heads:
        o = ws[j][tb:] + jnp.dot((qk * decay[j]).astype(bf16), v_new_b[j], preferred_element_type=f32)
        k_dec_t = (k * jnp.exp(g_last[j] - gcol[j])).T.astype(bf16)
        state_ref[j] = state[j] * jnp.exp(g_last[j]) + jnp.dot(k_dec_t, v_new_b[j], preferred_element_type=f32)
        zj = z_ref[:, j * HEAD_DIM:(j + 1) * HEAD_DIM].astype(f32)
        outs.append(o * lax.rsqrt(jnp.mean(o * o, axis=-1, keepdims=True) + NORM_EPS) * nw_ref[...] * _silu(zj))
    o_ref[...] = jnp.concatenate(outs, axis=1).astype(o_ref.dtype)


def _dn_delta_rule(proj, gates, gates_t, conv_w, norm_w, bsz, t_len, d_model):
    n_kheads = d_model // (2 * HEAD_DIM)
    n_vheads = d_model // HEAD_DIM
    rep = n_vheads // n_kheads
    vw = rep * HEAD_DIM
    tb = DN_CHUNK
    nt = t_len // tb
    hb = tb // HALO
    level_masks = _inverse_level_masks(tb)
    ri = np.arange(tb)
    neg_causal = jnp.asarray(np.where(ri[None, :] <= ri[:, None], 0.0, NEG_INF), jnp.float32)
    k_off = n_kheads
    v_off = 2 * n_kheads * HEAD_DIM // vw
    z_off = v_off + n_vheads * HEAD_DIM // vw

    def row(b, h, t):
        return b * nt + t

    def halo_row(b, h, t):
        return jnp.maximum((b * nt + t) * hb - 1, 0)

    in_specs = [
        pl.BlockSpec((tb, HEAD_DIM), lambda b, h, t: (row(b, h, t), h)),
        pl.BlockSpec((tb, HEAD_DIM), lambda b, h, t: (row(b, h, t), k_off + h)),
        pl.BlockSpec((tb, vw), lambda b, h, t: (row(b, h, t), v_off + h)),
        pl.BlockSpec((tb, vw), lambda b, h, t: (row(b, h, t), z_off + h)),
        pl.BlockSpec((HALO, HEAD_DIM), lambda b, h, t: (halo_row(b, h, t), h)),
        pl.BlockSpec((HALO, HEAD_DIM), lambda b, h, t: (halo_row(b, h, t), k_off + h)),
        pl.BlockSpec((HALO, vw), lambda b, h, t: (halo_row(b, h, t), v_off + h)),
        pl.BlockSpec((tb, LANES), lambda b, h, t: (row(b, h, t), 0)),
        pl.BlockSpec((None, LANES, tb), lambda b, h, t: (b, 0, t)),
        pl.BlockSpec((DN_CONV, HEAD_DIM), lambda b, h, t: (0, h)),
        pl.BlockSpec((DN_CONV, HEAD_DIM), lambda b, h, t: (0, k_off + h)),
        pl.BlockSpec((DN_CONV, vw), lambda b, h, t: (0, v_off + h)),
        pl.BlockSpec((1, HEAD_DIM), lambda b, h, t: (0, 0)),
        pl.BlockSpec(level_masks.shape, lambda b, h, t: (0, 0, 0)),
        pl.BlockSpec((tb, tb), lambda b, h, t: (0, 0)),
    ]
    return pl.pallas_call(
        functools.partial(_dn_kernel, n_vheads=n_vheads),
        grid=(bsz, n_kheads, nt),
        in_specs=in_specs,
        out_specs=pl.BlockSpec((tb, vw), lambda b, h, t: (row(b, h, t), h)),
        out_shape=jax.ShapeDtypeStruct((bsz * t_len, n_vheads * HEAD_DIM), jnp.bfloat16),
        scratch_shapes=[pltpu.VMEM((rep, HEAD_DIM, HEAD_DIM), jnp.float32)],
        compiler_params=_cparams(("parallel", "parallel", "arbitrary")),
        name="dn_delta_rule",
    )(proj, proj, proj, proj, proj, proj, proj, gates, gates_t, conv_w, conv_w, conv_w, norm_w[None, :],
      level_masks, neg_causal)


def _split3(a):
    hi = a.astype(jnp.bfloat16).astype(jnp.float32)
    mid = (a - hi).astype(jnp.bfloat16).astype(jnp.float32)
    lo = (a - hi - mid).astype(jnp.bfloat16).astype(jnp.float32)
    return hi, mid, lo


def _moba_kernel(slope_ref, q_ref, k_ref, v_ref, o_ref, kmean_ref, qaug_ref, kaug_ref):
    h = pl.program_id(1)
    own = pl.program_id(2)
    tq = q_ref.shape[0]
    t_len = k_ref.shape[0]
    nb = t_len // MB_BLOCK
    nt = (((1,), (1,)), ((), ()))
    log2e = 1.4426950408889634
    slope2 = slope_ref[h] * log2e

    @pl.when(own == 0)
    def _():
        kf = k_ref[...].astype(jnp.float32).reshape(nb, MB_BLOCK, HEAD_DIM)
        kmean_ref[...] = jnp.sum(kf, axis=1) * (1.0 / MB_BLOCK)
        pos = lax.broadcasted_iota(jnp.int32, (MB_BLOCK, LANES), 0).astype(jnp.float32)
        col = lax.broadcasted_iota(jnp.int32, (MB_BLOCK, LANES), 1)
        q_hi, q_mid, q_lo = _split3(pos * (-slope2))
        k_hi, k_mid, k_lo = _split3(pos * slope2)
        one = jnp.where(col < 6, 1.0, 0.0)
        qaug = jnp.where(col == 0, q_hi, jnp.where(col == 1, q_mid, jnp.where(col == 2, q_lo, one)))
        kaug = jnp.where(col == 3, k_hi, jnp.where(col == 4, k_mid, jnp.where(col == 5, k_lo, one)))
        qaug_ref[...] = qaug.astype(qaug_ref.dtype)
        kaug_ref[...] = kaug.astype(kaug_ref.dtype)

    qf = q_ref[...].astype(jnp.float32)
    gate = lax.dot_general(qf, kmean_ref[...], nt, precision=_HI, preferred_element_type=jnp.float32)
    blk = lax.broadcasted_iota(jnp.int32, gate.shape, 1).astype(jnp.float32)
    gate = jnp.where(blk < own.astype(jnp.float32), gate, -jnp.inf)
    picks = jnp.zeros((tq, 1), jnp.int32)
    for r in range(MB_TOPK):
        mx = jnp.max(gate, axis=-1, keepdims=True)
        idx = jnp.min(jnp.where(gate == mx, blk, float(nb)), axis=-1, keepdims=True)
        bit = jnp.left_shift(1, idx.astype(jnp.int32))
        picks = picks | jnp.where(r < own, bit, 0)
        gate = jnp.where(blk == idx, -jnp.inf, gate)

    q2 = jnp.concatenate([(qf * (HEAD_DIM ** -0.5 * log2e)).astype(jnp.bfloat16), qaug_ref[...]], axis=1)
    kaug = kaug_ref[...]

    def scores(n):
        start = pl.multiple_of(n * MB_BLOCK, MB_BLOCK)
        k2 = jnp.concatenate([k_ref[pl.ds(start, MB_BLOCK), :], kaug], axis=1)
        return lax.dot_general(q2, k2, nt, preferred_element_type=jnp.float32)

    def values(n):
        return v_ref[pl.ds(pl.multiple_of(n * MB_BLOCK, MB_BLOCK), MB_BLOCK), :]

    def past_scores(n):
        picked = (picks & jnp.left_shift(1, n)) != 0
        far = (own - n).astype(jnp.float32) * (-slope2 * MB_BLOCK)
        return scores(n) + jnp.where(picked, far, NEG_INF)

    def lane_fold(p):
        return sum(p[:, c * LANES:(c + 1) * LANES] for c in range(MB_BLOCK // LANES))

    def pair_scores(i):
        return past_scores(jnp.minimum(2 * i, nb - 1)), past_scores(jnp.minimum(2 * i + 1, nb - 1))

    def past_pair(i, carry):
        m, l, acc, s0, s1 = carry
        nxt0, nxt1 = pair_scores(i + 1)
        m_new = jnp.maximum(m, jnp.max(jnp.maximum(s0, s1), axis=-1, keepdims=True))
        p0, p1 = jnp.exp2(s0 - m_new), jnp.exp2(s1 - m_new)
        alpha = jnp.exp2(m - m_new)
        l = alpha * l + lane_fold(p0) + lane_fold(p1)
        acc = (alpha * acc + jnp.dot(p0.astype(jnp.bfloat16), values(2 * i), preferred_element_type=jnp.float32)
               + jnp.dot(p1.astype(jnp.bfloat16), values(2 * i + 1), preferred_element_type=jnp.float32))
        return m_new, l, acc, nxt0, nxt1

    m0 = jnp.full((tq, 1), NEG_INF, jnp.float32)
    l0 = jnp.zeros((tq, LANES), jnp.float32)
    a0 = jnp.zeros((tq, HEAD_DIM), jnp.float32)
    m, l, acc, _, _ = lax.fori_loop(0, (own + 1) // 2, past_pair, (m0, l0, a0) + pair_scores(0))
    rq = lax.broadcasted_iota(jnp.int32, (tq, MB_BLOCK), 0)
    ck = lax.broadcasted_iota(jnp.int32, (tq, MB_BLOCK), 1)
    s = jnp.where(rq >= ck, scores(own), NEG_INF)
    m_new = jnp.maximum(m, jnp.max(s, axis=-1, keepdims=True))
    p = jnp.exp2(s - m_new)
    alpha = jnp.exp2(m - m_new)
    l = jnp.sum(alpha * l + lane_fold(p), axis=-1, keepdims=True)
    acc = alpha * acc + jnp.dot(p.astype(jnp.bfloat16), values(own), preferred_element_type=jnp.float32)
    o_ref[...] = (acc / l).astype(o_ref.dtype)


def _moba_attention(proj, bsz, t_len, n_heads):
    tq = MB_BLOCK
    nq = t_len // tq
    nb = t_len // MB_BLOCK
    assert nb <= 32, "per-row picks are kept as one int32 bit mask"
    slopes = jnp.exp2(-8.0 * jnp.arange(1, n_heads + 1, dtype=jnp.float32) / n_heads)
    grid_spec = pltpu.PrefetchScalarGridSpec(
        num_scalar_prefetch=1,
        grid=(bsz, n_heads, nq),
        in_specs=[pl.BlockSpec((tq, HEAD_DIM), lambda b, h, i, s: (b * nq + i, h)),
                  pl.BlockSpec((t_len, HEAD_DIM), lambda b, h, i, s: (b, n_heads + h)),
                  pl.BlockSpec((t_len, HEAD_DIM), lambda b, h, i, s: (b, 2 * n_heads + h))],
        out_specs=pl.BlockSpec((tq, HEAD_DIM), lambda b, h, i, s: (b * nq + i, h)),
        scratch_shapes=[pltpu.VMEM((nb, HEAD_DIM), jnp.float32),
                        pltpu.VMEM((MB_BLOCK, LANES), jnp.bfloat16),
                        pltpu.VMEM((MB_BLOCK, LANES), jnp.bfloat16)],
    )
    return pl.pallas_call(
        _moba_kernel,
        grid_spec=grid_spec,
        out_shape=jax.ShapeDtypeStruct((bsz * t_len, n_heads * HEAD_DIM), jnp.bfloat16),
        compiler_params=_cparams(("parallel", "parallel", "arbitrary")),
        name="moba_attention",
    )(slopes, proj, proj, proj)


def _layer_norm_rows(y, g, b):
    mu = jnp.mean(y, axis=-1, keepdims=True)
    yc = y - mu
    var = jnp.mean(yc * yc, axis=-1, keepdims=True)
    return yc * lax.rsqrt(var + LN_EPS) * g + b


def _post_mixer_kernel(h_ref, w_ref, xres_ref, g_ref, b_ref, rw_ref, rb_ref,
                       x1_ref, x1b_ref, topi_ref, topg_ref, *, alpha, n_experts):
    j = pl.program_id(1)
    tn = w_ref.shape[1]
    col = pl.multiple_of(j * tn, tn)
    x1_ref[:, pl.ds(col, tn)] = jnp.dot(h_ref[...], w_ref[...], preferred_element_type=jnp.float32)

    @pl.when(j == pl.num_programs(1) - 1)
    def _():
        x1 = _layer_norm_rows(alpha * xres_ref[...] + x1_ref[...], g_ref[...], b_ref[...])
        x1_ref[...] = x1
        x1b_ref[...] = x1.astype(x1b_ref.dtype)
        logits = jnp.dot(x1, rw_ref[...], precision=_HI, preferred_element_type=jnp.float32) + rb_ref[...]
        lane = lax.broadcasted_iota(jnp.int32, logits.shape, 1)
        logits = jnp.where(lane < n_experts, logits, -jnp.inf)
        topi = jnp.zeros(logits.shape, jnp.int32)
        topv = jnp.full(logits.shape, -jnp.inf, jnp.float32)
        for r in range(TOP_K):
            mx = jnp.max(logits, axis=-1, keepdims=True)
            idx = jnp.min(jnp.where(logits == mx, lane, LANES), axis=-1, keepdims=True)
            topi = jnp.where(lane == r, idx, topi)
            topv = jnp.where(lane == r, mx, topv)
            logits = jnp.where(lane == idx, -jnp.inf, logits)
        e = jnp.exp(topv - jnp.max(topv, axis=-1, keepdims=True))
        topi_ref[...] = topi
        topg_ref[...] = e / jnp.sum(e, axis=-1, keepdims=True)


def _post_mixer(h, w_out, xres, ln_g, ln_b, router_w, router_b, alpha):
    n, kdim = h.shape
    d = w_out.shape[1]
    n_experts = router_w.shape[1]
    tm, tn = min(256, n), min(512, d)
    rw = jnp.pad(router_w, ((0, 0), (0, LANES - n_experts)))
    rb = jnp.pad(router_b, (0, LANES - n_experts))[None, :]
    row = lambda i, j: (i, 0)
    fixed = lambda i, j: (0, 0)
    return pl.pallas_call(
        functools.partial(_post_mixer_kernel, alpha=alpha, n_experts=n_experts),
        grid=(n // tm, d // tn),
        in_specs=[pl.BlockSpec((tm, kdim), row),
                  pl.BlockSpec((kdim, tn), lambda i, j: (0, j)),
                  pl.BlockSpec((tm, d), row),
                  pl.BlockSpec((1, d), fixed),
                  pl.BlockSpec((1, d), fixed),
                  pl.BlockSpec((d, LANES), fixed),
                  pl.BlockSpec((1, LANES), fixed)],
        out_specs=[pl.BlockSpec((tm, d), row),
                   pl.BlockSpec((tm, d), row),
                   pl.BlockSpec((tm, LANES), row),
                   pl.BlockSpec((tm, LANES), row)],
        out_shape=[jax.ShapeDtypeStruct((n, d), jnp.float32),
                   jax.ShapeDtypeStruct((n, d), jnp.bfloat16),
                   jax.ShapeDtypeStruct((n, LANES), jnp.int32),
                   jax.ShapeDtypeStruct((n, LANES), jnp.float32)],
        compiler_params=_cparams(("parallel", "arbitrary")),
        name="post_mixer",
    )(h, w_out, xres, ln_g[None, :], ln_b[None, :], rw, rb)


def _moe_kernel(be_ref, nu_ref, xs_ref, wg_ref, wu_ref, wd_ref, bg_ref, bu_ref, bd_ref, y_ref, acc_ref):
    i = pl.program_id(0)
    j = pl.program_id(1)
    last = pl.num_programs(1) - 1
    used = i < nu_ref[0]

    @pl.when(used)
    def _():
        xs = xs_ref[...]
        h_gate = jnp.dot(xs, wg_ref[...], preferred_element_type=jnp.float32) + bg_ref[...]
        h_up = jnp.dot(xs, wu_ref[...], preferred_element_type=jnp.float32) + bu_ref[...]
        h_gate = jnp.minimum(h_gate, SWIGLU_LIMIT)
        h_up = jnp.clip(h_up, -SWIGLU_LIMIT, SWIGLU_LIMIT)
        hid = (h_up + 1.0) * (h_gate * _sigmoid(SWIGLU_ALPHA * h_gate))
        part = jnp.dot(hid.astype(jnp.bfloat16), wd_ref[...], preferred_element_type=jnp.float32)

        @pl.when(j == 0)
        def _():
            acc_ref[...] = part

        @pl.when(j > 0)
        def _():
            acc_ref[...] += part

        @pl.when(j == last)
        def _():
            y_ref[...] = (acc_ref[...] + bd_ref[...]).astype(y_ref.dtype)

    @pl.when(jnp.logical_not(used) & (j == last))
    def _():
        y_ref[...] = jnp.zeros_like(y_ref)


def _moe_experts(xs, block_e, n_used, w_gate, w_up, w_down, b_gate, b_up, b_down, tm):
    n_slots, d = xs.shape
    n_experts, _, f = w_gate.shape
    tf = min(256, f)
    nf = f // tf
    n_blocks = n_slots // tm

    def jj(i, j, nu):
        return jnp.where(i < nu[0], j, nf - 1)

    grid_spec = pltpu.PrefetchScalarGridSpec(
        num_scalar_prefetch=2,
        grid=(n_blocks, nf),
        in_specs=[pl.BlockSpec((tm, d), lambda i, j, be, nu: (jnp.minimum(i, nu[0] - 1), 0)),
                  pl.BlockSpec((None, d, tf), lambda i, j, be, nu: (be[i], 0, jj(i, j, nu))),
                  pl.BlockSpec((None, d, tf), lambda i, j, be, nu: (be[i], 0, jj(i, j, nu))),
                  pl.BlockSpec((None, tf, d), lambda i, j, be, nu: (be[i], jj(i, j, nu), 0)),
                  pl.BlockSpec((None, 1, tf), lambda i, j, be, nu: (be[i], 0, jj(i, j, nu))),
                  pl.BlockSpec((None, 1, tf), lambda i, j, be, nu: (be[i], 0, jj(i, j, nu))),
                  pl.BlockSpec((None, 1, d), lambda i, j, be, nu: (be[i], 0, 0))],
        out_specs=pl.BlockSpec((tm, d), lambda i, j, be, nu: (i, 0)),
        scratch_shapes=[pltpu.VMEM((tm, d), jnp.float32)],
    )
    return pl.pallas_call(
        _moe_kernel,
        grid_spec=grid_spec,
        out_shape=jax.ShapeDtypeStruct((n_slots, d), jnp.bfloat16),
        compiler_params=_cparams(("arbitrary", "arbitrary")),
        name="moe_experts",
    )(block_e, n_used, xs, w_gate, w_up, w_down, b_gate[:, None, :], b_up[:, None, :], b_down[:, None, :])


def _combine_kernel(yg_ref, gate_ref, x1_ref, g_ref, b_ref, o_ref, ob_ref, *, alpha):
    gates = gate_ref[...]
    f = yg_ref[0].astype(jnp.float32) * gates[:, 0:1]
    for r in range(1, TOP_K):
        f = f + yg_ref[r].astype(jnp.float32) * gates[:, r:r + 1]
    out = _layer_norm_rows(alpha * x1_ref[...] + f, g_ref[...], b_ref[...])
    o_ref[...] = out
    ob_ref[...] = out.astype(ob_ref.dtype)


def _moe_combine(yg, gates, x1, ln_g, ln_b, alpha):
    n, d = x1.shape
    tm = min(256, n)
    return pl.pallas_call(
        functools.partial(_combine_kernel, alpha=alpha),
        grid=(n // tm,),
        in_specs=[pl.BlockSpec((TOP_K, tm, d), lambda i: (0, i, 0)),
                  pl.BlockSpec((tm, LANES), lambda i: (i, 0)),
                  pl.BlockSpec((tm, d), lambda i: (i, 0)),
                  pl.BlockSpec((1, d), lambda i: (0, 0)),
                  pl.BlockSpec((1, d), lambda i: (0, 0))],
        out_specs=[pl.BlockSpec((tm, d), lambda i: (i, 0)),
                   pl.BlockSpec((tm, d), lambda i: (i, 0))],
        out_shape=[jax.ShapeDtypeStruct((n, d), jnp.float32),
                   jax.ShapeDtypeStruct((n, d), jnp.bfloat16)],
        compiler_params=_cparams(("parallel",)),
        name="moe_combine",
    )(yg, gates, x1, ln_g[None, :], ln_b[None, :])


def _moe_layer(x1, x1b, topi, topg, ln_g, ln_b, w_gate, b_gate, w_up, b_up, w_down, b_down, alpha):
    n_tok, d = x1.shape
    n_experts = w_gate.shape[0]
    n_assign = n_tok * TOP_K
    tm = min(512, n_assign // n_experts)
    flat_e = topi[:, :TOP_K].reshape(-1)
    onehot = (flat_e[:, None] == jnp.arange(n_experts, dtype=jnp.int32)[None, :]).astype(jnp.int32)
    csum = jnp.cumsum(onehot, axis=0)
    rank = jnp.take_along_axis(csum, flat_e[:, None], axis=1)[:, 0] - 1
    counts = csum[-1]
    padded = (counts + tm - 1) // tm * tm
    pends = jnp.cumsum(padded)
    dest = (pends - padded)[flat_e] + rank
    n_blocks = n_assign // tm + n_experts
    flat_tok = jnp.arange(n_assign, dtype=jnp.int32) // TOP_K
    rows = jnp.zeros((n_blocks * tm,), jnp.int32).at[dest].set(flat_tok)
    block_e = jnp.minimum(jnp.searchsorted(pends, jnp.arange(n_blocks, dtype=jnp.int32) * tm, side='right'),
                          n_experts - 1).astype(jnp.int32)
    n_used = (pends[-1:] // tm).astype(jnp.int32)
    xs = jnp.take(x1b, rows, axis=0)
    y = _moe_experts(xs, block_e, n_used, w_gate.astype(jnp.bfloat16), w_up.astype(jnp.bfloat16),
                     w_down.astype(jnp.bfloat16), b_gate, b_up, b_down, tm)
    yg = jnp.take(y, dest.reshape(n_tok, TOP_K).T, axis=0)
    return _moe_combine(yg, topg, x1, ln_g, ln_b, alpha)


def kernel(x, dn_w_in, dn_conv_w, dn_a_log, dn_dt_bias, dn_norm_w, dn_w_out, mb_w_in, mb_w_out, ln_g, ln_b,
           router_w, router_b, w_gate, b_gate, w_up, b_up, w_down, b_down):
    bsz, t_len, d = x.shape
    depth = ln_g.shape[0]
    alpha = float((2 * depth) ** 0.25)
    n = bsz * t_len
    xf = x.reshape(n, d)
    xb = xf.astype(jnp.bfloat16)
    for i in range(depth):
        j = i // 2
        if i % 2 == 0:
            n_vheads = d // HEAD_DIM
            main = dn_w_in.shape[2] - 2 * n_vheads
            proj = _matmul(xb, dn_w_in[j][:, :main].astype(jnp.bfloat16), jnp.bfloat16, 1024, 512)
            gates = _dn_gates(xf, dn_w_in[j][:, main:], dn_a_log[j], dn_dt_bias[j], n_vheads)
            gates_t = jnp.transpose(gates.reshape(bsz, t_len, LANES), (0, 2, 1))
            h = _dn_delta_rule(proj, gates, gates_t, dn_conv_w[j], dn_norm_w[j], bsz, t_len, d)
            w_out = dn_w_out[j]
        else:
            proj = _matmul(xb, mb_w_in[j].astype(jnp.bfloat16), jnp.bfloat16, 1024, 512)
            h = _moba_attention(proj, bsz, t_len, d // HEAD_DIM)
            w_out = mb_w_out[j]
        x1, x1b, topi, topg = _post_mixer(h, w_out.astype(jnp.bfloat16), xf, ln_g[i, 0], ln_b[i, 0],
                                          router_w[i], router_b[i], alpha)
        xf, xb = _moe_layer(x1, x1b, topi, topg, ln_g[i, 1], ln_b[i, 1], w_gate[i], b_gate[i], w_up[i], b_up[i],
                            w_down[i], b_down[i], alpha)
    return xf.reshape(bsz, t_len, d)
```

```python
import functools

import jax
import jax.numpy as jnp
import numpy as np
from jax import lax
from jax.experimental import pallas as pl
from jax.experimental.pallas import tpu as pltpu

HEAD_DIM = 128
DN_CONV = 4
DN_CHUNK = 256
MB_BLOCK = 256
MB_TOPK = 3
TOP_K = 4
SWIGLU_LIMIT = 7.0
SWIGLU_ALPHA = 1.702
LN_EPS = 1e-5
NORM_EPS = 1e-6
NEG_INF = -1e30
LANES = 128
HALO = 16
VMEM_LIMIT = 56 * 1024 * 1024

_HI = lax.Precision.HIGHEST


def _cparams(sem):
    return pltpu.CompilerParams(dimension_semantics=sem, vmem_limit_bytes=VMEM_LIMIT)


def _sigmoid(v):
    return 1.0 / (1.0 + jnp.exp(-v))


def _silu(v):
    return v * _sigmoid(v)


def _mm_kernel(x_ref, w_ref, o_ref):
    o_ref[...] = jnp.dot(x_ref[...], w_ref[...], preferred_element_type=jnp.float32).astype(o_ref.dtype)


def _matmul(x, w, out_dtype, tm, tn):
    m, k = x.shape
    n = w.shape[1]
    tm, tn = min(tm, m), min(tn, n)
    return pl.pallas_call(
        _mm_kernel,
        grid=(m // tm, n // tn),
        in_specs=[pl.BlockSpec((tm, k), lambda i, j: (i, 0)),
                  pl.BlockSpec((k, tn), lambda i, j: (0, j))],
        out_specs=pl.BlockSpec((tm, tn), lambda i, j: (i, j)),
        out_shape=jax.ShapeDtypeStruct((m, n), out_dtype),
        compiler_params=_cparams(("parallel", "arbitrary")),
        name="dense_matmul",
    )(x, w)


def _dn_gates_kernel(x_ref, w_ref, alog_ref, dtb_ref, o_ref, *, n_heads):
    tm = x_ref.shape[0]
    ba = jnp.dot(x_ref[...], w_ref[...], precision=_HI, preferred_element_type=jnp.float32)
    lane = lax.broadcasted_iota(jnp.int32, ba.shape, 1)
    beta = _sigmoid(ba)
    v = ba + dtb_ref[...]
    softplus = jnp.maximum(v, 0.0) + jnp.log(1.0 + jnp.exp(-jnp.abs(v)))
    g = -jnp.exp(alog_ref[...]) * softplus
    g = jnp.where((lane >= n_heads) & (lane < 2 * n_heads), g, 0.0)
    ri = lax.broadcasted_iota(jnp.int32, (tm, tm), 0)
    ci = lax.broadcasted_iota(jnp.int32, (tm, tm), 1)
    shift = DN_CHUNK.bit_length() - 1
    tril = jnp.where((ci <= ri) & ((ri >> shift) == (ci >> shift)), 1.0, 0.0)
    gc = jnp.dot(tril, g, precision=_HI, preferred_element_type=jnp.float32)
    o_ref[...] = jnp.where(lane < n_heads, beta, gc)


def _dn_gates(x2d, w_ba, a_log, dt_bias, n_heads):
    n, d = x2d.shape
    tm = min(256, n)
    pad = LANES - 2 * n_heads
    w_p = jnp.pad(w_ba, ((0, 0), (0, pad)))
    alog_p = jnp.pad(a_log, (n_heads, pad))[None, :]
    dtb_p = jnp.pad(dt_bias, (n_heads, pad))[None, :]
    return pl.pallas_call(
        functools.partial(_dn_gates_kernel, n_heads=n_heads),
        grid=(n // tm,),
        in_specs=[pl.BlockSpec((tm, d), lambda i: (i, 0)),
                  pl.BlockSpec((d, LANES), lambda i: (0, 0)),
                  pl.BlockSpec((1, LANES), lambda i: (0, 0)),
                  pl.BlockSpec((1, LANES), lambda i: (0, 0))],
        out_specs=pl.BlockSpec((tm, LANES), lambda i: (i, 0)),
        out_shape=jax.ShapeDtypeStruct((n, LANES), jnp.float32),
        compiler_params=_cparams(("parallel",)),
        name="dn_gates",
    )(x2d, w_p, alog_p, dtb_p)


def _causal_conv_silu(x_ref, halo_ref, w_ref, first):
    tb = x_ref.shape[0]
    halo = jnp.where(first, 0.0, halo_ref[...].astype(jnp.float32))
    xcat = jnp.concatenate([halo, x_ref[...].astype(jnp.float32)], axis=0)
    w = w_ref[...]
    acc = xcat[HALO:] * w[DN_CONV - 1:DN_CONV, :]
    for s in range(1, DN_CONV):
        acc = acc + pltpu.roll(xcat, s, 0)[HALO:] * w[DN_CONV - 1 - s:DN_CONV - s, :]
    del tb
    return _silu(acc)


def _l2norm(v):
    return v * lax.rsqrt(jnp.sum(v * v, axis=-1, keepdims=True) + NORM_EPS)


def _inverse_level_masks(c):
    ri = np.arange(c)[:, None]
    ci = np.arange(c)[None, :]
    masks = [((ri >> 1) == (ci >> 1)) & (ri > ci)]
    s = 2
    while s < c:
        masks.append(((ri // (2 * s)) == (ci // (2 * s))) & ((ri & s) != 0) & ((ci & s) == 0))
        s *= 2
    return jnp.asarray(np.stack(masks), jnp.bfloat16)


def _dn_kernel(q_ref, k_ref, v_ref, z_ref, qh_ref, kh_ref, vh_ref, g_ref, gt_ref,
               wq_ref, wk_ref, wv_ref, nw_ref, lm_ref, negc_ref, o_ref, state_ref, *, n_vheads):
    hk = pl.program_id(1)
    t = pl.program_id(2)
    first = t == 0
    tb = q_ref.shape[0]
    rep = v_ref.shape[1] // HEAD_DIM
    bf16, f32 = jnp.bfloat16, jnp.float32

    @pl.when(first)
    def _():
        state_ref[...] = jnp.zeros_like(state_ref)

    q = _l2norm(_causal_conv_silu(q_ref, qh_ref, wq_ref, first)) * (HEAD_DIM ** -0.5)
    k = _l2norm(_causal_conv_silu(k_ref, kh_ref, wk_ref, first))
    v = _causal_conv_silu(v_ref, vh_ref, wv_ref, first)
    gates = g_ref[...]
    lane = lax.broadcasted_iota(jnp.int32, gates.shape, 1)
    eye = jnp.where(lax.broadcasted_iota(jnp.int32, (tb, tb), 0) == lax.broadcasted_iota(jnp.int32, (tb, tb), 1),
                    1.0, 0.0)
    nt = (((1,), (1,)), ((), ()))
    q_b = q.astype(bf16)
    k_b = k.astype(bf16)
    kk = lax.dot_general(k_b, k_b, nt, preferred_element_type=f32)
    qk = lax.dot_general(q_b, k_b, nt, preferred_element_type=f32)

    heads = range(rep)
    beta, gcol, g_last, decay, a_b, inv = [], [], [], [], [], []
    for j in heads:
        hv = hk * rep + j
        beta.append(jnp.sum(jnp.where(lane == hv, gates, 0.0), axis=-1, keepdims=True))
        gcol.append(jnp.sum(jnp.where(lane == n_vheads + hv, gates, 0.0), axis=-1, keepdims=True))
        grow = gt_ref[pl.ds(n_vheads + hv, 1), :]
        g_last.append(grow[:, tb - 1:tb])
        decay.append(jnp.exp(gcol[j] - grow + negc_ref[...]))
        a_b.append((kk * beta[j] * decay[j]).astype(bf16))
        inv.append(eye - (a_b[j] * lm_ref[0]).astype(f32))
    for lvl in range(1, lm_ref.shape[0]):
        tmp = [jnp.dot(a_b[j] * lm_ref[lvl], inv[j].astype(bf16), preferred_element_type=f32) for j in heads]
        inv = [inv[j] - jnp.dot(inv[j].astype(bf16), tmp[j].astype(bf16), preferred_element_type=f32)
               for j in heads]
    eg = [jnp.exp(gcol[j]) for j in heads]
    uw = [jnp.dot(inv[j].astype(bf16),
                  jnp.concatenate([v[:, j * HEAD_DIM:(j + 1) * HEAD_DIM] * beta[j], k * (beta[j] * eg[j])],
                                  axis=1).astype(bf16), preferred_element_type=f32) for j in heads]
    state = [state_ref[j] for j in heads]
    ws = [jnp.dot(jnp.concatenate([uw[j][:, HEAD_DIM:], q * eg[j]], axis=0).astype(bf16), state[j].astype(bf16),
                  preferred_element_type=f32) for j in heads]
    v_new_b = [(uw[j][:, :HEAD_DIM] - ws[j][:tb]).astype(bf16) for j in heads]
    outs = []
    for j in heads:
        o = ws[j][tb:] + jnp.dot((qk * decay[j]).astype(bf16), v_new_b[j], preferred_element_type=f32)
        k_dec_t = (k * jnp.exp(g_last[j] - gcol[j])).T.astype(bf16)
        state_ref[j] = state[j] * jnp.exp(g_last[j]) + jnp.dot(k_dec_t, v_new_b[j], preferred_element_type=f32)
        zj = z_ref[:, j * HEAD_DIM:(j + 1) * HEAD_DIM].astype(f32)
        outs.append(o * lax.rsqrt(jnp.mean(o * o, axis=-1, keepdims=True) + NORM_EPS) * nw_ref[...] * _silu(zj))
    o_ref[...] = jnp.concatenate(outs, axis=1).astype(o_ref.dtype)


def _dn_delta_rule(proj, gates, gates_t, conv_w, norm_w, bsz, t_len, d_model):
    n_kheads = d_model // (2 * HEAD_DIM)
    n_vheads = d_model // HEAD_DIM
    rep = n_vheads // n_kheads
    vw = rep * HEAD_DIM
    tb = DN_CHUNK
    nt = t_len // tb
    hb = tb // HALO
    level_masks = _inverse_level_masks(tb)
    ri = np.arange(tb)
    neg_causal = jnp.asarray(np.where(ri[None, :] <= ri[:, None], 0.0, NEG_INF), jnp.float32)
    k_off = n_kheads
    v_off = 2 * n_kheads * HEAD_DIM // vw
    z_off = v_off + n_vheads * HEAD_DIM // vw

    def row(b, h, t):
        return b * nt + t

    def halo_row(b, h, t):
        return jnp.maximum((b * nt + t) * hb - 1, 0)

    in_specs = [
        pl.BlockSpec((tb, HEAD_DIM), lambda b, h, t: (row(b, h, t), h)),
        pl.BlockSpec((tb, HEAD_DIM), lambda b, h, t: (row(b, h, t), k_off + h)),
        pl.BlockSpec((tb, vw), lambda b, h, t: (row(b, h, t), v_off + h)),
        pl.BlockSpec((tb, vw), lambda b, h, t: (row(b, h, t), z_off + h)),
        pl.BlockSpec((HALO, HEAD_DIM), lambda b, h, t: (halo_row(b, h, t), h)),
        pl.BlockSpec((HALO, HEAD_DIM), lambda b, h, t: (halo_row(b, h, t), k_off + h)),
        pl.BlockSpec((HALO, vw), lambda b, h, t: (halo_row(b, h, t), v_off + h)),
        pl.BlockSpec((tb, LANES), lambda b, h, t: (row(b, h, t), 0)),
        pl.BlockSpec((None, LANES, tb), lambda b, h, t: (b, 0, t)),
        pl.BlockSpec((DN_CONV, HEAD_DIM), lambda b, h, t: (0, h)),
        pl.BlockSpec((DN_CONV, HEAD_DIM), lambda b, h, t: (0, k_off + h)),
        pl.BlockSpec((DN_CONV, vw), lambda b, h, t: (0, v_off + h)),
        pl.BlockSpec((1, HEAD_DIM), lambda b, h, t: (0, 0)),
        pl.BlockSpec(level_masks.shape, lambda b, h, t: (0, 0, 0)),
        pl.BlockSpec((tb, tb), lambda b, h, t: (0, 0)),
    ]
    return pl.pallas_call(
        functools.partial(_dn_kernel, n_vheads=n_vheads),
        grid=(bsz, n_kheads, nt),
        in_specs=in_specs,
        out_specs=pl.BlockSpec((tb, vw), lambda b, h, t: (row(b, h, t), h)),
        out_shape=jax.ShapeDtypeStruct((bsz * t_len, n_vheads * HEAD_DIM), jnp.bfloat16),
        scratch_shapes=[pltpu.VMEM((rep, HEAD_DIM, HEAD_DIM), jnp.float32)],
        compiler_params=_cparams(("parallel", "parallel", "arbitrary")),
        name="dn_delta_rule",
    )(proj, proj, proj, proj, proj, proj, proj, gates, gates_t, conv_w, conv_w, conv_w, norm_w[None, :],
      level_masks, neg_causal)


def _split3(a):
    hi = a.astype(jnp.bfloat16).astype(jnp.float32)
    mid = (a - hi).astype(jnp.bfloat16).astype(jnp.float32)
    lo = (a - hi - mid).astype(jnp.bfloat16).astype(jnp.float32)
    return hi, mid, lo


def _moba_kernel(slope_ref, q_ref, k_ref, vt_ref, o_ref, kmean_ref, qaug_ref, kaug_ref, s_ref, m_ref, l_ref, acc_ref):
    h = pl.program_id(1)
    own = pl.program_id(2)
    tq = q_ref.shape[0]
    t_len = k_ref.shape[0]
    nb = t_len // MB_BLOCK
    nt = (((1,), (1,)), ((), ()))
    log2e = 1.4426950408889634
    slope2 = slope_ref[h] * log2e

    @pl.when(own == 0)
    def _():
        kf = k_ref[...].astype(jnp.float32).reshape(nb, MB_BLOCK, HEAD_DIM)
        kmean = jnp.sum(kf, axis=1) * (1.0 / MB_BLOCK)
        kmean_hi = kmean.astype(jnp.bfloat16)
        kmean_lo = (kmean - kmean_hi.astype(jnp.float32)).astype(jnp.bfloat16)
        kmean_ref[...] = jnp.concatenate([kmean_hi, kmean_lo], axis=0)
        pos = lax.broadcasted_iota(jnp.int32, (MB_BLOCK, LANES), 0).astype(jnp.float32)
        col = lax.broadcasted_iota(jnp.int32, (MB_BLOCK, LANES), 1)
        q_hi, q_mid, q_lo = _split3(pos * (-slope2))
        k_hi, k_mid, k_lo = _split3(pos * slope2)
        one = jnp.where(col < 6, 1.0, 0.0)
        qaug = jnp.where(col == 0, q_hi, jnp.where(col == 1, q_mid, jnp.where(col == 2, q_lo, one)))
        kaug = jnp.where(col == 3, k_hi, jnp.where(col == 4, k_mid, jnp.where(col == 5, k_lo, one)))
        qaug_ref[...] = qaug.astype(qaug_ref.dtype)
        kaug_ref[...] = kaug.astype(kaug_ref.dtype)

    q = q_ref[...]
    qf = q.astype(jnp.float32)
    gate2 = lax.dot_general(kmean_ref[...], q, nt, preferred_element_type=jnp.float32)
    gate = gate2[:nb] + gate2[nb:]
    blk = lax.broadcasted_iota(jnp.int32, gate.shape, 0).astype(jnp.float32)
    gate = jnp.where(blk < own.astype(jnp.float32), gate, -jnp.inf)
    picks = jnp.zeros((1, tq), jnp.int32)
    for r in range(MB_TOPK):
        mx = jnp.max(gate, axis=0, keepdims=True)
        idx = jnp.min(jnp.where(gate == mx, blk, float(nb)), axis=0, keepdims=True)
        bit = jnp.left_shift(1, idx.astype(jnp.int32))
        picks = picks | jnp.where(r < own, bit, 0)
        gate = jnp.where(blk == idx, -jnp.inf, gate)

    q2 = jnp.concatenate([(qf * (HEAD_DIM ** -0.5 * log2e)).astype(jnp.bfloat16), qaug_ref[...]], axis=1)
    kaug = kaug_ref[...]

    def scores(n):
        start = pl.multiple_of(n * MB_BLOCK, MB_BLOCK)
        k2 = jnp.concatenate([k_ref[pl.ds(start, MB_BLOCK), :], kaug], axis=1)
        return lax.dot_general(k2, q2, nt, preferred_element_type=jnp.float32)

    def values_t(n):
        return vt_ref[:, pl.ds(pl.multiple_of(n * MB_BLOCK, MB_BLOCK), MB_BLOCK)]

    def past_scores(n):
        picked = (picks & jnp.left_shift(1, n)) != 0
        far = (own - n).astype(jnp.float32) * (-slope2 * MB_BLOCK)
        return scores(n) + jnp.where(picked, far, NEG_INF)

    def fold(op, x):
        parts = [x[r:r + 8] for r in range(0, x.shape[0], 8)]
        while len(parts) > 1:
            parts = [op(parts[a], parts[a + 1]) for a in range(0, len(parts) - 1, 2)] + parts[len(parts) & ~1:]
        return parts[0]

    def stage_pair(slot, i):
        s_ref[slot, :MB_BLOCK, :] = past_scores(jnp.minimum(2 * i, nb - 1))
        s_ref[slot, MB_BLOCK:, :] = past_scores(jnp.minimum(2 * i + 1, nb - 1))

    def pair_values_t(i):
        return vt_ref[:, pl.ds(pl.multiple_of(2 * i * MB_BLOCK, 2 * MB_BLOCK), 2 * MB_BLOCK)]

    def absorb(s, v_t):
        m = m_ref[...]
        m_new = jnp.maximum(m, jnp.max(fold(jnp.maximum, s), axis=0, keepdims=True))
        p = jnp.exp2(s - m_new)
        alpha = jnp.exp2(m - m_new)
        m_ref[...] = m_new
        l_ref[...] = alpha * l_ref[...] + fold(jnp.add, p)
        acc_ref[...] = alpha * acc_ref[...] + jnp.dot(v_t, p.astype(jnp.bfloat16),
                                                      preferred_element_type=jnp.float32)

    n_pairs = (own + 1) // 2

    def two_pairs(j, carry):
        i = 2 * j
        stage_pair(1, i + 1)
        absorb(s_ref[0], pair_values_t(i))

        @pl.when(i + 1 < n_pairs)
        def _():
            stage_pair(0, i + 2)
            absorb(s_ref[1], pair_values_t(i + 1))
        return carry

    m_ref[...] = jnp.full(m_ref.shape, NEG_INF, jnp.float32)
    l_ref[...] = jnp.zeros_like(l_ref)
    acc_ref[...] = jnp.zeros_like(acc_ref)
    stage_pair(0, 0)
    lax.fori_loop(0, (n_pairs + 1) // 2, two_pairs, 0)
    key_pos = lax.broadcasted_iota(jnp.int32, (MB_BLOCK, tq), 0)
    query_pos = lax.broadcasted_iota(jnp.int32, (MB_BLOCK, tq), 1)
    absorb(jnp.where(key_pos <= query_pos, scores(own), NEG_INF), values_t(own))
    l = jnp.sum(l_ref[...], axis=0, keepdims=True)
    o_ref[...] = (acc_ref[...] / l).T.astype(o_ref.dtype)


def _moba_attention(proj, bsz, t_len, n_heads):
    tq = MB_BLOCK
    nq = t_len // tq
    nb = t_len // MB_BLOCK
    assert nb <= 32, "per-row picks are kept as one int32 bit mask"
    slopes = jnp.exp2(-8.0 * jnp.arange(1, n_heads + 1, dtype=jnp.float32) / n_heads)
    v_t = jnp.transpose(proj[:, 2 * n_heads * HEAD_DIM:])
    grid_spec = pltpu.PrefetchScalarGridSpec(
        num_scalar_prefetch=1,
        grid=(bsz, n_heads, nq),
        in_specs=[pl.BlockSpec((tq, HEAD_DIM), lambda b, h, i, s: (b * nq + i, h)),
                  pl.BlockSpec((t_len, HEAD_DIM), lambda b, h, i, s: (b, n_heads + h)),
                  pl.BlockSpec((HEAD_DIM, t_len), lambda b, h, i, s: (h, b))],
        out_specs=pl.BlockSpec((tq, HEAD_DIM), lambda b, h, i, s: (b * nq + i, h)),
        scratch_shapes=[pltpu.VMEM((2 * nb, HEAD_DIM), jnp.bfloat16),
                        pltpu.VMEM((MB_BLOCK, LANES), jnp.bfloat16),
                        pltpu.VMEM((MB_BLOCK, LANES), jnp.bfloat16),
                        pltpu.VMEM((2, 2 * MB_BLOCK, tq), jnp.float32),
                        pltpu.VMEM((1, tq), jnp.float32),
                        pltpu.VMEM((8, tq), jnp.float32),
                        pltpu.VMEM((HEAD_DIM, tq), jnp.float32)],
    )
    return pl.pallas_call(
        _moba_kernel,
        grid_spec=grid_spec,
        out_shape=jax.ShapeDtypeStruct((bsz * t_len, n_heads * HEAD_DIM), jnp.bfloat16),
        compiler_params=_cparams(("parallel", "parallel", "arbitrary")),
        name="moba_attention",
    )(slopes, proj, proj, v_t)


def _layer_norm_rows(y, g, b):
    mu = jnp.mean(y, axis=-1, keepdims=True)
    yc = y - mu
    var = jnp.mean(yc * yc, axis=-1, keepdims=True)
    return yc * lax.rsqrt(var + LN_EPS) * g + b


def _post_mixer_kernel(h_ref, w_ref, xres_ref, g_ref, b_ref, rw_ref, rb_ref,
                       x1_ref, x1b_ref, topi_ref, topg_ref, *, alpha, n_experts):
    j = pl.program_id(1)
    tn = w_ref.shape[1]
    col = pl.multiple_of(j * tn, tn)
    x1_ref[:, pl.ds(col, tn)] = jnp.dot(h_ref[...], w_ref[...], preferred_element_type=jnp.float32)

    @pl.when(j == pl.num_programs(1) - 1)
    def _():
        x1 = _layer_norm_rows(alpha * xres_ref[...] + x1_ref[...], g_ref[...], b_ref[...])
        x1_ref[...] = x1
        x1b_ref[...] = x1.astype(x1b_ref.dtype)
        logits = jnp.dot(x1, rw_ref[...], precision=_HI, preferred_element_type=jnp.float32) + rb_ref[...]
        lane = lax.broadcasted_iota(jnp.int32, logits.shape, 1)
        logits = jnp.where(lane < n_experts, logits, -jnp.inf)
        topi = jnp.zeros(logits.shape, jnp.int32)
        topv = jnp.full(logits.shape, -jnp.inf, jnp.float32)
        for r in range(TOP_K):
            mx = jnp.max(logits, axis=-1, keepdims=True)
            idx = jnp.min(jnp.where(logits == mx, lane, LANES), axis=-1, keepdims=True)
            topi = jnp.where(lane == r, idx, topi)
            topv = jnp.where(lane == r, mx, topv)
            logits = jnp.where(lane == idx, -jnp.inf, logits)
        e = jnp.exp(topv - jnp.max(topv, axis=-1, keepdims=True))
        topi_ref[...] = topi
        topg_ref[...] = e / jnp.sum(e, axis=-1, keepdims=True)


def _post_mixer(h, w_out, xres, ln_g, ln_b, router_w, router_b, alpha):
    n, kdim = h.shape
    d = w_out.shape[1]
    n_experts = router_w.shape[1]
    tm, tn = min(256, n), min(512, d)
    rw = jnp.pad(router_w, ((0, 0), (0, LANES - n_experts)))
    rb = jnp.pad(router_b, (0, LANES - n_experts))[None, :]
    row = lambda i, j: (i, 0)
    fixed = lambda i, j: (0, 0)
    return pl.pallas_call(
        functools.partial(_post_mixer_kernel, alpha=alpha, n_experts=n_experts),
        grid=(n // tm, d // tn),
        in_specs=[pl.BlockSpec((tm, kdim), row),
                  pl.BlockSpec((kdim, tn), lambda i, j: (0, j)),
                  pl.BlockSpec((tm, d), row),
                  pl.BlockSpec((1, d), fixed),
                  pl.BlockSpec((1, d), fixed),
                  pl.BlockSpec((d, LANES), fixed),
                  pl.BlockSpec((1, LANES), fixed)],
        out_specs=[pl.BlockSpec((tm, d), row),
                   pl.BlockSpec((tm, d), row),
                   pl.BlockSpec((tm, LANES), row),
                   pl.BlockSpec((tm, LANES), row)],
        out_shape=[jax.ShapeDtypeStruct((n, d), jnp.float32),
                   jax.ShapeDtypeStruct((n, d), jnp.bfloat16),
                   jax.ShapeDtypeStruct((n, LANES), jnp.int32),
                   jax.ShapeDtypeStruct((n, LANES), jnp.float32)],
        compiler_params=_cparams(("parallel", "arbitrary")),
        name="post_mixer",
    )(h, w_out, xres, ln_g[None, :], ln_b[None, :], rw, rb)


def _moe_kernel(be_ref, nu_ref, xs_ref, wg_ref, wu_ref, wd_ref, bg_ref, bu_ref, bd_ref, y_ref, acc_ref):
    i = pl.program_id(0)
    j = pl.program_id(1)
    last = pl.num_programs(1) - 1
    used = i < nu_ref[0]

    @pl.when(used)
    def _():
        xs = xs_ref[...]
        h_gate = jnp.dot(xs, wg_ref[...], preferred_element_type=jnp.float32) + bg_ref[...]
        h_up = jnp.dot(xs, wu_ref[...], preferred_element_type=jnp.float32) + bu_ref[...]
        h_gate = jnp.minimum(h_gate, SWIGLU_LIMIT)
        h_up = jnp.clip(h_up, -SWIGLU_LIMIT, SWIGLU_LIMIT)
        hid = (h_up + 1.0) * (h_gate * _sigmoid(SWIGLU_ALPHA * h_gate))
        part = jnp.dot(hid.astype(jnp.bfloat16), wd_ref[...], preferred_element_type=jnp.float32)

        @pl.when(j == 0)
        def _():
            acc_ref[...] = part

        @pl.when(j > 0)
        def _():
            acc_ref[...] += part

        @pl.when(j == last)
        def _():
            y_ref[...] = (acc_ref[...] + bd_ref[...]).astype(y_ref.dtype)

    @pl.when(jnp.logical_not(used) & (j == last))
    def _():
        y_ref[...] = jnp.zeros_like(y_ref)


def _moe_experts(xs, block_e, n_used, w_gate, w_up, w_down, b_gate, b_up, b_down, tm):
    n_slots, d = xs.shape
    n_experts, _, f = w_gate.shape
    tf = min(256, f)
    nf = f // tf
    n_blocks = n_slots // tm

    def jj(i, j, nu):
        return jnp.where(i < nu[0], j, nf - 1)

    grid_spec = pltpu.PrefetchScalarGridSpec(
        num_scalar_prefetch=2,
        grid=(n_blocks, nf),
        in_specs=[pl.BlockSpec((tm, d), lambda i, j, be, nu: (jnp.minimum(i, nu[0] - 1), 0)),
                  pl.BlockSpec((None, d, tf), lambda i, j, be, nu: (be[i], 0, jj(i, j, nu))),
                  pl.BlockSpec((None, d, tf), lambda i, j, be, nu: (be[i], 0, jj(i, j, nu))),
                  pl.BlockSpec((None, tf, d), lambda i, j, be, nu: (be[i], jj(i, j, nu), 0)),
                  pl.BlockSpec((None, 1, tf), lambda i, j, be, nu: (be[i], 0, jj(i, j, nu))),
                  pl.BlockSpec((None, 1, tf), lambda i, j, be, nu: (be[i], 0, jj(i, j, nu))),
                  pl.BlockSpec((None, 1, d), lambda i, j, be, nu: (be[i], 0, 0))],
        out_specs=pl.BlockSpec((tm, d), lambda i, j, be, nu: (i, 0)),
        scratch_shapes=[pltpu.VMEM((tm, d), jnp.float32)],
    )
    return pl.pallas_call(
        _moe_kernel,
        grid_spec=grid_spec,
        out_shape=jax.ShapeDtypeStruct((n_slots, d), jnp.bfloat16),
        compiler_params=_cparams(("arbitrary", "arbitrary")),
        name="moe_experts",
    )(block_e, n_used, xs, w_gate, w_up, w_down, b_gate[:, None, :], b_up[:, None, :], b_down[:, None, :])


def _combine_kernel(yg_ref, gate_ref, x1_ref, g_ref, b_ref, o_ref, ob_ref, *, alpha):
    gates = gate_ref[...]
    f = yg_ref[0].astype(jnp.float32) * gates[:, 0:1]
    for r in range(1, TOP_K):
        f = f + yg_ref[r].astype(jnp.float32) * gates[:, r:r + 1]
    out = _layer_norm_rows(alpha * x1_ref[...] + f, g_ref[...], b_ref[...])
    o_ref[...] = out
    ob_ref[...] = out.astype(ob_ref.dtype)


def _moe_combine(yg, gates, x1, ln_g, ln_b, alpha):
    n, d = x1.shape
    tm = min(256, n)
    return pl.pallas_call(
        functools.partial(_combine_kernel, alpha=alpha),
        grid=(n // tm,),
        in_specs=[pl.BlockSpec((TOP_K, tm, d), lambda i: (0, i, 0)),
                  pl.BlockSpec((tm, LANES), lambda i: (i, 0)),
                  pl.BlockSpec((tm, d), lambda i: (i, 0)),
                  pl.BlockSpec((1, d), lambda i: (0, 0)),
                  pl.BlockSpec((1, d), lambda i: (0, 0))],
        out_specs=[pl.BlockSpec((tm, d), lambda i: (i, 0)),
                   pl.BlockSpec((tm, d), lambda i: (i, 0))],
        out_shape=[jax.ShapeDtypeStruct((n, d), jnp.float32),
                   jax.ShapeDtypeStruct((n, d), jnp.bfloat16)],
        compiler_params=_cparams(("parallel",)),
        name="moe_combine",
    )(yg, gates, x1, ln_g[None, :], ln_b[None, :])


def _moe_layer(x1, x1b, topi, topg, ln_g, ln_b, w_gate, b_gate, w_up, b_up, w_down, b_down, alpha):
    n_tok, d = x1.shape
    n_experts = w_gate.shape[0]
    n_assign = n_tok * TOP_K
    tm = min(512, n_assign // n_experts)
    flat_e = topi[:, :TOP_K].reshape(-1)
    onehot = (flat_e[:, None] == jnp.arange(n_experts, dtype=jnp.int32)[None, :]).astype(jnp.int32)
    csum = jnp.cumsum(onehot, axis=0)
    rank = jnp.take_along_axis(csum, flat_e[:, None], axis=1)[:, 0] - 1
    counts = csum[-1]
    padded = (counts + tm - 1) // tm * tm
    pends = jnp.cumsum(padded)
    dest = (pends - padded)[flat_e] + rank
    n_blocks = n_assign // tm + n_experts
    flat_tok = jnp.arange(n_assign, dtype=jnp.int32) // TOP_K
    rows = jnp.zeros((n_blocks * tm,), jnp.int32).at[dest].set(flat_tok)
    block_e = jnp.minimum(jnp.searchsorted(pends, jnp.arange(n_blocks, dtype=jnp.int32) * tm, side='right'),
                          n_experts - 1).astype(jnp.int32)
    n_used = (pends[-1:] // tm).astype(jnp.int32)
    xs = jnp.take(x1b, rows, axis=0)
    y = _moe_experts(xs, block_e, n_used, w_gate.astype(jnp.bfloat16), w_up.astype(jnp.bfloat16),
                     w_down.astype(jnp.bfloat16), b_gate, b_up, b_down, tm)
    yg = jnp.take(y, dest.reshape(n_tok, TOP_K).T, axis=0)
    return _moe_combine(yg, topg, x1, ln_g, ln_b, alpha)


def kernel(x, dn_w_in, dn_conv_w, dn_a_log, dn_dt_bias, dn_norm_w, dn_w_out, mb_w_in, mb_w_out, ln_g, ln_b,
           router_w, router_b, w_gate, b_gate, w_up, b_up, w_down, b_down):
    bsz, t_len, d = x.shape
    depth = ln_g.shape[0]
    alpha = float((2 * depth) ** 0.25)
    n = bsz * t_len
    xf = x.reshape(n, d)
    xb = xf.astype(jnp.bfloat16)
    for i in range(depth):
        j = i // 2
        if i % 2 == 0:
            n_vheads = d // HEAD_DIM
            main = dn_w_in.shape[2] - 2 * n_vheads
            proj = _matmul(xb, dn_w_in[j][:, :main].astype(jnp.bfloat16), jnp.bfloat16, 1024, 512)
            gates = _dn_gates(xf, dn_w_in[j][:, main:], dn_a_log[j], dn_dt_bias[j], n_vheads)
            gates_t = jnp.transpose(gates.reshape(bsz, t_len, LANES), (0, 2, 1))
            h = _dn_delta_rule(proj, gates, gates_t, dn_conv_w[j], dn_norm_w[j], bsz, t_len, d)
            w_out = dn_w_out[j]
        else:
            proj = _matmul(xb, mb_w_in[j].astype(jnp.bfloat16), jnp.bfloat16, 1024, 512)
            h = _moba_attention(proj, bsz, t_len, d // HEAD_DIM)
            w_out = mb_w_out[j]
        x1, x1b, topi, topg = _post_mixer(h, w_out.astype(jnp.bfloat16), xf, ln_g[i, 0], ln_b[i, 0],
                                          router_w[i], router_b[i], alpha)
        xf, xb = _moe_layer(x1, x1b, topi, topg, ln_g[i, 1], ln_b[i, 1], w_gate[i], b_gate[i], w_up[i], b_up[i],
                            w_down[i], b_down[i], alpha)
    return xf.reshape(bsz, t_len, d)
```

```python
import functools

import jax
import jax.numpy as jnp
import numpy as np
from jax import lax
from jax.experimental import pallas as pl
from jax.experimental.pallas import tpu as pltpu

HEAD_DIM = 128
DN_CONV = 4
DN_CHUNK = 256
MB_BLOCK = 256
MB_TOPK = 3
TOP_K = 4
SWIGLU_LIMIT = 7.0
SWIGLU_ALPHA = 1.702
LN_EPS = 1e-5
NORM_EPS = 1e-6
NEG_INF = -1e30
LANES = 128
HALO = 16
VMEM_LIMIT = 56 * 1024 * 1024

_HI = lax.Precision.HIGHEST


def _cparams(sem):
    return pltpu.CompilerParams(dimension_semantics=sem, vmem_limit_bytes=VMEM_LIMIT)


def _sigmoid(v):
    return 1.0 / (1.0 + jnp.exp(-v))


def _silu(v):
    return v * _sigmoid(v)


def _mm_kernel(x_ref, w_ref, o_ref):
    o_ref[...] = jnp.dot(x_ref[...], w_ref[...], preferred_element_type=jnp.float32).astype(o_ref.dtype)


def _matmul(x, w, n, out_dtype, tm, tn):
    m, k = x.shape
    tm, tn = min(tm, m), min(tn, n)
    assert n % tn == 0 and m % tm == 0
    return pl.pallas_call(
        _mm_kernel,
        grid=(m // tm, n // tn),
        in_specs=[pl.BlockSpec((tm, k), lambda i, j: (i, 0)),
                  pl.BlockSpec((k, tn), lambda i, j: (0, j))],
        out_specs=pl.BlockSpec((tm, tn), lambda i, j: (i, j)),
        out_shape=jax.ShapeDtypeStruct((m, n), out_dtype),
        compiler_params=_cparams(("parallel", "arbitrary")),
        name="dense_matmul",
    )(x, w)


def _dn_gates_kernel(x_ref, w_ref, alog_ref, dtb_ref, o_ref, *, n_heads):
    tm = x_ref.shape[0]
    ba = jnp.dot(x_ref[...], w_ref[...], precision=_HI, preferred_element_type=jnp.float32)
    lane = lax.broadcasted_iota(jnp.int32, ba.shape, 1)
    beta = _sigmoid(ba)
    v = ba + dtb_ref[...]
    softplus = jnp.maximum(v, 0.0) + jnp.log(1.0 + jnp.exp(-jnp.abs(v)))
    g = -jnp.exp(alog_ref[...]) * softplus
    g = jnp.where((lane >= n_heads) & (lane < 2 * n_heads), g, 0.0)
    ri = lax.broadcasted_iota(jnp.int32, (tm, tm), 0)
    ci = lax.broadcasted_iota(jnp.int32, (tm, tm), 1)
    shift = DN_CHUNK.bit_length() - 1
    tril = jnp.where((ci <= ri) & ((ri >> shift) == (ci >> shift)), 1.0, 0.0)
    gc = jnp.dot(tril, g, precision=_HI, preferred_element_type=jnp.float32)
    o_ref[...] = jnp.where(lane < n_heads, beta, gc)


def _dn_gates(x2d, w_ba, a_log, dt_bias, n_heads):
    n, d = x2d.shape
    tm = min(256, n)
    pad = LANES - 2 * n_heads
    w_p = jnp.pad(w_ba, ((0, 0), (0, pad)))
    alog_p = jnp.pad(a_log, (n_heads, pad))[None, :]
    dtb_p = jnp.pad(dt_bias, (n_heads, pad))[None, :]
    return pl.pallas_call(
        functools.partial(_dn_gates_kernel, n_heads=n_heads),
        grid=(n // tm,),
        in_specs=[pl.BlockSpec((tm, d), lambda i: (i, 0)),
                  pl.BlockSpec((d, LANES), lambda i: (0, 0)),
                  pl.BlockSpec((1, LANES), lambda i: (0, 0)),
                  pl.BlockSpec((1, LANES), lambda i: (0, 0))],
        out_specs=pl.BlockSpec((tm, LANES), lambda i: (i, 0)),
        out_shape=jax.ShapeDtypeStruct((n, LANES), jnp.float32),
        compiler_params=_cparams(("parallel",)),
        name="dn_gates",
    )(x2d, w_p, alog_p, dtb_p)


def _causal_conv_silu(x_ref, halo_ref, w_ref, first):
    tb = x_ref.shape[0]
    halo = jnp.where(first, 0.0, halo_ref[...].astype(jnp.float32))
    xcat = jnp.concatenate([halo, x_ref[...].astype(jnp.float32)], axis=0)
    w = w_ref[...]
    acc = xcat[HALO:] * w[DN_CONV - 1:DN_CONV, :]
    for s in range(1, DN_CONV):
        acc = acc + pltpu.roll(xcat, s, 0)[HALO:] * w[DN_CONV - 1 - s:DN_CONV - s, :]
    del tb
    return _silu(acc)


def _l2norm(v):
    return v * lax.rsqrt(jnp.sum(v * v, axis=-1, keepdims=True) + NORM_EPS)


def _inverse_level_masks(c):
    ri = np.arange(c)[:, None]
    ci = np.arange(c)[None, :]
    masks = [((ri >> 1) == (ci >> 1)) & (ri > ci)]
    s = 2
    while s < c:
        masks.append(((ri // (2 * s)) == (ci // (2 * s))) & ((ri & s) != 0) & ((ci & s) == 0))
        s *= 2
    return jnp.asarray(np.stack(masks), jnp.bfloat16)


def _dn_kernel(q_ref, k_ref, v_ref, z_ref, qh_ref, kh_ref, vh_ref, g_ref, gt_ref,
               wq_ref, wk_ref, wv_ref, nw_ref, lm_ref, negc_ref, o_ref, state_ref, *, n_vheads):
    hk = pl.program_id(1)
    t = pl.program_id(2)
    first = t == 0
    tb = q_ref.shape[0]
    rep = v_ref.shape[1] // HEAD_DIM
    bf16, f32 = jnp.bfloat16, jnp.float32

    @pl.when(first)
    def _():
        state_ref[...] = jnp.zeros_like(state_ref)

    q = _l2norm(_causal_conv_silu(q_ref, qh_ref, wq_ref, first)) * (HEAD_DIM ** -0.5)
    k = _l2norm(_causal_conv_silu(k_ref, kh_ref, wk_ref, first))
    v = _causal_conv_silu(v_ref, vh_ref, wv_ref, first)
    gates = g_ref[...]
    lane = lax.broadcasted_iota(jnp.int32, gates.shape, 1)
    eye = jnp.where(lax.broadcasted_iota(jnp.int32, (tb, tb), 0) == lax.broadcasted_iota(jnp.int32, (tb, tb), 1),
                    1.0, 0.0)
    nt = (((1,), (1,)), ((), ()))
    q_b = q.astype(bf16)
    k_b = k.astype(bf16)
    kk = lax.dot_general(k_b, k_b, nt, preferred_element_type=f32)
    qk = lax.dot_general(q_b, k_b, nt, preferred_element_type=f32)

    heads = range(rep)
    beta, gcol, g_last, decay, a_b, inv = [], [], [], [], [], []
    for j in heads:
        hv = hk * rep + j
        beta.append(jnp.sum(jnp.where(lane == hv, gates, 0.0), axis=-1, keepdims=True))
        gcol.append(jnp.sum(jnp.where(lane == n_vheads + hv, gates, 0.0), axis=-1, keepdims=True))
        grow = gt_ref[pl.ds(n_vheads + hv, 1), :]
        g_last.append(grow[:, tb - 1:tb])
        decay.append(jnp.exp(gcol[j] - grow + negc_ref[...]))
        a_b.append((kk * beta[j] * decay[j]).astype(bf16))
        inv.append(eye - (a_b[j] * lm_ref[0]).astype(f32))
    for lvl in range(1, lm_ref.shape[0]):
        tmp = [jnp.dot(a_b[j] * lm_ref[lvl], inv[j].astype(bf16), preferred_element_type=f32) for j in heads]
        inv = [inv[j] - jnp.dot(inv[j].astype(bf16), tmp[j].astype(bf16), preferred_element_type=f32)
               for j in heads]
    eg = [jnp.exp(gcol[j]) for j in heads]
    uw = [jnp.dot(inv[j].astype(bf16),
                  jnp.concatenate([v[:, j * HEAD_DIM:(j + 1) * HEAD_DIM] * beta[j], k * (beta[j] * eg[j])],
                                  axis=1).astype(bf16), preferred_element_type=f32) for j in heads]
    state = [state_ref[j] for j in heads]
    ws = [jnp.dot(jnp.concatenate([uw[j][:, HEAD_DIM:], q * eg[j]], axis=0).astype(bf16), state[j].astype(bf16),
                  preferred_element_type=f32) for j in heads]
    v_new_b = [(uw[j][:, :HEAD_DIM] - ws[j][:tb]).astype(bf16) for j in heads]
    outs = []
    for j in heads:
        o = ws[j][tb:] + jnp.dot((qk * decay[j]).astype(bf16), v_new_b[j], preferred_element_type=f32)
        k_dec_t = (k * jnp.exp(g_last[j] - gcol[j])).T.astype(bf16)
        state_ref[j] = state[j] * jnp.exp(g_last[j]) + jnp.dot(k_dec_t, v_new_b[j], preferred_element_type=f32)
        zj = z_ref[:, j * HEAD_DIM:(j + 1) * HEAD_DIM].astype(f32)
        outs.append(o * lax.rsqrt(jnp.mean(o * o, axis=-1, keepdims=True) + NORM_EPS) * nw_ref[...] * _silu(zj))
    o_ref[...] = jnp.concatenate(outs, axis=1).astype(o_ref.dtype)


def _dn_delta_rule(proj, gates, gates_t, conv_w, norm_w, bsz, t_len, d_model):
    n_kheads = d_model // (2 * HEAD_DIM)
    n_vheads = d_model // HEAD_DIM
    rep = n_vheads // n_kheads
    vw = rep * HEAD_DIM
    tb = DN_CHUNK
    nt = t_len // tb
    hb = tb // HALO
    level_masks = _inverse_level_masks(tb)
    ri = np.arange(tb)
    neg_causal = jnp.asarray(np.where(ri[None, :] <= ri[:, None], 0.0, NEG_INF), jnp.float32)
    k_off = n_kheads
    v_off = 2 * n_kheads * HEAD_DIM // vw
    z_off = v_off + n_vheads * HEAD_DIM // vw

    def row(b, h, t):
        return b * nt + t

    def halo_row(b, h, t):
        return jnp.maximum((b * nt + t) * hb - 1, 0)

    in_specs = [
        pl.BlockSpec((tb, HEAD_DIM), lambda b, h, t: (row(b, h, t), h)),
        pl.BlockSpec((tb, HEAD_DIM), lambda b, h, t: (row(b, h, t), k_off + h)),
        pl.BlockSpec((tb, vw), lambda b, h, t: (row(b, h, t), v_off + h)),
        pl.BlockSpec((tb, vw), lambda b, h, t: (row(b, h, t), z_off + h)),
        pl.BlockSpec((HALO, HEAD_DIM), lambda b, h, t: (halo_row(b, h, t), h)),
        pl.BlockSpec((HALO, HEAD_DIM), lambda b, h, t: (halo_row(b, h, t), k_off + h)),
        pl.BlockSpec((HALO, vw), lambda b, h, t: (halo_row(b, h, t), v_off + h)),
        pl.BlockSpec((tb, LANES), lambda b, h, t: (row(b, h, t), 0)),
        pl.BlockSpec((None, LANES, tb), lambda b, h, t: (b, 0, t)),
        pl.BlockSpec((DN_CONV, HEAD_DIM), lambda b, h, t: (0, h)),
        pl.BlockSpec((DN_CONV, HEAD_DIM), lambda b, h, t: (0, k_off + h)),
        pl.BlockSpec((DN_CONV, vw), lambda b, h, t: (0, v_off + h)),
        pl.BlockSpec((1, HEAD_DIM), lambda b, h, t: (0, 0)),
        pl.BlockSpec(level_masks.shape, lambda b, h, t: (0, 0, 0)),
        pl.BlockSpec((tb, tb), lambda b, h, t: (0, 0)),
    ]
    return pl.pallas_call(
        functools.partial(_dn_kernel, n_vheads=n_vheads),
        grid=(bsz, n_kheads, nt),
        in_specs=in_specs,
        out_specs=pl.BlockSpec((tb, vw), lambda b, h, t: (row(b, h, t), h)),
        out_shape=jax.ShapeDtypeStruct((bsz * t_len, n_vheads * HEAD_DIM), jnp.bfloat16),
        scratch_shapes=[pltpu.VMEM((rep, HEAD_DIM, HEAD_DIM), jnp.float32)],
        compiler_params=_cparams(("parallel", "parallel", "arbitrary")),
        name="dn_delta_rule",
    )(proj, proj, proj, proj, proj, proj, proj, gates, gates_t, conv_w, conv_w, conv_w, norm_w[None, :],
      level_masks, neg_causal)


def _split3(a):
    hi = a.astype(jnp.bfloat16).astype(jnp.float32)
    mid = (a - hi).astype(jnp.bfloat16).astype(jnp.float32)
    lo = (a - hi - mid).astype(jnp.bfloat16).astype(jnp.float32)
    return hi, mid, lo


def _moba_kernel(slope_ref, q_ref, k_ref, vt_ref, o_ref, kmean_ref, qaug_ref, kaug_ref, s_ref, m_ref, l_ref, acc_ref):
    h = pl.program_id(1)
    own = pl.program_id(2)
    tq = q_ref.shape[0]
    t_len = k_ref.shape[0]
    nb = t_len // MB_BLOCK
    nt = (((1,), (1,)), ((), ()))
    log2e = 1.4426950408889634
    slope2 = slope_ref[h] * log2e

    @pl.when(own == 0)
    def _():
        kf = k_ref[...].astype(jnp.float32).reshape(nb, MB_BLOCK, HEAD_DIM)
        kmean = jnp.sum(kf, axis=1) * (1.0 / MB_BLOCK)
        kmean_hi = kmean.astype(jnp.bfloat16)
        kmean_lo = (kmean - kmean_hi.astype(jnp.float32)).astype(jnp.bfloat16)
        kmean_ref[...] = jnp.concatenate([kmean_hi, kmean_lo], axis=0)
        pos = lax.broadcasted_iota(jnp.int32, (MB_BLOCK, LANES), 0).astype(jnp.float32)
        col = lax.broadcasted_iota(jnp.int32, (MB_BLOCK, LANES), 1)
        q_hi, q_mid, q_lo = _split3(pos * (-slope2))
        k_hi, k_mid, k_lo = _split3(pos * slope2)
        one = jnp.where(col < 6, 1.0, 0.0)
        qaug = jnp.where(col == 0, q_hi, jnp.where(col == 1, q_mid, jnp.where(col == 2, q_lo, one)))
        kaug = jnp.where(col == 3, k_hi, jnp.where(col == 4, k_mid, jnp.where(col == 5, k_lo, one)))
        qaug_ref[...] = qaug.astype(qaug_ref.dtype)
        kaug_ref[...] = kaug.astype(kaug_ref.dtype)

    q = q_ref[...]
    qf = q.astype(jnp.float32)
    gate2 = lax.dot_general(kmean_ref[...], q, nt, preferred_element_type=jnp.float32)
    gate = gate2[:nb] + gate2[nb:]
    blk = lax.broadcasted_iota(jnp.int32, gate.shape, 0).astype(jnp.float32)
    gate = jnp.where(blk < own.astype(jnp.float32), gate, -jnp.inf)
    picks = jnp.zeros((1, tq), jnp.int32)
    for r in range(MB_TOPK):
        mx = jnp.max(gate, axis=0, keepdims=True)
        idx = jnp.min(jnp.where(gate == mx, blk, float(nb)), axis=0, keepdims=True)
        bit = jnp.left_shift(1, idx.astype(jnp.int32))
        picks = picks | jnp.where(r < own, bit, 0)
        gate = jnp.where(blk == idx, -jnp.inf, gate)

    q2 = jnp.concatenate([(qf * (HEAD_DIM ** -0.5 * log2e)).astype(jnp.bfloat16), qaug_ref[...]], axis=1)
    kaug = kaug_ref[...]

    def scores(n):
        start = pl.multiple_of(n * MB_BLOCK, MB_BLOCK)
        k2 = jnp.concatenate([k_ref[pl.ds(start, MB_BLOCK), :], kaug], axis=1)
        return lax.dot_general(k2, q2, nt, preferred_element_type=jnp.float32)

    def values_t(n):
        return vt_ref[:, pl.ds(pl.multiple_of(n * MB_BLOCK, MB_BLOCK), MB_BLOCK)]

    def past_scores(n):
        picked = (picks & jnp.left_shift(1, n)) != 0
        far = (own - n).astype(jnp.float32) * (-slope2 * MB_BLOCK)
        return scores(n) + jnp.where(picked, far, NEG_INF)

    def fold(op, x):
        parts = [x[r:r + 8] for r in range(0, x.shape[0], 8)]
        while len(parts) > 1:
            parts = [op(parts[a], parts[a + 1]) for a in range(0, len(parts) - 1, 2)] + parts[len(parts) & ~1:]
        return parts[0]

    def stage_pair(slot, i):
        s_ref[slot, :MB_BLOCK, :] = past_scores(jnp.minimum(2 * i, nb - 1))
        s_ref[slot, MB_BLOCK:, :] = past_scores(jnp.minimum(2 * i + 1, nb - 1))

    def pair_values_t(i):
        return vt_ref[:, pl.ds(pl.multiple_of(2 * i * MB_BLOCK, 2 * MB_BLOCK), 2 * MB_BLOCK)]

    def absorb(s, v_t):
        m = m_ref[...]
        m_new = jnp.maximum(m, jnp.max(fold(jnp.maximum, s), axis=0, keepdims=True))
        p = jnp.exp2(s - m_new)
        alpha = jnp.exp2(m - m_new)
        m_ref[...] = m_new
        l_ref[...] = alpha * l_ref[...] + fold(jnp.add, p)
        acc_ref[...] = alpha * acc_ref[...] + jnp.dot(v_t, p.astype(jnp.bfloat16),
                                                      preferred_element_type=jnp.float32)

    n_pairs = (own + 1) // 2

    def two_pairs(j, carry):
        i = 2 * j
        stage_pair(1, i + 1)
        absorb(s_ref[0], pair_values_t(i))

        @pl.when(i + 1 < n_pairs)
        def _():
            stage_pair(0, i + 2)
            absorb(s_ref[1], pair_values_t(i + 1))
        return carry

    m_ref[...] = jnp.full(m_ref.shape, NEG_INF, jnp.float32)
    l_ref[...] = jnp.zeros_like(l_ref)
    acc_ref[...] = jnp.zeros_like(acc_ref)
    stage_pair(0, 0)
    lax.fori_loop(0, (n_pairs + 1) // 2, two_pairs, 0)
    key_pos = lax.broadcasted_iota(jnp.int32, (MB_BLOCK, tq), 0)
    query_pos = lax.broadcasted_iota(jnp.int32, (MB_BLOCK, tq), 1)
    absorb(jnp.where(key_pos <= query_pos, scores(own), NEG_INF), values_t(own))
    l = jnp.sum(l_ref[...], axis=0, keepdims=True)
    o_ref[...] = (acc_ref[...] / l).T.astype(o_ref.dtype)


def _moba_attention(proj, bsz, t_len, n_heads):
    tq = MB_BLOCK
    nq = t_len // tq
    nb = t_len // MB_BLOCK
    assert nb <= 32, "per-row picks are kept as one int32 bit mask"
    slopes = jnp.exp2(-8.0 * jnp.arange(1, n_heads + 1, dtype=jnp.float32) / n_heads)
    v_t = jnp.transpose(proj[:, 2 * n_heads * HEAD_DIM:])
    grid_spec = pltpu.PrefetchScalarGridSpec(
        num_scalar_prefetch=1,
        grid=(bsz, n_heads, nq),
        in_specs=[pl.BlockSpec((tq, HEAD_DIM), lambda b, h, i, s: (b * nq + i, h)),
                  pl.BlockSpec((t_len, HEAD_DIM), lambda b, h, i, s: (b, n_heads + h)),
                  pl.BlockSpec((HEAD_DIM, t_len), lambda b, h, i, s: (h, b))],
        out_specs=pl.BlockSpec((tq, HEAD_DIM), lambda b, h, i, s: (b * nq + i, h)),
        scratch_shapes=[pltpu.VMEM((2 * nb, HEAD_DIM), jnp.bfloat16),
                        pltpu.VMEM((MB_BLOCK, LANES), jnp.bfloat16),
                        pltpu.VMEM((MB_BLOCK, LANES), jnp.bfloat16),
                        pltpu.VMEM((2, 2 * MB_BLOCK, tq), jnp.float32),
                        pltpu.VMEM((1, tq), jnp.float32),
                        pltpu.VMEM((8, tq), jnp.float32),
                        pltpu.VMEM((HEAD_DIM, tq), jnp.float32)],
    )
    return pl.pallas_call(
        _moba_kernel,
        grid_spec=grid_spec,
        out_shape=jax.ShapeDtypeStruct((bsz * t_len, n_heads * HEAD_DIM), jnp.bfloat16),
        compiler_params=_cparams(("parallel", "parallel", "arbitrary")),
        name="moba_attention",
    )(slopes, proj, proj, v_t)


def _layer_norm_rows(y, g, b):
    mu = jnp.mean(y, axis=-1, keepdims=True)
    yc = y - mu
    var = jnp.mean(yc * yc, axis=-1, keepdims=True)
    return yc * lax.rsqrt(var + LN_EPS) * g + b


def _post_mixer_kernel(h_ref, w_ref, xres_ref, g_ref, b_ref, rw_ref, rb_ref,
                       x1_ref, x1b_ref, topi_ref, topg_ref, rank_ref, count_ref, *, alpha, n_experts):
    i = pl.program_id(0)
    j = pl.program_id(1)
    tm = h_ref.shape[0]
    tn = w_ref.shape[1]
    col = pl.multiple_of(j * tn, tn)
    x1_ref[:, pl.ds(col, tn)] = jnp.dot(h_ref[...], w_ref[...], preferred_element_type=jnp.float32)

    @pl.when((i == 0) & (j == 0))
    def _():
        count_ref[...] = jnp.zeros_like(count_ref)

    @pl.when(j == pl.num_programs(1) - 1)
    def _():
        x1 = _layer_norm_rows(alpha * xres_ref[...] + x1_ref[...], g_ref[...], b_ref[...])
        x1_ref[...] = x1
        x1_hi = x1.astype(jnp.bfloat16)
        x1b_ref[...] = x1_hi
        x1_lo = (x1 - x1_hi.astype(jnp.float32)).astype(jnp.bfloat16)
        hi_terms = jnp.dot(x1_hi, rw_ref[...], preferred_element_type=jnp.float32)
        lo_term = jnp.dot(x1_lo, rw_ref[:, :LANES], preferred_element_type=jnp.float32)
        logits = hi_terms[:, :LANES] + hi_terms[:, LANES:] + lo_term + rb_ref[...]
        lane_i = lax.broadcasted_iota(jnp.int32, logits.shape, 1)
        lane = lane_i.astype(jnp.float32)
        logits = jnp.where(lane_i < n_experts, logits, -jnp.inf)
        topi = jnp.zeros(logits.shape, jnp.float32)
        topv = jnp.full(logits.shape, -jnp.inf, jnp.float32)
        chosen = []
        for r in range(TOP_K):
            mx = jnp.max(logits, axis=-1, keepdims=True)
            idx = jnp.min(jnp.where(logits == mx, lane, float(LANES)), axis=-1, keepdims=True)
            hit = lane == idx
            chosen.append(hit)
            topi = jnp.where(lane_i == r, idx, topi)
            topv = jnp.where(lane_i == r, mx, topv)
            logits = jnp.where(hit, -jnp.inf, logits)
        e = jnp.exp(topv - jnp.max(topv, axis=-1, keepdims=True))
        topi_ref[...] = topi.astype(jnp.int32)
        topg_ref[...] = e / jnp.sum(e, axis=-1, keepdims=True)
        onehot = jnp.where(functools.reduce(jnp.logical_or, chosen), 1.0, 0.0)
        earlier = jnp.where(lax.broadcasted_iota(jnp.int32, (tm, tm), 1) < lax.broadcasted_iota(jnp.int32, (tm, tm), 0),
                            1.0, 0.0).astype(jnp.bfloat16)
        before = count_ref[...] + jnp.dot(earlier, onehot.astype(jnp.bfloat16), preferred_element_type=jnp.float32)
        rank = jnp.zeros(logits.shape, jnp.float32)
        for r in range(TOP_K):
            rank = jnp.where(lane_i == r, jnp.sum(jnp.where(chosen[r], before, 0.0), axis=-1, keepdims=True), rank)
        rank_ref[...] = rank.astype(jnp.int32)
        count_ref[...] += jnp.sum(onehot, axis=0, keepdims=True)


def _post_mixer(h, w_out, xres, ln_g, ln_b, router_w, router_b, alpha):
    n, kdim = h.shape
    d = w_out.shape[1]
    n_experts = router_w.shape[1]
    tm, tn = min(256, n), min(512, d)
    rw = jnp.pad(router_w, ((0, 0), (0, LANES - n_experts)))
    rw_hi = rw.astype(jnp.bfloat16)
    rw_lo = (rw - rw_hi.astype(jnp.float32)).astype(jnp.bfloat16)
    rw = jnp.concatenate([rw_hi, rw_lo], axis=1)
    rb = jnp.pad(router_b, (0, LANES - n_experts))[None, :]
    row = lambda i, j: (i, 0)
    fixed = lambda i, j: (0, 0)
    return pl.pallas_call(
        functools.partial(_post_mixer_kernel, alpha=alpha, n_experts=n_experts),
        grid=(n // tm, d // tn),
        in_specs=[pl.BlockSpec((tm, kdim), row),
                  pl.BlockSpec((kdim, tn), lambda i, j: (0, j)),
                  pl.BlockSpec((tm, d), row),
                  pl.BlockSpec((1, d), fixed),
                  pl.BlockSpec((1, d), fixed),
                  pl.BlockSpec((d, 2 * LANES), fixed),
                  pl.BlockSpec((1, LANES), fixed)],
        out_specs=[pl.BlockSpec((tm, d), row),
                   pl.BlockSpec((tm, d), row),
                   pl.BlockSpec((tm, LANES), row),
                   pl.BlockSpec((tm, LANES), row),
                   pl.BlockSpec((tm, LANES), row),
                   pl.BlockSpec((1, LANES), fixed)],
        out_shape=[jax.ShapeDtypeStruct((n, d), jnp.float32),
                   jax.ShapeDtypeStruct((n, d), jnp.bfloat16),
                   jax.ShapeDtypeStruct((n, LANES), jnp.int32),
                   jax.ShapeDtypeStruct((n, LANES), jnp.float32),
                   jax.ShapeDtypeStruct((n, LANES), jnp.int32),
                   jax.ShapeDtypeStruct((1, LANES), jnp.float32)],
        compiler_params=_cparams(("arbitrary", "arbitrary")),
        name="post_mixer",
    )(h, w_out, xres, ln_g[None, :], ln_b[None, :], rw, rb)


def _moe_kernel(be_ref, nu_ref, xs_ref, wg_ref, wu_ref, wd_ref, bg_ref, bu_ref, bd_ref, y_ref, acc_ref):
    i = pl.program_id(0)
    j = pl.program_id(1)
    last = pl.num_programs(1) - 1
    used = i < nu_ref[0]

    @pl.when(used)
    def _():
        xs = xs_ref[...]
        bf16 = jnp.bfloat16
        h_gate = jnp.dot(xs, wg_ref[...].astype(bf16), preferred_element_type=jnp.float32) + bg_ref[...]
        h_up = jnp.dot(xs, wu_ref[...].astype(bf16), preferred_element_type=jnp.float32) + bu_ref[...]
        h_gate = jnp.minimum(h_gate, SWIGLU_LIMIT)
        h_up = jnp.clip(h_up, -SWIGLU_LIMIT, SWIGLU_LIMIT)
        hid = (h_up + 1.0) * (h_gate * _sigmoid(SWIGLU_ALPHA * h_gate))
        part = jnp.dot(hid.astype(bf16), wd_ref[...].astype(bf16), preferred_element_type=jnp.float32)

        @pl.when(j == 0)
        def _():
            acc_ref[...] = part

        @pl.when(j > 0)
        def _():
            acc_ref[...] += part

        @pl.when(j == last)
        def _():
            y_ref[...] = (acc_ref[...] + bd_ref[...]).astype(y_ref.dtype)

    @pl.when(jnp.logical_not(used) & (j == last))
    def _():
        y_ref[...] = jnp.zeros_like(y_ref)


def _moe_experts(xs, block_e, n_used, w_gate, w_up, w_down, b_gate, b_up, b_down, tm):
    n_slots, d = xs.shape
    n_experts, _, f = w_gate.shape
    tf = min(256, f)
    nf = f // tf
    n_blocks = n_slots // tm

    def jj(i, j, nu):
        return jnp.where(i < nu[0], j, nf - 1)

    grid_spec = pltpu.PrefetchScalarGridSpec(
        num_scalar_prefetch=2,
        grid=(n_blocks, nf),
        in_specs=[pl.BlockSpec((tm, d), lambda i, j, be, nu: (jnp.minimum(i, nu[0] - 1), 0)),
                  pl.BlockSpec((None, d, tf), lambda i, j, be, nu: (be[i], 0, jj(i, j, nu))),
                  pl.BlockSpec((None, d, tf), lambda i, j, be, nu: (be[i], 0, jj(i, j, nu))),
                  pl.BlockSpec((None, tf, d), lambda i, j, be, nu: (be[i], jj(i, j, nu), 0)),
                  pl.BlockSpec((None, 1, tf), lambda i, j, be, nu: (be[i], 0, jj(i, j, nu))),
                  pl.BlockSpec((None, 1, tf), lambda i, j, be, nu: (be[i], 0, jj(i, j, nu))),
                  pl.BlockSpec((None, 1, d), lambda i, j, be, nu: (be[i], 0, 0))],
        out_specs=pl.BlockSpec((tm, d), lambda i, j, be, nu: (i, 0)),
        scratch_shapes=[pltpu.VMEM((tm, d), jnp.float32)],
    )
    return pl.pallas_call(
        _moe_kernel,
        grid_spec=grid_spec,
        out_shape=jax.ShapeDtypeStruct((n_slots, d), jnp.bfloat16),
        compiler_params=_cparams(("arbitrary", "arbitrary")),
        name="moe_experts",
    )(block_e, n_used, xs, w_gate, w_up, w_down, b_gate[:, None, :], b_up[:, None, :], b_down[:, None, :])


def _combine_kernel(yg_ref, gate_ref, x1_ref, g_ref, b_ref, o_ref, ob_ref, *, alpha):
    gates = gate_ref[...]
    f = yg_ref[0].astype(jnp.float32) * gates[:, 0:1]
    for r in range(1, TOP_K):
        f = f + yg_ref[r].astype(jnp.float32) * gates[:, r:r + 1]
    out = _layer_norm_rows(alpha * x1_ref[...] + f, g_ref[...], b_ref[...])
    o_ref[...] = out
    ob_ref[...] = out.astype(ob_ref.dtype)


def _moe_combine(yg, gates, x1, ln_g, ln_b, alpha):
    n, d = x1.shape
    tm = min(256, n)
    return pl.pallas_call(
        functools.partial(_combine_kernel, alpha=alpha),
        grid=(n // tm,),
        in_specs=[pl.BlockSpec((TOP_K, tm, d), lambda i: (0, i, 0)),
                  pl.BlockSpec((tm, LANES), lambda i: (i, 0)),
                  pl.BlockSpec((tm, d), lambda i: (i, 0)),
                  pl.BlockSpec((1, d), lambda i: (0, 0)),
                  pl.BlockSpec((1, d), lambda i: (0, 0))],
        out_specs=[pl.BlockSpec((tm, d), lambda i: (i, 0)),
                   pl.BlockSpec((tm, d), lambda i: (i, 0))],
        out_shape=[jax.ShapeDtypeStruct((n, d), jnp.float32),
                   jax.ShapeDtypeStruct((n, d), jnp.bfloat16)],
        compiler_params=_cparams(("parallel",)),
        name="moe_combine",
    )(yg, gates, x1, ln_g[None, :], ln_b[None, :])


def _moe_layer(x1, x1b, topi, topg, rank, counts, ln_g, ln_b, w_gate, b_gate, w_up, b_up, w_down, b_down, alpha):
    n_tok, d = x1.shape
    n_experts = w_gate.shape[0]
    n_assign = n_tok * TOP_K
    tm = min(512, n_assign // n_experts)
    flat_e = topi[:, :TOP_K].reshape(-1)
    rank = rank[:, :TOP_K].reshape(-1)
    counts = counts[0, :n_experts].astype(jnp.int32)
    padded = (counts + tm - 1) // tm * tm
    pends = jnp.cumsum(padded)
    dest = (pends - padded)[flat_e] + rank
    n_blocks = n_assign // tm + n_experts
    flat_tok = jnp.arange(n_assign, dtype=jnp.int32) // TOP_K
    rows = jnp.zeros((n_blocks * tm,), jnp.int32).at[dest].set(flat_tok)
    block_e = jnp.minimum(jnp.searchsorted(pends, jnp.arange(n_blocks, dtype=jnp.int32) * tm, side='right'),
                          n_experts - 1).astype(jnp.int32)
    n_used = (pends[-1:] // tm).astype(jnp.int32)
    xs = jnp.take(x1b, rows, axis=0)
    y = _moe_experts(xs, block_e, n_used, w_gate, w_up, w_down, b_gate, b_up, b_down, tm)
    yg = jnp.take(y, dest.reshape(n_tok, TOP_K).T, axis=0)
    return _moe_combine(yg, topg, x1, ln_g, ln_b, alpha)


def kernel(x, dn_w_in, dn_conv_w, dn_a_log, dn_dt_bias, dn_norm_w, dn_w_out, mb_w_in, mb_w_out, ln_g, ln_b,
           router_w, router_b, w_gate, b_gate, w_up, b_up, w_down, b_down):
    bsz, t_len, d = x.shape
    depth = ln_g.shape[0]
    alpha = float((2 * depth) ** 0.25)
    n = bsz * t_len
    xf = x.reshape(n, d)
    xb = xf.astype(jnp.bfloat16)
    for i in range(depth):
        j = i // 2
        if i % 2 == 0:
            n_vheads = d // HEAD_DIM
            main = dn_w_in.shape[2] - 2 * n_vheads
            proj = _matmul(xb, dn_w_in[j].astype(jnp.bfloat16), main, jnp.bfloat16, 1024, 512)
            gates = _dn_gates(xf, dn_w_in[j][:, main:], dn_a_log[j], dn_dt_bias[j], n_vheads)
            gates_t = jnp.transpose(gates.reshape(bsz, t_len, LANES), (0, 2, 1))
            h = _dn_delta_rule(proj, gates, gates_t, dn_conv_w[j], dn_norm_w[j], bsz, t_len, d)
            w_out = dn_w_out[j]
        else:
            proj = _matmul(xb, mb_w_in[j].astype(jnp.bfloat16), mb_w_in.shape[2], jnp.bfloat16, 1024, 512)
            h = _moba_attention(proj, bsz, t_len, d // HEAD_DIM)
            w_out = mb_w_out[j]
        x1, x1b, topi, topg, rank, counts = _post_mixer(h, w_out.astype(jnp.bfloat16), xf, ln_g[i, 0], ln_b[i, 0],
                                                        router_w[i], router_b[i], alpha)
        xf, xb = _moe_layer(x1, x1b, topi, topg, rank, counts, ln_g[i, 1], ln_b[i, 1], w_gate[i], b_gate[i],
                            w_up[i], b_up[i], w_down[i], b_down[i], alpha)
    return xf.reshape(bsz, t_len, d)
```

```python
import functools

import jax
import jax.numpy as jnp
import numpy as np
from jax import lax
from jax.experimental import pallas as pl
from jax.experimental.pallas import tpu as pltpu

HEAD_DIM = 128
DN_CONV = 4
DN_CHUNK = 256
DN_GROUP = 2
MB_BLOCK = 256
MB_TOPK = 3
TOP_K = 4
SWIGLU_LIMIT = 7.0
SWIGLU_ALPHA = 1.702
LN_EPS = 1e-5
NORM_EPS = 1e-6
NEG_INF = -1e30
LANES = 128
HALO = 16
VMEM_LIMIT = 56 * 1024 * 1024

_HI = lax.Precision.HIGHEST


def _cparams(sem):
    return pltpu.CompilerParams(dimension_semantics=sem, vmem_limit_bytes=VMEM_LIMIT)


def _sigmoid(v):
    return 1.0 / (1.0 + jnp.exp(-v))


def _silu(v):
    return v * _sigmoid(v)


def _mm_kernel(x_ref, w_ref, o_ref):
    o_ref[...] = jnp.dot(x_ref[...], w_ref[...], preferred_element_type=jnp.float32).astype(o_ref.dtype)


def _matmul(x, w, n, out_dtype, tm, tn):
    m, k = x.shape
    tm, tn = min(tm, m), min(tn, n)
    assert n % tn == 0 and m % tm == 0
    return pl.pallas_call(
        _mm_kernel,
        grid=(m // tm, n // tn),
        in_specs=[pl.BlockSpec((tm, k), lambda i, j: (i, 0)),
                  pl.BlockSpec((k, tn), lambda i, j: (0, j))],
        out_specs=pl.BlockSpec((tm, tn), lambda i, j: (i, j)),
        out_shape=jax.ShapeDtypeStruct((m, n), out_dtype),
        compiler_params=_cparams(("parallel", "arbitrary")),
        name="dense_matmul",
    )(x, w)


def _dn_gates_kernel(x_ref, w_ref, alog_ref, dtb_ref, o_ref, *, n_heads):
    tm = x_ref.shape[0]
    ba = jnp.dot(x_ref[...], w_ref[...], precision=_HI, preferred_element_type=jnp.float32)
    lane = lax.broadcasted_iota(jnp.int32, ba.shape, 1)
    beta = _sigmoid(ba)
    v = ba + dtb_ref[...]
    softplus = jnp.maximum(v, 0.0) + jnp.log(1.0 + jnp.exp(-jnp.abs(v)))
    g = -jnp.exp(alog_ref[...]) * softplus
    g = jnp.where((lane >= n_heads) & (lane < 2 * n_heads), g, 0.0)
    ri = lax.broadcasted_iota(jnp.int32, (tm, tm), 0)
    ci = lax.broadcasted_iota(jnp.int32, (tm, tm), 1)
    shift = DN_CHUNK.bit_length() - 1
    tril = jnp.where((ci <= ri) & ((ri >> shift) == (ci >> shift)), 1.0, 0.0)
    gc = jnp.dot(tril, g, precision=_HI, preferred_element_type=jnp.float32)
    o_ref[...] = jnp.where(lane < n_heads, beta, gc)


def _dn_gates(x2d, w_ba, a_log, dt_bias, n_heads):
    n, d = x2d.shape
    tm = min(256, n)
    pad = LANES - 2 * n_heads
    w_p = jnp.pad(w_ba, ((0, 0), (0, pad)))
    alog_p = jnp.pad(a_log, (n_heads, pad))[None, :]
    dtb_p = jnp.pad(dt_bias, (n_heads, pad))[None, :]
    return pl.pallas_call(
        functools.partial(_dn_gates_kernel, n_heads=n_heads),
        grid=(n // tm,),
        in_specs=[pl.BlockSpec((tm, d), lambda i: (i, 0)),
                  pl.BlockSpec((d, LANES), lambda i: (0, 0)),
                  pl.BlockSpec((1, LANES), lambda i: (0, 0)),
                  pl.BlockSpec((1, LANES), lambda i: (0, 0))],
        out_specs=pl.BlockSpec((tm, LANES), lambda i: (i, 0)),
        out_shape=jax.ShapeDtypeStruct((n, LANES), jnp.float32),
        compiler_params=_cparams(("parallel",)),
        name="dn_gates",
    )(x2d, w_p, alog_p, dtb_p)


def _causal_conv_silu(x_ref, halo_ref, w_ref, first):
    tb = x_ref.shape[0]
    halo = jnp.where(first, 0.0, halo_ref[...].astype(jnp.float32))
    xcat = jnp.concatenate([halo, x_ref[...].astype(jnp.float32)], axis=0)
    w = w_ref[...]
    acc = xcat[HALO:] * w[DN_CONV - 1:DN_CONV, :]
    for s in range(1, DN_CONV):
        acc = acc + pltpu.roll(xcat, s, 0)[HALO:] * w[DN_CONV - 1 - s:DN_CONV - s, :]
    del tb
    return _silu(acc)


def _l2norm(v):
    return v * lax.rsqrt(jnp.sum(v * v, axis=-1, keepdims=True) + NORM_EPS)


def _inverse_level_masks(c):
    ri = np.arange(c)[:, None]
    ci = np.arange(c)[None, :]
    masks = [((ri >> 1) == (ci >> 1)) & (ri > ci)]
    s = 2
    while s < c:
        masks.append(((ri // (2 * s)) == (ci // (2 * s))) & ((ri & s) != 0) & ((ci & s) == 0))
        s *= 2
    return jnp.asarray(np.stack(masks), jnp.bfloat16)


def _dn_kernel(q_ref, k_ref, v_ref, z_ref, qh_ref, kh_ref, vh_ref, g_ref, gt_ref,
               wq_ref, wk_ref, wv_ref, nw_ref, lm_ref, negc_ref, o_ref, state_ref, *, n_vheads):
    hg = pl.program_id(1)
    t = pl.program_id(2)
    first = t == 0
    tb = q_ref.shape[0]
    group = q_ref.shape[1] // HEAD_DIM
    rep = v_ref.shape[1] // q_ref.shape[1]
    bf16, f32 = jnp.bfloat16, jnp.float32

    @pl.when(first)
    def _():
        state_ref[...] = jnp.zeros_like(state_ref)

    def head(x, c):
        return x[:, c * HEAD_DIM:(c + 1) * HEAD_DIM]

    q_all = _causal_conv_silu(q_ref, qh_ref, wq_ref, first)
    k_all = _causal_conv_silu(k_ref, kh_ref, wk_ref, first)
    v = _causal_conv_silu(v_ref, vh_ref, wv_ref, first)
    q = [_l2norm(head(q_all, g)) * (HEAD_DIM ** -0.5) for g in range(group)]
    k = [_l2norm(head(k_all, g)) for g in range(group)]
    gates = g_ref[...]
    lane = lax.broadcasted_iota(jnp.int32, gates.shape, 1)
    eye = jnp.where(lax.broadcasted_iota(jnp.int32, (tb, tb), 0) == lax.broadcasted_iota(jnp.int32, (tb, tb), 1),
                    1.0, 0.0)
    nt = (((1,), (1,)), ((), ()))
    k_b = [k[g].astype(bf16) for g in range(group)]
    kk = [lax.dot_general(k_b[g], k_b[g], nt, preferred_element_type=f32) for g in range(group)]
    qk = [lax.dot_general(q[g].astype(bf16), k_b[g], nt, preferred_element_type=f32) for g in range(group)]

    heads = range(group * rep)
    beta, gcol, g_last, decay, a_b, inv = [], [], [], [], [], []
    for c in heads:
        hv = hg * group * rep + c
        beta.append(jnp.sum(jnp.where(lane == hv, gates, 0.0), axis=-1, keepdims=True))
        gcol.append(jnp.sum(jnp.where(lane == n_vheads + hv, gates, 0.0), axis=-1, keepdims=True))
        grow = gt_ref[pl.ds(n_vheads + hv, 1), :]
        g_last.append(grow[:, tb - 1:tb])
        decay.append(jnp.exp(gcol[c] - grow + negc_ref[...]))
        a_b.append((kk[c // rep] * beta[c] * decay[c]).astype(bf16))
        inv.append(eye - (a_b[c] * lm_ref[0]).astype(f32))
    for lvl in range(1, lm_ref.shape[0]):
        tmp = [jnp.dot(a_b[c] * lm_ref[lvl], inv[c].astype(bf16), preferred_element_type=f32) for c in heads]
        inv = [inv[c] - jnp.dot(inv[c].astype(bf16), tmp[c].astype(bf16), preferred_element_type=f32)
               for c in heads]
    eg = [jnp.exp(gcol[c]) for c in heads]
    uw = [jnp.dot(inv[c].astype(bf16),
                  jnp.concatenate([head(v, c) * beta[c], k[c // rep] * (beta[c] * eg[c])], axis=1).astype(bf16),
                  preferred_element_type=f32) for c in heads]
    state = [state_ref[c] for c in heads]
    ws = [jnp.dot(jnp.concatenate([uw[c][:, HEAD_DIM:], q[c // rep] * eg[c]], axis=0).astype(bf16),
                  state[c].astype(bf16), preferred_element_type=f32) for c in heads]
    v_new_b = [(uw[c][:, :HEAD_DIM] - ws[c][:tb]).astype(bf16) for c in heads]
    outs = []
    for c in heads:
        o = ws[c][tb:] + jnp.dot((qk[c // rep] * decay[c]).astype(bf16), v_new_b[c], preferred_element_type=f32)
        k_dec_t = (k[c // rep] * jnp.exp(g_last[c] - gcol[c])).T.astype(bf16)
        state_ref[c] = state[c] * jnp.exp(g_last[c]) + jnp.dot(k_dec_t, v_new_b[c], preferred_element_type=f32)
        zc = head(z_ref[...], c).astype(f32)
        outs.append(o * lax.rsqrt(jnp.mean(o * o, axis=-1, keepdims=True) + NORM_EPS) * nw_ref[...] * _silu(zc))
    o_ref[...] = jnp.concatenate(outs, axis=1).astype(o_ref.dtype)


def _dn_delta_rule(proj, gates, gates_t, conv_w, norm_w, bsz, t_len, d_model):
    n_kheads = d_model // (2 * HEAD_DIM)
    n_vheads = d_model // HEAD_DIM
    group = min(DN_GROUP, n_kheads)
    kw = group * HEAD_DIM
    vw = kw * (n_vheads // n_kheads)
    tb = DN_CHUNK
    nt = t_len // tb
    hb = tb // HALO
    level_masks = _inverse_level_masks(tb)
    ri = np.arange(tb)
    neg_causal = jnp.asarray(np.where(ri[None, :] <= ri[:, None], 0.0, NEG_INF), jnp.float32)
    k_off = n_kheads * HEAD_DIM // kw
    v_off = 2 * n_kheads * HEAD_DIM // vw
    z_off = v_off + n_vheads * HEAD_DIM // vw

    def row(b, h, t):
        return b * nt + t

    def halo_row(b, h, t):
        return jnp.maximum((b * nt + t) * hb - 1, 0)

    in_specs = [
        pl.BlockSpec((tb, kw), lambda b, h, t: (row(b, h, t), h)),
        pl.BlockSpec((tb, kw), lambda b, h, t: (row(b, h, t), k_off + h)),
        pl.BlockSpec((tb, vw), lambda b, h, t: (row(b, h, t), v_off + h)),
        pl.BlockSpec((tb, vw), lambda b, h, t: (row(b, h, t), z_off + h)),
        pl.BlockSpec((HALO, kw), lambda b, h, t: (halo_row(b, h, t), h)),
        pl.BlockSpec((HALO, kw), lambda b, h, t: (halo_row(b, h, t), k_off + h)),
        pl.BlockSpec((HALO, vw), lambda b, h, t: (halo_row(b, h, t), v_off + h)),
        pl.BlockSpec((tb, LANES), lambda b, h, t: (row(b, h, t), 0)),
        pl.BlockSpec((None, LANES, tb), lambda b, h, t: (b, 0, t)),
        pl.BlockSpec((DN_CONV, kw), lambda b, h, t: (0, h)),
        pl.BlockSpec((DN_CONV, kw), lambda b, h, t: (0, k_off + h)),
        pl.BlockSpec((DN_CONV, vw), lambda b, h, t: (0, v_off + h)),
        pl.BlockSpec((1, HEAD_DIM), lambda b, h, t: (0, 0)),
        pl.BlockSpec(level_masks.shape, lambda b, h, t: (0, 0, 0)),
        pl.BlockSpec((tb, tb), lambda b, h, t: (0, 0)),
    ]
    return pl.pallas_call(
        functools.partial(_dn_kernel, n_vheads=n_vheads),
        grid=(bsz, n_kheads // group, nt),
        in_specs=in_specs,
        out_specs=pl.BlockSpec((tb, vw), lambda b, h, t: (row(b, h, t), h)),
        out_shape=jax.ShapeDtypeStruct((bsz * t_len, n_vheads * HEAD_DIM), jnp.bfloat16),
        scratch_shapes=[pltpu.VMEM((vw // HEAD_DIM, HEAD_DIM, HEAD_DIM), jnp.float32)],
        compiler_params=_cparams(("parallel", "parallel", "arbitrary")),
        name="dn_delta_rule",
    )(proj, proj, proj, proj, proj, proj, proj, gates, gates_t, conv_w, conv_w, conv_w, norm_w[None, :],
      level_masks, neg_causal)


def _split3(a):
    hi = a.astype(jnp.bfloat16).astype(jnp.float32)
    mid = (a - hi).astype(jnp.bfloat16).astype(jnp.float32)
    lo = (a - hi - mid).astype(jnp.bfloat16).astype(jnp.float32)
    return hi, mid, lo


def _moba_kernel(slope_ref, q_ref, k_ref, vt_ref, o_ref, kmean_ref, qaug_ref, kaug_ref, s_ref, m_ref, l_ref, acc_ref):
    h = pl.program_id(1)
    own = pl.program_id(2)
    tq = q_ref.shape[0]
    t_len = k_ref.shape[0]
    nb = t_len // MB_BLOCK
    nt = (((1,), (1,)), ((), ()))
    log2e = 1.4426950408889634
    slope2 = slope_ref[h] * log2e

    @pl.when(own == 0)
    def _():
        kf = k_ref[...].astype(jnp.float32).reshape(nb, MB_BLOCK, HEAD_DIM)
        kmean = jnp.sum(kf, axis=1) * (1.0 / MB_BLOCK)
        kmean_hi = kmean.astype(jnp.bfloat16)
        kmean_lo = (kmean - kmean_hi.astype(jnp.float32)).astype(jnp.bfloat16)
        kmean_ref[...] = jnp.concatenate([kmean_hi, kmean_lo], axis=0)
        pos = lax.broadcasted_iota(jnp.int32, (MB_BLOCK, LANES), 0).astype(jnp.float32)
        col = lax.broadcasted_iota(jnp.int32, (MB_BLOCK, LANES), 1)
        q_hi, q_mid, q_lo = _split3(pos * (-slope2))
        k_hi, k_mid, k_lo = _split3(pos * slope2)
        one = jnp.where(col < 6, 1.0, 0.0)
        qaug = jnp.where(col == 0, q_hi, jnp.where(col == 1, q_mid, jnp.where(col == 2, q_lo, one)))
        kaug = jnp.where(col == 3, k_hi, jnp.where(col == 4, k_mid, jnp.where(col == 5, k_lo, one)))
        qaug_ref[...] = qaug.astype(qaug_ref.dtype)
        kaug_ref[...] = kaug.astype(kaug_ref.dtype)

    q = q_ref[...]
    qf = q.astype(jnp.float32)
    gate2 = lax.dot_general(kmean_ref[...], q, nt, preferred_element_type=jnp.float32)
    gate = gate2[:nb] + gate2[nb:]
    blk = lax.broadcasted_iota(jnp.int32, gate.shape, 0).astype(jnp.float32)
    gate = jnp.where(blk < own.astype(jnp.float32), gate, -jnp.inf)
    picks = jnp.zeros((1, tq), jnp.int32)
    for r in range(MB_TOPK):
        mx = jnp.max(gate, axis=0, keepdims=True)
        idx = jnp.min(jnp.where(gate == mx, blk, float(nb)), axis=0, keepdims=True)
        bit = jnp.left_shift(1, idx.astype(jnp.int32))
        picks = picks | jnp.where(r < own, bit, 0)
        gate = jnp.where(blk == idx, -jnp.inf, gate)

    q2 = jnp.concatenate([(qf * (HEAD_DIM ** -0.5 * log2e)).astype(jnp.bfloat16), qaug_ref[...]], axis=1)
    kaug = kaug_ref[...]

    def scores(n):
        start = pl.multiple_of(n * MB_BLOCK, MB_BLOCK)
        k2 = jnp.concatenate([k_ref[pl.ds(start, MB_BLOCK), :], kaug], axis=1)
        return lax.dot_general(k2, q2, nt, preferred_element_type=jnp.float32)

    def values_t(n):
        return vt_ref[:, pl.ds(pl.multiple_of(n * MB_BLOCK, MB_BLOCK), MB_BLOCK)]

    def past_scores(n):
        picked = (picks & jnp.left_shift(1, n)) != 0
        far = (own - n).astype(jnp.float32) * (-slope2 * MB_BLOCK)
        return scores(n) + jnp.where(picked, far, NEG_INF)

    def fold(op, x):
        parts = [x[r:r + 8] for r in range(0, x.shape[0], 8)]
        while len(parts) > 1:
            parts = [op(parts[a], parts[a + 1]) for a in range(0, len(parts) - 1, 2)] + parts[len(parts) & ~1:]
        return parts[0]

    def stage_pair(slot, i):
        s_ref[slot, :MB_BLOCK, :] = past_scores(jnp.minimum(2 * i, nb - 1))
        s_ref[slot, MB_BLOCK:, :] = past_scores(jnp.minimum(2 * i + 1, nb - 1))

    def pair_values_t(i):
        return vt_ref[:, pl.ds(pl.multiple_of(2 * i * MB_BLOCK, 2 * MB_BLOCK), 2 * MB_BLOCK)]

    def absorb(s, v_t):
        m = m_ref[...]
        m_new = jnp.maximum(m, jnp.max(fold(jnp.maximum, s), axis=0, keepdims=True))
        p = jnp.exp2(s - m_new)
        alpha = jnp.exp2(m - m_new)
        m_ref[...] = m_new
        l_ref[...] = alpha * l_ref[...] + fold(jnp.add, p)
        acc_ref[...] = alpha * acc_ref[...] + jnp.dot(v_t, p.astype(jnp.bfloat16),
                                                      preferred_element_type=jnp.float32)

    n_pairs = (own + 1) // 2

    def two_pairs(j, carry):
        i = 2 * j
        stage_pair(1, i + 1)
        absorb(s_ref[0], pair_values_t(i))

        @pl.when(i + 1 < n_pairs)
        def _():
            stage_pair(0, i + 2)
            absorb(s_ref[1], pair_values_t(i + 1))
        return carry

    m_ref[...] = jnp.full(m_ref.shape, NEG_INF, jnp.float32)
    l_ref[...] = jnp.zeros_like(l_ref)
    acc_ref[...] = jnp.zeros_like(acc_ref)
    stage_pair(0, 0)
    lax.fori_loop(0, (n_pairs + 1) // 2, two_pairs, 0)
    key_pos = lax.broadcasted_iota(jnp.int32, (MB_BLOCK, tq), 0)
    query_pos = lax.broadcasted_iota(jnp.int32, (MB_BLOCK, tq), 1)
    absorb(jnp.where(key_pos <= query_pos, scores(own), NEG_INF), values_t(own))
    l = jnp.sum(l_ref[...], axis=0, keepdims=True)
    o_ref[...] = (acc_ref[...] / l).T.astype(o_ref.dtype)


def _moba_attention(proj, bsz, t_len, n_heads):
    tq = MB_BLOCK
    nq = t_len // tq
    nb = t_len // MB_BLOCK
    assert nb <= 32, "per-row picks are kept as one int32 bit mask"
    slopes = jnp.exp2(-8.0 * jnp.arange(1, n_heads + 1, dtype=jnp.float32) / n_heads)
    v_t = jnp.transpose(proj[:, 2 * n_heads * HEAD_DIM:])
    grid_spec = pltpu.PrefetchScalarGridSpec(
        num_scalar_prefetch=1,
        grid=(bsz, n_heads, nq),
        in_specs=[pl.BlockSpec((tq, HEAD_DIM), lambda b, h, i, s: (b * nq + i, h)),
                  pl.BlockSpec((t_len, HEAD_DIM), lambda b, h, i, s: (b, n_heads + h)),
                  pl.BlockSpec((HEAD_DIM, t_len), lambda b, h, i, s: (h, b))],
        out_specs=pl.BlockSpec((tq, HEAD_DIM), lambda b, h, i, s: (b * nq + i, h)),
        scratch_shapes=[pltpu.VMEM((2 * nb, HEAD_DIM), jnp.bfloat16),
                        pltpu.VMEM((MB_BLOCK, LANES), jnp.bfloat16),
                        pltpu.VMEM((MB_BLOCK, LANES), jnp.bfloat16),
                        pltpu.VMEM((2, 2 * MB_BLOCK, tq), jnp.float32),
                        pltpu.VMEM((1, tq), jnp.float32),
                        pltpu.VMEM((8, tq), jnp.float32),
                        pltpu.VMEM((HEAD_DIM, tq), jnp.float32)],
    )
    return pl.pallas_call(
        _moba_kernel,
        grid_spec=grid_spec,
        out_shape=jax.ShapeDtypeStruct((bsz * t_len, n_heads * HEAD_DIM), jnp.bfloat16),
        compiler_params=_cparams(("parallel", "parallel", "arbitrary")),
        name="moba_attention",
    )(slopes, proj, proj, v_t)


def _layer_norm_rows(y, g, b):
    mu = jnp.mean(y, axis=-1, keepdims=True)
    yc = y - mu
    var = jnp.mean(yc * yc, axis=-1, keepdims=True)
    return yc * lax.rsqrt(var + LN_EPS) * g + b


def _post_mixer_kernel(h_ref, w_ref, xres_ref, g_ref, b_ref, rw_ref, rb_ref,
                       x1_ref, x1b_ref, topi_ref, topg_ref, rank_ref, count_ref, *, alpha, n_experts):
    i = pl.program_id(0)
    j = pl.program_id(1)
    tm = h_ref.shape[0]
    tn = w_ref.shape[1]
    col = pl.multiple_of(j * tn, tn)
    x1_ref[:, pl.ds(col, tn)] = jnp.dot(h_ref[...], w_ref[...], preferred_element_type=jnp.float32)

    @pl.when((i == 0) & (j == 0))
    def _():
        count_ref[...] = jnp.zeros_like(count_ref)

    @pl.when(j == pl.num_programs(1) - 1)
    def _():
        x1 = _layer_norm_rows(alpha * xres_ref[...] + x1_ref[...], g_ref[...], b_ref[...])
        x1_ref[...] = x1
        x1_hi = x1.astype(jnp.bfloat16)
        x1b_ref[...] = x1_hi
        x1_lo = (x1 - x1_hi.astype(jnp.float32)).astype(jnp.bfloat16)
        hi_terms = jnp.dot(x1_hi, rw_ref[...], preferred_element_type=jnp.float32)
        lo_term = jnp.dot(x1_lo, rw_ref[:, :LANES], preferred_element_type=jnp.float32)
        logits = hi_terms[:, :LANES] + hi_terms[:, LANES:] + lo_term + rb_ref[...]
        lane_i = lax.broadcasted_iota(jnp.int32, logits.shape, 1)
        lane = lane_i.astype(jnp.float32)
        logits = jnp.where(lane_i < n_experts, logits, -jnp.inf)
        topi = jnp.zeros(logits.shape, jnp.float32)
        topv = jnp.full(logits.shape, -jnp.inf, jnp.float32)
        chosen = []
        for r in range(TOP_K):
            mx = jnp.max(logits, axis=-1, keepdims=True)
            idx = jnp.min(jnp.where(logits == mx, lane, float(LANES)), axis=-1, keepdims=True)
            hit = lane == idx
            chosen.append(hit)
            topi = jnp.where(lane_i == r, idx, topi)
            topv = jnp.where(lane_i == r, mx, topv)
            logits = jnp.where(hit, -jnp.inf, logits)
        e = jnp.exp(topv - jnp.max(topv, axis=-1, keepdims=True))
        topi_ref[...] = topi.astype(jnp.int32)
        topg_ref[...] = e / jnp.sum(e, axis=-1, keepdims=True)
        onehot = jnp.where(functools.reduce(jnp.logical_or, chosen), 1.0, 0.0)
        earlier = jnp.where(lax.broadcasted_iota(jnp.int32, (tm, tm), 1) < lax.broadcasted_iota(jnp.int32, (tm, tm), 0),
                            1.0, 0.0).astype(jnp.bfloat16)
        before = count_ref[...] + jnp.dot(earlier, onehot.astype(jnp.bfloat16), preferred_element_type=jnp.float32)
        rank = jnp.zeros(logits.shape, jnp.float32)
        for r in range(TOP_K):
            rank = jnp.where(lane_i == r, jnp.sum(jnp.where(chosen[r], before, 0.0), axis=-1, keepdims=True), rank)
        rank_ref[...] = rank.astype(jnp.int32)
        count_ref[...] += jnp.sum(onehot, axis=0, keepdims=True)


def _post_mixer(h, w_out, xres, ln_g, ln_b, router_w, router_b, alpha):
    n, kdim = h.shape
    d = w_out.shape[1]
    n_experts = router_w.shape[1]
    tm, tn = min(256, n), min(512, d)
    rw = jnp.pad(router_w, ((0, 0), (0, LANES - n_experts)))
    rw_hi = rw.astype(jnp.bfloat16)
    rw_lo = (rw - rw_hi.astype(jnp.float32)).astype(jnp.bfloat16)
    rw = jnp.concatenate([rw_hi, rw_lo], axis=1)
    rb = jnp.pad(router_b, (0, LANES - n_experts))[None, :]
    row = lambda i, j: (i, 0)
    fixed = lambda i, j: (0, 0)
    return pl.pallas_call(
        functools.partial(_post_mixer_kernel, alpha=alpha, n_experts=n_experts),
        grid=(n // tm, d // tn),
        in_specs=[pl.BlockSpec((tm, kdim), row),
                  pl.BlockSpec((kdim, tn), lambda i, j: (0, j)),
                  pl.BlockSpec((tm, d), row),
                  pl.BlockSpec((1, d), fixed),
                  pl.BlockSpec((1, d), fixed),
                  pl.BlockSpec((d, 2 * LANES), fixed),
                  pl.BlockSpec((1, LANES), fixed)],
        out_specs=[pl.BlockSpec((tm, d), row),
                   pl.BlockSpec((tm, d), row),
                   pl.BlockSpec((tm, LANES), row),
                   pl.BlockSpec((tm, LANES), row),
                   pl.BlockSpec((tm, LANES), row),
                   pl.BlockSpec((1, LANES), fixed)],
        out_shape=[jax.ShapeDtypeStruct((n, d), jnp.float32),
                   jax.ShapeDtypeStruct((n, d), jnp.bfloat16),
                   jax.ShapeDtypeStruct((n, LANES), jnp.int32),
                   jax.ShapeDtypeStruct((n, LANES), jnp.float32),
                   jax.ShapeDtypeStruct((n, LANES), jnp.int32),
                   jax.ShapeDtypeStruct((1, LANES), jnp.float32)],
        compiler_params=_cparams(("arbitrary", "arbitrary")),
        name="post_mixer",
    )(h, w_out, xres, ln_g[None, :], ln_b[None, :], rw, rb)


def _moe_kernel(be_ref, nu_ref, xs_ref, wg_ref, wu_ref, wd_ref, bg_ref, bu_ref, bd_ref, y_ref, hid_ref, *, nf):
    i = pl.program_id(0)
    s = pl.program_id(1)
    used = i < nu_ref[0]
    tf = wg_ref.shape[1]
    td = wd_ref.shape[1]
    bf16 = jnp.bfloat16

    @pl.when(used & (s < nf))
    def _():
        xs = xs_ref[...]
        h_gate = jnp.dot(xs, wg_ref[...].astype(bf16), preferred_element_type=jnp.float32) + bg_ref[...]
        h_up = jnp.dot(xs, wu_ref[...].astype(bf16), preferred_element_type=jnp.float32) + bu_ref[...]
        h_gate = jnp.minimum(h_gate, SWIGLU_LIMIT)
        h_up = jnp.clip(h_up, -SWIGLU_LIMIT, SWIGLU_LIMIT)
        hid = (h_up + 1.0) * (h_gate * _sigmoid(SWIGLU_ALPHA * h_gate))
        hid_ref[:, pl.ds(pl.multiple_of(s * tf, tf), tf)] = hid.astype(bf16)

    @pl.when(used & (s >= nf))
    def _():
        col = pl.multiple_of((s - nf) * td, td)
        y = jnp.dot(hid_ref[...], wd_ref[...].astype(bf16), preferred_element_type=jnp.float32) + bd_ref[...]
        y_ref[:, pl.ds(col, td)] = y.astype(y_ref.dtype)

    @pl.when(jnp.logical_not(used) & (s == pl.num_programs(1) - 1))
    def _():
        y_ref[...] = jnp.zeros_like(y_ref)


def _moe_experts(xs, block_e, n_used, layer, w_gate, w_up, w_down, b_gate, b_up, b_down, tm):
    n_slots, d = xs.shape
    _, n_experts, _, f = w_gate.shape
    tf, td = min(256, f), min(2048, d)
    nf, nd = f // tf, d // td
    n_blocks = n_slots // tm

    def f_tile(i, s, nu):
        return jnp.where(i < nu[0], jnp.minimum(s, nf - 1), nf - 1)

    def d_tile(i, s, nu):
        return jnp.where(i < nu[0], jnp.clip(s - nf, 0, nd - 1), nd - 1)

    grid_spec = pltpu.PrefetchScalarGridSpec(
        num_scalar_prefetch=2,
        grid=(n_blocks, nf + nd),
        in_specs=[pl.BlockSpec((tm, d), lambda i, s, be, nu: (jnp.minimum(i, nu[0] - 1), 0)),
                  pl.BlockSpec((None, None, d, tf), lambda i, s, be, nu: (layer, be[i], 0, f_tile(i, s, nu))),
                  pl.BlockSpec((None, None, d, tf), lambda i, s, be, nu: (layer, be[i], 0, f_tile(i, s, nu))),
                  pl.BlockSpec((None, None, f, td), lambda i, s, be, nu: (layer, be[i], 0, d_tile(i, s, nu))),
                  pl.BlockSpec((None, None, 1, tf), lambda i, s, be, nu: (layer, be[i], 0, f_tile(i, s, nu))),
                  pl.BlockSpec((None, None, 1, tf), lambda i, s, be, nu: (layer, be[i], 0, f_tile(i, s, nu))),
                  pl.BlockSpec((None, None, 1, td), lambda i, s, be, nu: (layer, be[i], 0, d_tile(i, s, nu)))],
        out_specs=pl.BlockSpec((tm, d), lambda i, s, be, nu: (i, 0)),
        scratch_shapes=[pltpu.VMEM((tm, f), jnp.bfloat16)],
    )
    return pl.pallas_call(
        functools.partial(_moe_kernel, nf=nf),
        grid_spec=grid_spec,
        out_shape=jax.ShapeDtypeStruct((n_slots, d), jnp.bfloat16),
        compiler_params=_cparams(("arbitrary", "arbitrary")),
        name="moe_experts",
    )(block_e, n_used, xs, w_gate, w_up, w_down,
      b_gate[:, :, None, :], b_up[:, :, None, :], b_down[:, :, None, :])


def _combine_kernel(yg_ref, gate_ref, x1_ref, g_ref, b_ref, o_ref, ob_ref, *, alpha):
    gates = gate_ref[...]
    f = yg_ref[0].astype(jnp.float32) * gates[:, 0:1]
    for r in range(1, TOP_K):
        f = f + yg_ref[r].astype(jnp.float32) * gates[:, r:r + 1]
    out = _layer_norm_rows(alpha * x1_ref[...] + f, g_ref[...], b_ref[...])
    o_ref[...] = out
    ob_ref[...] = out.astype(ob_ref.dtype)


def _moe_combine(yg, gates, x1, ln_g, ln_b, alpha):
    n, d = x1.shape
    tm = min(256, n)
    return pl.pallas_call(
        functools.partial(_combine_kernel, alpha=alpha),
        grid=(n // tm,),
        in_specs=[pl.BlockSpec((TOP_K, tm, d), lambda i: (0, i, 0)),
                  pl.BlockSpec((tm, LANES), lambda i: (i, 0)),
                  pl.BlockSpec((tm, d), lambda i: (i, 0)),
                  pl.BlockSpec((1, d), lambda i: (0, 0)),
                  pl.BlockSpec((1, d), lambda i: (0, 0))],
        out_specs=[pl.BlockSpec((tm, d), lambda i: (i, 0)),
                   pl.BlockSpec((tm, d), lambda i: (i, 0))],
        out_shape=[jax.ShapeDtypeStruct((n, d), jnp.float32),
                   jax.ShapeDtypeStruct((n, d), jnp.bfloat16)],
        compiler_params=_cparams(("parallel",)),
        name="moe_combine",
    )(yg, gates, x1, ln_g[None, :], ln_b[None, :])


def _moe_layer(x1, x1b, topi, topg, rank, counts, ln_g, ln_b, layer, w_gate, b_gate, w_up, b_up, w_down, b_down,
               alpha):
    n_tok, d = x1.shape
    n_experts = w_gate.shape[1]
    n_assign = n_tok * TOP_K
    tm = min(512, n_assign // n_experts)
    flat_e = topi[:, :TOP_K].reshape(-1)
    rank = rank[:, :TOP_K].reshape(-1)
    counts = counts[0, :n_experts].astype(jnp.int32)
    padded = (counts + tm - 1) // tm * tm
    pends = jnp.cumsum(padded)
    dest = (pends - padded)[flat_e] + rank
    n_blocks = n_assign // tm + n_experts
    flat_tok = jnp.arange(n_assign, dtype=jnp.int32) // TOP_K
    rows = jnp.zeros((n_blocks * tm,), jnp.int32).at[dest].set(flat_tok, mode="promise_in_bounds",
                                                               unique_indices=True)
    block_e = jnp.minimum(jnp.searchsorted(pends, jnp.arange(n_blocks, dtype=jnp.int32) * tm, side='right'),
                          n_experts - 1).astype(jnp.int32)
    n_used = (pends[-1:] // tm).astype(jnp.int32)
    xs = x1b.at[rows].get(mode="promise_in_bounds")
    y = _moe_experts(xs, block_e, n_used, layer, w_gate, w_up, w_down, b_gate, b_up, b_down, tm)
    yg = y.at[dest.reshape(n_tok, TOP_K).T].get(mode="promise_in_bounds")
    return _moe_combine(yg, topg, x1, ln_g, ln_b, alpha)


def kernel(x, dn_w_in, dn_conv_w, dn_a_log, dn_dt_bias, dn_norm_w, dn_w_out, mb_w_in, mb_w_out, ln_g, ln_b,
           router_w, router_b, w_gate, b_gate, w_up, b_up, w_down, b_down):
    bsz, t_len, d = x.shape
    depth = ln_g.shape[0]
    alpha = float((2 * depth) ** 0.25)
    n = bsz * t_len
    xf = x.reshape(n, d)
    xb = xf.astype(jnp.bfloat16)
    for i in range(depth):
        j = i // 2
        if i % 2 == 0:
            n_vheads = d // HEAD_DIM
            main = dn_w_in.shape[2] - 2 * n_vheads
            proj = _matmul(xb, dn_w_in[j].astype(jnp.bfloat16), main, jnp.bfloat16, 1024, 512)
            gates = _dn_gates(xf, dn_w_in[j][:, main:], dn_a_log[j], dn_dt_bias[j], n_vheads)
            gates_t = jnp.transpose(gates.reshape(bsz, t_len, LANES), (0, 2, 1))
            h = _dn_delta_rule(proj, gates, gates_t, dn_conv_w[j], dn_norm_w[j], bsz, t_len, d)
            w_out = dn_w_out[j]
        else:
            proj = _matmul(xb, mb_w_in[j].astype(jnp.bfloat16), mb_w_in.shape[2], jnp.bfloat16, 1024, 512)
            h = _moba_attention(proj, bsz, t_len, d // HEAD_DIM)
            w_out = mb_w_out[j]
        x1, x1b, topi, topg, rank, counts = _post_mixer(h, w_out.astype(jnp.bfloat16), xf, ln_g[i, 0], ln_b[i, 0],
                                                        router_w[i], router_b[i], alpha)
        xf, xb = _moe_layer(x1, x1b, topi, topg, rank, counts, ln_g[i, 1], ln_b[i, 1], i, w_gate, b_gate,
                            w_up, b_up, w_down, b_down, alpha)
    return xf.reshape(bsz, t_len, d)
```

```python
import functools

import jax
import jax.numpy as jnp
import numpy as np
from jax import lax
from jax.experimental import pallas as pl
from jax.experimental.pallas import tpu as pltpu

HEAD_DIM = 128
DN_CONV = 4
DN_CHUNK = 256
DN_GROUP = 2
MB_BLOCK = 256
MB_TOPK = 3
TOP_K = 4
SWIGLU_LIMIT = 7.0
SWIGLU_ALPHA = 1.702
LN_EPS = 1e-5
NORM_EPS = 1e-6
NEG_INF = -1e30
LANES = 128
HALO = 16
VMEM_LIMIT = 56 * 1024 * 1024

_HI = lax.Precision.HIGHEST


def _cparams(sem):
    return pltpu.CompilerParams(dimension_semantics=sem, vmem_limit_bytes=VMEM_LIMIT)


def _sigmoid(v):
    return 1.0 / (1.0 + jnp.exp(-v))


def _silu(v):
    return v * _sigmoid(v)


def _mm_kernel(x_ref, w_ref, o_ref):
    o_ref[...] = jnp.dot(x_ref[...], w_ref[...], preferred_element_type=jnp.float32).astype(o_ref.dtype)


def _matmul(x, w, n, out_dtype, tm, tn):
    m, k = x.shape
    tm, tn = min(tm, m), min(tn, n)
    assert n % tn == 0 and m % tm == 0
    return pl.pallas_call(
        _mm_kernel,
        grid=(m // tm, n // tn),
        in_specs=[pl.BlockSpec((tm, k), lambda i, j: (i, 0)),
                  pl.BlockSpec((k, tn), lambda i, j: (0, j))],
        out_specs=pl.BlockSpec((tm, tn), lambda i, j: (i, j)),
        out_shape=jax.ShapeDtypeStruct((m, n), out_dtype),
        compiler_params=_cparams(("parallel", "arbitrary")),
        name="dense_matmul",
    )(x, w)


def _dn_gates_kernel(x_ref, w_ref, alog_ref, dtb_ref, o_ref, *, n_heads):
    tm = x_ref.shape[0]
    ba = jnp.dot(x_ref[...], w_ref[...], precision=_HI, preferred_element_type=jnp.float32)
    lane = lax.broadcasted_iota(jnp.int32, ba.shape, 1)
    beta = _sigmoid(ba)
    v = ba + dtb_ref[...]
    softplus = jnp.maximum(v, 0.0) + jnp.log(1.0 + jnp.exp(-jnp.abs(v)))
    g = -jnp.exp(alog_ref[...]) * softplus
    g = jnp.where((lane >= n_heads) & (lane < 2 * n_heads), g, 0.0)
    ri = lax.broadcasted_iota(jnp.int32, (tm, tm), 0)
    ci = lax.broadcasted_iota(jnp.int32, (tm, tm), 1)
    shift = DN_CHUNK.bit_length() - 1
    tril = jnp.where((ci <= ri) & ((ri >> shift) == (ci >> shift)), 1.0, 0.0)
    gc = jnp.dot(tril, g, precision=_HI, preferred_element_type=jnp.float32)
    o_ref[...] = jnp.where(lane < n_heads, beta, gc)


def _dn_gates(x2d, w_ba, a_log, dt_bias, n_heads):
    n, d = x2d.shape
    tm = min(256, n)
    pad = LANES - 2 * n_heads
    w_p = jnp.pad(w_ba, ((0, 0), (0, pad)))
    alog_p = jnp.pad(a_log, (n_heads, pad))[None, :]
    dtb_p = jnp.pad(dt_bias, (n_heads, pad))[None, :]
    return pl.pallas_call(
        functools.partial(_dn_gates_kernel, n_heads=n_heads),
        grid=(n // tm,),
        in_specs=[pl.BlockSpec((tm, d), lambda i: (i, 0)),
                  pl.BlockSpec((d, LANES), lambda i: (0, 0)),
                  pl.BlockSpec((1, LANES), lambda i: (0, 0)),
                  pl.BlockSpec((1, LANES), lambda i: (0, 0))],
        out_specs=pl.BlockSpec((tm, LANES), lambda i: (i, 0)),
        out_shape=jax.ShapeDtypeStruct((n, LANES), jnp.float32),
        compiler_params=_cparams(("parallel",)),
        name="dn_gates",
    )(x2d, w_p, alog_p, dtb_p)


def _causal_conv_silu(x_ref, halo_ref, w_ref, first):
    tb = x_ref.shape[0]
    halo = jnp.where(first, 0.0, halo_ref[...].astype(jnp.float32))
    xcat = jnp.concatenate([halo, x_ref[...].astype(jnp.float32)], axis=0)
    w = w_ref[...]
    acc = xcat[HALO:] * w[DN_CONV - 1:DN_CONV, :]
    for s in range(1, DN_CONV):
        acc = acc + pltpu.roll(xcat, s, 0)[HALO:] * w[DN_CONV - 1 - s:DN_CONV - s, :]
    del tb
    return _silu(acc)


def _l2norm(v):
    return v * lax.rsqrt(jnp.sum(v * v, axis=-1, keepdims=True) + NORM_EPS)


def _inverse_level_masks(c):
    ri = np.arange(c)[:, None]
    ci = np.arange(c)[None, :]
    masks = [((ri >> 1) == (ci >> 1)) & (ri > ci)]
    s = 2
    while s < c:
        masks.append(((ri // (2 * s)) == (ci // (2 * s))) & ((ri & s) != 0) & ((ci & s) == 0))
        s *= 2
    return jnp.asarray(np.stack(masks), jnp.bfloat16)


def _dn_kernel(q_ref, k_ref, v_ref, z_ref, qh_ref, kh_ref, vh_ref, g_ref, gt_ref,
               wq_ref, wk_ref, wv_ref, nw_ref, lm_ref, negc_ref, o_ref, state_ref, *, n_vheads):
    hg = pl.program_id(1)
    t = pl.program_id(2)
    first = t == 0
    tb = q_ref.shape[0]
    group = q_ref.shape[1] // HEAD_DIM
    rep = v_ref.shape[1] // q_ref.shape[1]
    bf16, f32 = jnp.bfloat16, jnp.float32

    @pl.when(first)
    def _():
        state_ref[...] = jnp.zeros_like(state_ref)

    def head(x, c):
        return x[:, c * HEAD_DIM:(c + 1) * HEAD_DIM]

    q_all = _causal_conv_silu(q_ref, qh_ref, wq_ref, first)
    k_all = _causal_conv_silu(k_ref, kh_ref, wk_ref, first)
    v = _causal_conv_silu(v_ref, vh_ref, wv_ref, first)
    q = [_l2norm(head(q_all, g)) * (HEAD_DIM ** -0.5) for g in range(group)]
    k = [_l2norm(head(k_all, g)) for g in range(group)]
    gates = g_ref[...]
    lane = lax.broadcasted_iota(jnp.int32, gates.shape, 1)
    eye = jnp.where(lax.broadcasted_iota(jnp.int32, (tb, tb), 0) == lax.broadcasted_iota(jnp.int32, (tb, tb), 1),
                    1.0, 0.0)
    nt = (((1,), (1,)), ((), ()))
    k_b = [k[g].astype(bf16) for g in range(group)]
    kk = [lax.dot_general(k_b[g], k_b[g], nt, preferred_element_type=f32) for g in range(group)]
    qk = [lax.dot_general(q[g].astype(bf16), k_b[g], nt, preferred_element_type=f32) for g in range(group)]

    heads = range(group * rep)
    beta, gcol, g_last, decay, a_b, inv = [], [], [], [], [], []
    for c in heads:
        hv = hg * group * rep + c
        beta.append(jnp.sum(jnp.where(lane == hv, gates, 0.0), axis=-1, keepdims=True))
        gcol.append(jnp.sum(jnp.where(lane == n_vheads + hv, gates, 0.0), axis=-1, keepdims=True))
        grow = gt_ref[pl.ds(n_vheads + hv, 1), :]
        g_last.append(grow[:, tb - 1:tb])
        decay.append(jnp.exp(gcol[c] - grow + negc_ref[...]))
        a_b.append((kk[c // rep] * beta[c] * decay[c]).astype(bf16))
        inv.append(eye - (a_b[c] * lm_ref[0]).astype(f32))
    for lvl in range(1, lm_ref.shape[0]):
        tmp = [jnp.dot(a_b[c] * lm_ref[lvl], inv[c].astype(bf16), preferred_element_type=f32) for c in heads]
        inv = [inv[c] - jnp.dot(inv[c].astype(bf16), tmp[c].astype(bf16), preferred_element_type=f32)
               for c in heads]
    eg = [jnp.exp(gcol[c]) for c in heads]
    uw = [jnp.dot(inv[c].astype(bf16),
                  jnp.concatenate([head(v, c) * beta[c], k[c // rep] * (beta[c] * eg[c])], axis=1).astype(bf16),
                  preferred_element_type=f32) for c in heads]
    state = [state_ref[c] for c in heads]
    ws = [jnp.dot(jnp.concatenate([uw[c][:, HEAD_DIM:], q[c // rep] * eg[c]], axis=0).astype(bf16),
                  state[c].astype(bf16), preferred_element_type=f32) for c in heads]
    v_new_b = [(uw[c][:, :HEAD_DIM] - ws[c][:tb]).astype(bf16) for c in heads]
    outs = []
    for c in heads:
        o = ws[c][tb:] + jnp.dot((qk[c // rep] * decay[c]).astype(bf16), v_new_b[c], preferred_element_type=f32)
        k_dec_t = (k[c // rep] * jnp.exp(g_last[c] - gcol[c])).T.astype(bf16)
        state_ref[c] = state[c] * jnp.exp(g_last[c]) + jnp.dot(k_dec_t, v_new_b[c], preferred_element_type=f32)
        zc = head(z_ref[...], c).astype(f32)
        outs.append(o * lax.rsqrt(jnp.mean(o * o, axis=-1, keepdims=True) + NORM_EPS) * nw_ref[...] * _silu(zc))
    o_ref[...] = jnp.concatenate(outs, axis=1).astype(o_ref.dtype)


def _dn_delta_rule(proj, gates, gates_t, conv_w, norm_w, bsz, t_len, d_model):
    n_kheads = d_model // (2 * HEAD_DIM)
    n_vheads = d_model // HEAD_DIM
    group = min(DN_GROUP, n_kheads)
    kw = group * HEAD_DIM
    vw = kw * (n_vheads // n_kheads)
    tb = DN_CHUNK
    nt = t_len // tb
    hb = tb // HALO
    level_masks = _inverse_level_masks(tb)
    ri = np.arange(tb)
    neg_causal = jnp.asarray(np.where(ri[None, :] <= ri[:, None], 0.0, NEG_INF), jnp.float32)
    k_off = n_kheads * HEAD_DIM // kw
    v_off = 2 * n_kheads * HEAD_DIM // vw
    z_off = v_off + n_vheads * HEAD_DIM // vw

    def row(b, h, t):
        return b * nt + t

    def halo_row(b, h, t):
        return jnp.maximum((b * nt + t) * hb - 1, 0)

    in_specs = [
        pl.BlockSpec((tb, kw), lambda b, h, t: (row(b, h, t), h)),
        pl.BlockSpec((tb, kw), lambda b, h, t: (row(b, h, t), k_off + h)),
        pl.BlockSpec((tb, vw), lambda b, h, t: (row(b, h, t), v_off + h)),
        pl.BlockSpec((tb, vw), lambda b, h, t: (row(b, h, t), z_off + h)),
        pl.BlockSpec((HALO, kw), lambda b, h, t: (halo_row(b, h, t), h)),
        pl.BlockSpec((HALO, kw), lambda b, h, t: (halo_row(b, h, t), k_off + h)),
        pl.BlockSpec((HALO, vw), lambda b, h, t: (halo_row(b, h, t), v_off + h)),
        pl.BlockSpec((tb, LANES), lambda b, h, t: (row(b, h, t), 0)),
        pl.BlockSpec((None, LANES, tb), lambda b, h, t: (b, 0, t)),
        pl.BlockSpec((DN_CONV, kw), lambda b, h, t: (0, h)),
        pl.BlockSpec((DN_CONV, kw), lambda b, h, t: (0, k_off + h)),
        pl.BlockSpec((DN_CONV, vw), lambda b, h, t: (0, v_off + h)),
        pl.BlockSpec((1, HEAD_DIM), lambda b, h, t: (0, 0)),
        pl.BlockSpec(level_masks.shape, lambda b, h, t: (0, 0, 0)),
        pl.BlockSpec((tb, tb), lambda b, h, t: (0, 0)),
    ]
    return pl.pallas_call(
        functools.partial(_dn_kernel, n_vheads=n_vheads),
        grid=(bsz, n_kheads // group, nt),
        in_specs=in_specs,
        out_specs=pl.BlockSpec((tb, vw), lambda b, h, t: (row(b, h, t), h)),
        out_shape=jax.ShapeDtypeStruct((bsz * t_len, n_vheads * HEAD_DIM), jnp.bfloat16),
        scratch_shapes=[pltpu.VMEM((vw // HEAD_DIM, HEAD_DIM, HEAD_DIM), jnp.float32)],
        compiler_params=_cparams(("parallel", "parallel", "arbitrary")),
        name="dn_delta_rule",
    )(proj, proj, proj, proj, proj, proj, proj, gates, gates_t, conv_w, conv_w, conv_w, norm_w[None, :],
      level_masks, neg_causal)


def _split3(a):
    hi = a.astype(jnp.bfloat16).astype(jnp.float32)
    mid = (a - hi).astype(jnp.bfloat16).astype(jnp.float32)
    lo = (a - hi - mid).astype(jnp.bfloat16).astype(jnp.float32)
    return hi, mid, lo


def _moba_kernel(slope_ref, q_ref, k_ref, vt_ref, o_ref, kmean_ref, qaug_ref, kaug_ref, s_ref, m_ref, l_ref, acc_ref):
    h = pl.program_id(1)
    own = pl.program_id(2)
    tq = q_ref.shape[0]
    t_len = k_ref.shape[0]
    nb = t_len // MB_BLOCK
    nt = (((1,), (1,)), ((), ()))
    log2e = 1.4426950408889634
    slope2 = slope_ref[h] * log2e

    @pl.when(own == 0)
    def _():
        kf = k_ref[...].astype(jnp.float32).reshape(nb, MB_BLOCK, HEAD_DIM)
        kmean = jnp.sum(kf, axis=1) * (1.0 / MB_BLOCK)
        kmean_hi = kmean.astype(jnp.bfloat16)
        kmean_lo = (kmean - kmean_hi.astype(jnp.float32)).astype(jnp.bfloat16)
        kmean_ref[...] = jnp.concatenate([kmean_hi, kmean_lo], axis=0)
        pos = lax.broadcasted_iota(jnp.int32, (MB_BLOCK, LANES), 0).astype(jnp.float32)
        col = lax.broadcasted_iota(jnp.int32, (MB_BLOCK, LANES), 1)
        q_hi, q_mid, q_lo = _split3(pos * (-slope2))
        k_hi, k_mid, k_lo = _split3(pos * slope2)
        one = jnp.where(col < 6, 1.0, 0.0)
        qaug = jnp.where(col == 0, q_hi, jnp.where(col == 1, q_mid, jnp.where(col == 2, q_lo, one)))
        kaug = jnp.where(col == 3, k_hi, jnp.where(col == 4, k_mid, jnp.where(col == 5, k_lo, one)))
        qaug_ref[...] = qaug.astype(qaug_ref.dtype)
        kaug_ref[...] = kaug.astype(kaug_ref.dtype)

    q = q_ref[...]
    qf = q.astype(jnp.float32)
    gate2 = lax.dot_general(kmean_ref[...], q, nt, preferred_element_type=jnp.float32)
    gate = gate2[:nb] + gate2[nb:]
    blk = lax.broadcasted_iota(jnp.int32, gate.shape, 0).astype(jnp.float32)
    gate = jnp.where(blk < own.astype(jnp.float32), gate, -jnp.inf)
    picks = jnp.zeros((1, tq), jnp.int32)
    for r in range(MB_TOPK):
        mx = jnp.max(gate, axis=0, keepdims=True)
        idx = jnp.min(jnp.where(gate == mx, blk, float(nb)), axis=0, keepdims=True)
        bit = jnp.left_shift(1, idx.astype(jnp.int32))
        picks = picks | jnp.where(r < own, bit, 0)
        gate = jnp.where(blk == idx, -jnp.inf, gate)

    q2 = jnp.concatenate([(qf * (HEAD_DIM ** -0.5 * log2e)).astype(jnp.bfloat16), qaug_ref[...]], axis=1)
    kaug = kaug_ref[...]

    def scores(n):
        start = pl.multiple_of(n * MB_BLOCK, MB_BLOCK)
        k2 = jnp.concatenate([k_ref[pl.ds(start, MB_BLOCK), :], kaug], axis=1)
        return lax.dot_general(k2, q2, nt, preferred_element_type=jnp.float32)

    def values_t(n):
        return vt_ref[:, pl.ds(pl.multiple_of(n * MB_BLOCK, MB_BLOCK), MB_BLOCK)]

    def past_scores(n):
        picked = (picks & jnp.left_shift(1, n)) != 0
        far = (own - n).astype(jnp.float32) * (-slope2 * MB_BLOCK)
        return scores(n) + jnp.where(picked, far, NEG_INF)

    def fold(op, x):
        parts = [x[r:r + 8] for r in range(0, x.shape[0], 8)]
        while len(parts) > 1:
            parts = [op(parts[a], parts[a + 1]) for a in range(0, len(parts) - 1, 2)] + parts[len(parts) & ~1:]
        return parts[0]

    def stage_pair(slot, i):
        s_ref[slot, :MB_BLOCK, :] = past_scores(jnp.minimum(2 * i, nb - 1))
        s_ref[slot, MB_BLOCK:, :] = past_scores(jnp.minimum(2 * i + 1, nb - 1))

    def pair_values_t(i):
        return vt_ref[:, pl.ds(pl.multiple_of(2 * i * MB_BLOCK, 2 * MB_BLOCK), 2 * MB_BLOCK)]

    def absorb(s, v_t):
        m = m_ref[...]
        m_new = jnp.maximum(m, jnp.max(fold(jnp.maximum, s), axis=0, keepdims=True))
        p = jnp.exp2(s - m_new)
        alpha = jnp.exp2(m - m_new)
        m_ref[...] = m_new
        l_ref[...] = alpha * l_ref[...] + fold(jnp.add, p)
        acc_ref[...] = alpha * acc_ref[...] + jnp.dot(v_t, p.astype(jnp.bfloat16),
                                                      preferred_element_type=jnp.float32)

    n_pairs = (own + 1) // 2

    def two_pairs(j, carry):
        i = 2 * j
        stage_pair(1, i + 1)
        absorb(s_ref[0], pair_values_t(i))

        @pl.when(i + 1 < n_pairs)
        def _():
            stage_pair(0, i + 2)
            absorb(s_ref[1], pair_values_t(i + 1))
        return carry

    m_ref[...] = jnp.full(m_ref.shape, NEG_INF, jnp.float32)
    l_ref[...] = jnp.zeros_like(l_ref)
    acc_ref[...] = jnp.zeros_like(acc_ref)
    stage_pair(0, 0)
    lax.fori_loop(0, (n_pairs + 1) // 2, two_pairs, 0)
    key_pos = lax.broadcasted_iota(jnp.int32, (MB_BLOCK, tq), 0)
    query_pos = lax.broadcasted_iota(jnp.int32, (MB_BLOCK, tq), 1)
    absorb(jnp.where(key_pos <= query_pos, scores(own), NEG_INF), values_t(own))
    l = jnp.sum(l_ref[...], axis=0, keepdims=True)
    o_ref[...] = (acc_ref[...] / l).T.astype(o_ref.dtype)


def _moba_attention(proj, bsz, t_len, n_heads):
    tq = MB_BLOCK
    nq = t_len // tq
    nb = t_len // MB_BLOCK
    assert nb <= 32, "per-row picks are kept as one int32 bit mask"
    slopes = jnp.exp2(-8.0 * jnp.arange(1, n_heads + 1, dtype=jnp.float32) / n_heads)
    v_t = jnp.transpose(proj[:, 2 * n_heads * HEAD_DIM:])
    grid_spec = pltpu.PrefetchScalarGridSpec(
        num_scalar_prefetch=1,
        grid=(bsz, n_heads, nq),
        in_specs=[pl.BlockSpec((tq, HEAD_DIM), lambda b, h, i, s: (b * nq + i, h)),
                  pl.BlockSpec((t_len, HEAD_DIM), lambda b, h, i, s: (b, n_heads + h)),
                  pl.BlockSpec((HEAD_DIM, t_len), lambda b, h, i, s: (h, b))],
        out_specs=pl.BlockSpec((tq, HEAD_DIM), lambda b, h, i, s: (b * nq + i, h)),
        scratch_shapes=[pltpu.VMEM((2 * nb, HEAD_DIM), jnp.bfloat16),
                        pltpu.VMEM((MB_BLOCK, LANES), jnp.bfloat16),
                        pltpu.VMEM((MB_BLOCK, LANES), jnp.bfloat16),
                        pltpu.VMEM((2, 2 * MB_BLOCK, tq), jnp.float32),
                        pltpu.VMEM((1, tq), jnp.float32),
                        pltpu.VMEM((8, tq), jnp.float32),
                        pltpu.VMEM((HEAD_DIM, tq), jnp.float32)],
    )
    return pl.pallas_call(
        _moba_kernel,
        grid_spec=grid_spec,
        out_shape=jax.ShapeDtypeStruct((bsz * t_len, n_heads * HEAD_DIM), jnp.bfloat16),
        compiler_params=_cparams(("parallel", "parallel", "arbitrary")),
        name="moba_attention",
    )(slopes, proj, proj, v_t)


def _pack_bf16_pairs(x):
    half = x.shape[1] // 2
    bits = lax.bitcast_convert_type(x.astype(jnp.float32), jnp.uint32)
    return (bits[:, :half] >> 16) | bits[:, half:]


def _unpack_bf16_pairs(p):
    lo = lax.bitcast_convert_type(p << 16, jnp.float32)
    hi = lax.bitcast_convert_type(p & jnp.uint32(0xFFFF0000), jnp.float32)
    return jnp.concatenate([lo, hi], axis=1).astype(jnp.bfloat16)


def _layer_norm_rows(y, g, b):
    mu = jnp.mean(y, axis=-1, keepdims=True)
    yc = y - mu
    var = jnp.mean(yc * yc, axis=-1, keepdims=True)
    return yc * lax.rsqrt(var + LN_EPS) * g + b


def _post_mixer_kernel(h_ref, w_ref, xres_ref, g_ref, b_ref, rw_ref, rb_ref,
                       x1_ref, x1b_ref, topi_ref, topg_ref, rank_ref, count_ref, *, alpha, n_experts):
    i = pl.program_id(0)
    j = pl.program_id(1)
    tm = h_ref.shape[0]
    tn = w_ref.shape[1]
    col = pl.multiple_of(j * tn, tn)
    x1_ref[:, pl.ds(col, tn)] = jnp.dot(h_ref[...], w_ref[...], preferred_element_type=jnp.float32)

    @pl.when((i == 0) & (j == 0))
    def _():
        count_ref[...] = jnp.zeros_like(count_ref)

    @pl.when(j == pl.num_programs(1) - 1)
    def _():
        x1 = _layer_norm_rows(alpha * xres_ref[...] + x1_ref[...], g_ref[...], b_ref[...])
        x1_ref[...] = x1
        x1_hi = x1.astype(jnp.bfloat16)
        x1b_ref[...] = _pack_bf16_pairs(x1_hi)
        x1_lo = (x1 - x1_hi.astype(jnp.float32)).astype(jnp.bfloat16)
        hi_terms = jnp.dot(x1_hi, rw_ref[...], preferred_element_type=jnp.float32)
        lo_term = jnp.dot(x1_lo, rw_ref[:, :LANES], preferred_element_type=jnp.float32)
        logits = hi_terms[:, :LANES] + hi_terms[:, LANES:] + lo_term + rb_ref[...]
        lane_i = lax.broadcasted_iota(jnp.int32, logits.shape, 1)
        lane = lane_i.astype(jnp.float32)
        logits = jnp.where(lane_i < n_experts, logits, -jnp.inf)
        topi = jnp.zeros(logits.shape, jnp.float32)
        topv = jnp.full(logits.shape, -jnp.inf, jnp.float32)
        chosen = []
        for r in range(TOP_K):
            mx = jnp.max(logits, axis=-1, keepdims=True)
            idx = jnp.min(jnp.where(logits == mx, lane, float(LANES)), axis=-1, keepdims=True)
            hit = lane == idx
            chosen.append(hit)
            topi = jnp.where(lane_i == r, idx, topi)
            topv = jnp.where(lane_i == r, mx, topv)
            logits = jnp.where(hit, -jnp.inf, logits)
        e = jnp.exp(topv - jnp.max(topv, axis=-1, keepdims=True))
        topi_ref[...] = topi.astype(jnp.int32)
        topg_ref[...] = e / jnp.sum(e, axis=-1, keepdims=True)
        onehot = jnp.where(functools.reduce(jnp.logical_or, chosen), 1.0, 0.0)
        earlier = jnp.where(lax.broadcasted_iota(jnp.int32, (tm, tm), 1) < lax.broadcasted_iota(jnp.int32, (tm, tm), 0),
                            1.0, 0.0).astype(jnp.bfloat16)
        before = count_ref[...] + jnp.dot(earlier, onehot.astype(jnp.bfloat16), preferred_element_type=jnp.float32)
        rank = jnp.zeros(logits.shape, jnp.float32)
        for r in range(TOP_K):
            rank = jnp.where(lane_i == r, jnp.sum(jnp.where(chosen[r], before, 0.0), axis=-1, keepdims=True), rank)
        rank_ref[...] = rank.astype(jnp.int32)
        count_ref[...] += jnp.sum(onehot, axis=0, keepdims=True)


def _post_mixer(h, w_out, xres, ln_g, ln_b, router_w, router_b, alpha):
    n, kdim = h.shape
    d = w_out.shape[1]
    n_experts = router_w.shape[1]
    tm, tn = min(256, n), min(512, d)
    rw = jnp.pad(router_w, ((0, 0), (0, LANES - n_experts)))
    rw_hi = rw.astype(jnp.bfloat16)
    rw_lo = (rw - rw_hi.astype(jnp.float32)).astype(jnp.bfloat16)
    rw = jnp.concatenate([rw_hi, rw_lo], axis=1)
    rb = jnp.pad(router_b, (0, LANES - n_experts))[None, :]
    row = lambda i, j: (i, 0)
    fixed = lambda i, j: (0, 0)
    return pl.pallas_call(
        functools.partial(_post_mixer_kernel, alpha=alpha, n_experts=n_experts),
        grid=(n // tm, d // tn),
        in_specs=[pl.BlockSpec((tm, kdim), row),
                  pl.BlockSpec((kdim, tn), lambda i, j: (0, j)),
                  pl.BlockSpec((tm, d), row),
                  pl.BlockSpec((1, d), fixed),
                  pl.BlockSpec((1, d), fixed),
                  pl.BlockSpec((d, 2 * LANES), fixed),
                  pl.BlockSpec((1, LANES), fixed)],
        out_specs=[pl.BlockSpec((tm, d), row),
                   pl.BlockSpec((tm, d // 2), row),
                   pl.BlockSpec((tm, LANES), row),
                   pl.BlockSpec((tm, LANES), row),
                   pl.BlockSpec((tm, LANES), row),
                   pl.BlockSpec((1, LANES), fixed)],
        out_shape=[jax.ShapeDtypeStruct((n, d), jnp.float32),
                   jax.ShapeDtypeStruct((n, d // 2), jnp.uint32),
                   jax.ShapeDtypeStruct((n, LANES), jnp.int32),
                   jax.ShapeDtypeStruct((n, LANES), jnp.float32),
                   jax.ShapeDtypeStruct((n, LANES), jnp.int32),
                   jax.ShapeDtypeStruct((1, LANES), jnp.float32)],
        compiler_params=_cparams(("arbitrary", "arbitrary")),
        name="post_mixer",
    )(h, w_out, xres, ln_g[None, :], ln_b[None, :], rw, rb)


def _moe_kernel(be_ref, nu_ref, rows_ref, x_hbm, wg_ref, wu_ref, wd_ref, bg_ref, bu_ref, bd_ref, y_ref,
                hid_ref, xbuf_ref, xs_ref, sem, *, nf):
    i = pl.program_id(0)
    s = pl.program_id(1)
    n_used = nu_ref[0]
    used = i < n_used
    tm = xs_ref.shape[0]
    tf = wg_ref.shape[1]
    td = wd_ref.shape[1]
    bf16 = jnp.bfloat16

    def gather_copy(tile, r, slot):
        token = rows_ref[tile * tm + r]
        return pltpu.make_async_copy(x_hbm.at[pl.ds(token, 1)], xbuf_ref.at[slot, pl.ds(r, 1)], sem.at[slot])

    def start_gather(tile, slot):
        def issue(r, carry):
            gather_copy(tile, r, slot).start()
            return carry
        lax.fori_loop(0, tm, issue, 0, unroll=8)

    def wait_gather(slot):
        pltpu.make_async_copy(xbuf_ref.at[slot], xbuf_ref.at[slot], sem.at[slot]).wait()

    @pl.when(used & (s == 0))
    def _():
        slot = i % 2

        @pl.when(i == 0)
        def _():
            start_gather(0, 0)

        wait_gather(slot)

        @pl.when(i + 1 < n_used)
        def _():
            start_gather(i + 1, 1 - slot)

        xs_ref[...] = _unpack_bf16_pairs(xbuf_ref[slot])

    @pl.when(used & (s < nf))
    def _():
        xs = xs_ref[...]
        h_gate = jnp.dot(xs, wg_ref[...].astype(bf16), preferred_element_type=jnp.float32) + bg_ref[...]
        h_up = jnp.dot(xs, wu_ref[...].astype(bf16), preferred_element_type=jnp.float32) + bu_ref[...]
        h_gate = jnp.minimum(h_gate, SWIGLU_LIMIT)
        h_up = jnp.clip(h_up, -SWIGLU_LIMIT, SWIGLU_LIMIT)
        hid = (h_up + 1.0) * (h_gate * _sigmoid(SWIGLU_ALPHA * h_gate))
        hid_ref[:, pl.ds(pl.multiple_of(s * tf, tf), tf)] = hid.astype(bf16)

    @pl.when(used & (s >= nf))
    def _():
        col = pl.multiple_of((s - nf) * td, td)
        y = jnp.dot(hid_ref[...], wd_ref[...].astype(bf16), preferred_element_type=jnp.float32) + bd_ref[...]
        y_ref[:, pl.ds(col, td)] = y.astype(y_ref.dtype)

    @pl.when(jnp.logical_not(used) & (s == pl.num_programs(1) - 1))
    def _():
        y_ref[...] = jnp.zeros_like(y_ref)


def _moe_experts(x_packed, rows, block_e, n_used, layer, w_gate, w_up, w_down, b_gate, b_up, b_down, tm):
    n_slots = rows.shape[0]
    _, n_experts, d, f = w_gate.shape
    tf, td = min(256, f), min(2048, d)
    nf, nd = f // tf, d // td
    n_blocks = n_slots // tm

    def f_tile(i, s, nu):
        return jnp.where(i < nu[0], jnp.minimum(s, nf - 1), nf - 1)

    def d_tile(i, s, nu):
        return jnp.where(i < nu[0], jnp.clip(s - nf, 0, nd - 1), nd - 1)

    grid_spec = pltpu.PrefetchScalarGridSpec(
        num_scalar_prefetch=3,
        grid=(n_blocks, nf + nd),
        in_specs=[pl.BlockSpec(memory_space=pl.ANY),
                  pl.BlockSpec((None, None, d, tf), lambda i, s, be, nu, rw: (layer, be[i], 0, f_tile(i, s, nu))),
                  pl.BlockSpec((None, None, d, tf), lambda i, s, be, nu, rw: (layer, be[i], 0, f_tile(i, s, nu))),
                  pl.BlockSpec((None, None, f, td), lambda i, s, be, nu, rw: (layer, be[i], 0, d_tile(i, s, nu))),
                  pl.BlockSpec((None, None, 1, tf), lambda i, s, be, nu, rw: (layer, be[i], 0, f_tile(i, s, nu))),
                  pl.BlockSpec((None, None, 1, tf), lambda i, s, be, nu, rw: (layer, be[i], 0, f_tile(i, s, nu))),
                  pl.BlockSpec((None, None, 1, td), lambda i, s, be, nu, rw: (layer, be[i], 0, d_tile(i, s, nu)))],
        out_specs=pl.BlockSpec((tm, d), lambda i, s, be, nu, rw: (i, 0)),
        scratch_shapes=[pltpu.VMEM((tm, f), jnp.bfloat16),
                        pltpu.VMEM((2, tm, d // 2), jnp.uint32),
                        pltpu.VMEM((tm, d), jnp.bfloat16),
                        pltpu.SemaphoreType.DMA((2,))],
    )
    return pl.pallas_call(
        functools.partial(_moe_kernel, nf=nf),
        grid_spec=grid_spec,
        out_shape=jax.ShapeDtypeStruct((n_slots, d), jnp.bfloat16),
        compiler_params=_cparams(("arbitrary", "arbitrary")),
        name="moe_experts",
    )(block_e, n_used, rows, x_packed, w_gate, w_up, w_down,
      b_gate[:, :, None, :], b_up[:, :, None, :], b_down[:, :, None, :])


def _combine_kernel(yg_ref, gate_ref, x1_ref, g_ref, b_ref, o_ref, ob_ref, *, alpha):
    gates = gate_ref[...]
    f = yg_ref[0].astype(jnp.float32) * gates[:, 0:1]
    for r in range(1, TOP_K):
        f = f + yg_ref[r].astype(jnp.float32) * gates[:, r:r + 1]
    out = _layer_norm_rows(alpha * x1_ref[...] + f, g_ref[...], b_ref[...])
    o_ref[...] = out
    ob_ref[...] = out.astype(ob_ref.dtype)


def _moe_combine(yg, gates, x1, ln_g, ln_b, alpha):
    n, d = x1.shape
    tm = min(256, n)
    return pl.pallas_call(
        functools.partial(_combine_kernel, alpha=alpha),
        grid=(n // tm,),
        in_specs=[pl.BlockSpec((TOP_K, tm, d), lambda i: (0, i, 0)),
                  pl.BlockSpec((tm, LANES), lambda i: (i, 0)),
                  pl.BlockSpec((tm, d), lambda i: (i, 0)),
                  pl.BlockSpec((1, d), lambda i: (0, 0)),
                  pl.BlockSpec((1, d), lambda i: (0, 0))],
        out_specs=[pl.BlockSpec((tm, d), lambda i: (i, 0)),
                   pl.BlockSpec((tm, d), lambda i: (i, 0))],
        out_shape=[jax.ShapeDtypeStruct((n, d), jnp.float32),
                   jax.ShapeDtypeStruct((n, d), jnp.bfloat16)],
        compiler_params=_cparams(("parallel",)),
        name="moe_combine",
    )(yg, gates, x1, ln_g[None, :], ln_b[None, :])


def _moe_layer(x1, x1b, topi, topg, rank, counts, ln_g, ln_b, layer, w_gate, b_gate, w_up, b_up, w_down, b_down,
               alpha):
    n_tok, d = x1.shape
    n_experts = w_gate.shape[1]
    n_assign = n_tok * TOP_K
    tm = min(512, n_assign // n_experts)
    flat_e = topi[:, :TOP_K].reshape(-1)
    rank = rank[:, :TOP_K].reshape(-1)
    counts = counts[0, :n_experts].astype(jnp.int32)
    padded = (counts + tm - 1) // tm * tm
    pends = jnp.cumsum(padded)
    dest = (pends - padded)[flat_e] + rank
    n_blocks = n_assign // tm + n_experts
    flat_tok = jnp.arange(n_assign, dtype=jnp.int32) // TOP_K
    rows = jnp.zeros((n_blocks * tm,), jnp.int32).at[dest].set(flat_tok, mode="promise_in_bounds",
                                                               unique_indices=True)
    block_e = jnp.minimum(jnp.searchsorted(pends, jnp.arange(n_blocks, dtype=jnp.int32) * tm, side='right'),
                          n_experts - 1).astype(jnp.int32)
    n_used = (pends[-1:] // tm).astype(jnp.int32)
    y = _moe_experts(x1b, rows, block_e, n_used, layer, w_gate, w_up, w_down, b_gate, b_up, b_down, tm)
    yg = y.at[dest.reshape(n_tok, TOP_K).T].get(mode="promise_in_bounds")
    return _moe_combine(yg, topg, x1, ln_g, ln_b, alpha)


def kernel(x, dn_w_in, dn_conv_w, dn_a_log, dn_dt_bias, dn_norm_w, dn_w_out, mb_w_in, mb_w_out, ln_g, ln_b,
           router_w, router_b, w_gate, b_gate, w_up, b_up, w_down, b_down):
    bsz, t_len, d = x.shape
    depth = ln_g.shape[0]
    alpha = float((2 * depth) ** 0.25)
    n = bsz * t_len
    xf = x.reshape(n, d)
    xb = xf.astype(jnp.bfloat16)
    for i in range(depth):
        j = i // 2
        if i % 2 == 0:
            n_vheads = d // HEAD_DIM
            main = dn_w_in.shape[2] - 2 * n_vheads
            proj = _matmul(xb, dn_w_in[j].astype(jnp.bfloat16), main, jnp.bfloat16, 1024, 512)
            gates = _dn_gates(xf, dn_w_in[j][:, main:], dn_a_log[j], dn_dt_bias[j], n_vheads)
            gates_t = jnp.transpose(gates.reshape(bsz, t_len, LANES), (0, 2, 1))
            h = _dn_delta_rule(proj, gates, gates_t, dn_conv_w[j], dn_norm_w[j], bsz, t_len, d)
            w_out = dn_w_out[j]
        else:
            proj = _matmul(xb, mb_w_in[j].astype(jnp.bfloat16), mb_w_in.shape[2], jnp.bfloat16, 1024, 512)
            h = _moba_attention(proj, bsz, t_len, d // HEAD_DIM)
            w_out = mb_w_out[j]
        x1, x1b, topi, topg, rank, counts = _post_mixer(h, w_out.astype(jnp.bfloat16), xf, ln_g[i, 0], ln_b[i, 0],
                                                        router_w[i], router_b[i], alpha)
        xf, xb = _moe_layer(x1, x1b, topi, topg, rank, counts, ln_g[i, 1], ln_b[i, 1], i, w_gate, b_gate,
                            w_up, b_up, w_down, b_down, alpha)
    return xf.reshape(bsz, t_len, d)
```

```python
import functools

import jax
import jax.numpy as jnp
import numpy as np
from jax import lax
from jax.experimental import pallas as pl
from jax.experimental.pallas import tpu as pltpu

HEAD_DIM = 128
DN_CONV = 4
DN_CHUNK = 256
DN_GROUP = 2
MOE_GROUP = 2
MB_BLOCK = 256
MB_TOPK = 3
TOP_K = 4
SWIGLU_LIMIT = 7.0
SWIGLU_ALPHA = 1.702
LN_EPS = 1e-5
NORM_EPS = 1e-6
NEG_INF = -1e30
LANES = 128
HALO = 16
VMEM_LIMIT = 56 * 1024 * 1024

_HI = lax.Precision.HIGHEST


def _cparams(sem):
    return pltpu.CompilerParams(dimension_semantics=sem, vmem_limit_bytes=VMEM_LIMIT)


def _sigmoid(v):
    return 1.0 / (1.0 + jnp.exp(-v))


def _silu(v):
    return v * _sigmoid(v)


def _mm_kernel(x_ref, w_ref, o_ref):
    o_ref[...] = jnp.dot(x_ref[...], w_ref[...], preferred_element_type=jnp.float32).astype(o_ref.dtype)


def _matmul(x, w, n, out_dtype, tm, tn):
    m, k = x.shape
    tm, tn = min(tm, m), min(tn, n)
    assert n % tn == 0 and m % tm == 0
    return pl.pallas_call(
        _mm_kernel,
        grid=(m // tm, n // tn),
        in_specs=[pl.BlockSpec((tm, k), lambda i, j: (i, 0)),
                  pl.BlockSpec((k, tn), lambda i, j: (0, j))],
        out_specs=pl.BlockSpec((tm, tn), lambda i, j: (i, j)),
        out_shape=jax.ShapeDtypeStruct((m, n), out_dtype),
        compiler_params=_cparams(("parallel", "arbitrary")),
        name="dense_matmul",
    )(x, w)


def _dn_gates_kernel(x_ref, w_ref, alog_ref, dtb_ref, o_ref, *, n_heads):
    tm = x_ref.shape[0]
    ba = jnp.dot(x_ref[...], w_ref[...], precision=_HI, preferred_element_type=jnp.float32)
    lane = lax.broadcasted_iota(jnp.int32, ba.shape, 1)
    beta = _sigmoid(ba)
    v = ba + dtb_ref[...]
    softplus = jnp.maximum(v, 0.0) + jnp.log(1.0 + jnp.exp(-jnp.abs(v)))
    g = -jnp.exp(alog_ref[...]) * softplus
    g = jnp.where((lane >= n_heads) & (lane < 2 * n_heads), g, 0.0)
    ri = lax.broadcasted_iota(jnp.int32, (tm, tm), 0)
    ci = lax.broadcasted_iota(jnp.int32, (tm, tm), 1)
    shift = DN_CHUNK.bit_length() - 1
    tril = jnp.where((ci <= ri) & ((ri >> shift) == (ci >> shift)), 1.0, 0.0)
    gc = jnp.dot(tril, g, precision=_HI, preferred_element_type=jnp.float32)
    o_ref[...] = jnp.where(lane < n_heads, beta, gc)


def _dn_gates(x2d, w_ba, a_log, dt_bias, n_heads):
    n, d = x2d.shape
    tm = min(256, n)
    pad = LANES - 2 * n_heads
    w_p = jnp.pad(w_ba, ((0, 0), (0, pad)))
    alog_p = jnp.pad(a_log, (n_heads, pad))[None, :]
    dtb_p = jnp.pad(dt_bias, (n_heads, pad))[None, :]
    return pl.pallas_call(
        functools.partial(_dn_gates_kernel, n_heads=n_heads),
        grid=(n // tm,),
        in_specs=[pl.BlockSpec((tm, d), lambda i: (i, 0)),
                  pl.BlockSpec((d, LANES), lambda i: (0, 0)),
                  pl.BlockSpec((1, LANES), lambda i: (0, 0)),
                  pl.BlockSpec((1, LANES), lambda i: (0, 0))],
        out_specs=pl.BlockSpec((tm, LANES), lambda i: (i, 0)),
        out_shape=jax.ShapeDtypeStruct((n, LANES), jnp.float32),
        compiler_params=_cparams(("parallel",)),
        name="dn_gates",
    )(x2d, w_p, alog_p, dtb_p)


def _causal_conv_silu(x_ref, halo_ref, w_ref, first):
    tb = x_ref.shape[0]
    halo = jnp.where(first, 0.0, halo_ref[...].astype(jnp.float32))
    xcat = jnp.concatenate([halo, x_ref[...].astype(jnp.float32)], axis=0)
    w = w_ref[...]
    acc = xcat[HALO:] * w[DN_CONV - 1:DN_CONV, :]
    for s in range(1, DN_CONV):
        acc = acc + pltpu.roll(xcat, s, 0)[HALO:] * w[DN_CONV - 1 - s:DN_CONV - s, :]
    del tb
    return _silu(acc)


def _l2norm(v):
    return v * lax.rsqrt(jnp.sum(v * v, axis=-1, keepdims=True) + NORM_EPS)


def _inverse_level_masks(c):
    ri = np.arange(c)[:, None]
    ci = np.arange(c)[None, :]
    masks = [((ri >> 1) == (ci >> 1)) & (ri > ci)]
    s = 2
    while s < c:
        masks.append(((ri // (2 * s)) == (ci // (2 * s))) & ((ri & s) != 0) & ((ci & s) == 0))
        s *= 2
    return jnp.asarray(np.stack(masks), jnp.bfloat16)


def _dn_kernel(q_ref, k_ref, v_ref, z_ref, qh_ref, kh_ref, vh_ref, g_ref, gt_ref,
               wq_ref, wk_ref, wv_ref, nw_ref, lm_ref, negc_ref, o_ref, state_ref, *, n_vheads):
    hg = pl.program_id(1)
    t = pl.program_id(2)
    first = t == 0
    tb = q_ref.shape[0]
    group = q_ref.shape[1] // HEAD_DIM
    rep = v_ref.shape[1] // q_ref.shape[1]
    bf16, f32 = jnp.bfloat16, jnp.float32

    @pl.when(first)
    def _():
        state_ref[...] = jnp.zeros_like(state_ref)

    def head(x, c):
        return x[:, c * HEAD_DIM:(c + 1) * HEAD_DIM]

    q_all = _causal_conv_silu(q_ref, qh_ref, wq_ref, first)
    k_all = _causal_conv_silu(k_ref, kh_ref, wk_ref, first)
    v = _causal_conv_silu(v_ref, vh_ref, wv_ref, first)
    q = [_l2norm(head(q_all, g)) * (HEAD_DIM ** -0.5) for g in range(group)]
    k = [_l2norm(head(k_all, g)) for g in range(group)]
    gates = g_ref[...]
    lane = lax.broadcasted_iota(jnp.int32, gates.shape, 1)
    eye = jnp.where(lax.broadcasted_iota(jnp.int32, (tb, tb), 0) == lax.broadcasted_iota(jnp.int32, (tb, tb), 1),
                    1.0, 0.0)
    nt = (((1,), (1,)), ((), ()))
    k_b = [k[g].astype(bf16) for g in range(group)]
    kk = [lax.dot_general(k_b[g], k_b[g], nt, preferred_element_type=f32) for g in range(group)]
    qk = [lax.dot_general(q[g].astype(bf16), k_b[g], nt, preferred_element_type=f32) for g in range(group)]

    heads = range(group * rep)
    beta, gcol, g_last, decay, a_b, inv = [], [], [], [], [], []
    for c in heads:
        hv = hg * group * rep + c
        beta.append(jnp.sum(jnp.where(lane == hv, gates, 0.0), axis=-1, keepdims=True))
        gcol.append(jnp.sum(jnp.where(lane == n_vheads + hv, gates, 0.0), axis=-1, keepdims=True))
        grow = gt_ref[pl.ds(n_vheads + hv, 1), :]
        g_last.append(grow[:, tb - 1:tb])
        decay.append(jnp.exp(gcol[c] - grow + negc_ref[...]))
        a_b.append((kk[c // rep] * beta[c] * decay[c]).astype(bf16))
        inv.append(eye - (a_b[c] * lm_ref[0]).astype(f32))
    for lvl in range(1, lm_ref.shape[0]):
        tmp = [jnp.dot(a_b[c] * lm_ref[lvl], inv[c].astype(bf16), preferred_element_type=f32) for c in heads]
        inv = [inv[c] - jnp.dot(inv[c].astype(bf16), tmp[c].astype(bf16), preferred_element_type=f32)
               for c in heads]
    eg = [jnp.exp(gcol[c]) for c in heads]
    uw = [jnp.dot(inv[c].astype(bf16),
                  jnp.concatenate([head(v, c) * beta[c], k[c // rep] * (beta[c] * eg[c])], axis=1).astype(bf16),
                  preferred_element_type=f32) for c in heads]
    state = [state_ref[c] for c in heads]
    ws = [jnp.dot(jnp.concatenate([uw[c][:, HEAD_DIM:], q[c // rep] * eg[c]], axis=0).astype(bf16),
                  state[c].astype(bf16), preferred_element_type=f32) for c in heads]
    v_new_b = [(uw[c][:, :HEAD_DIM] - ws[c][:tb]).astype(bf16) for c in heads]
    outs = []
    for c in heads:
        o = ws[c][tb:] + jnp.dot((qk[c // rep] * decay[c]).astype(bf16), v_new_b[c], preferred_element_type=f32)
        k_dec_t = (k[c // rep] * jnp.exp(g_last[c] - gcol[c])).T.astype(bf16)
        state_ref[c] = state[c] * jnp.exp(g_last[c]) + jnp.dot(k_dec_t, v_new_b[c], preferred_element_type=f32)
        zc = head(z_ref[...], c).astype(f32)
        outs.append(o * lax.rsqrt(jnp.mean(o * o, axis=-1, keepdims=True) + NORM_EPS) * nw_ref[...] * _silu(zc))
    o_ref[...] = jnp.concatenate(outs, axis=1).astype(o_ref.dtype)


def _dn_delta_rule(proj, gates, gates_t, conv_w, norm_w, bsz, t_len, d_model):
    n_kheads = d_model // (2 * HEAD_DIM)
    n_vheads = d_model // HEAD_DIM
    group = min(DN_GROUP, n_kheads)
    kw = group * HEAD_DIM
    vw = kw * (n_vheads // n_kheads)
    tb = DN_CHUNK
    nt = t_len // tb
    hb = tb // HALO
    level_masks = _inverse_level_masks(tb)
    ri = np.arange(tb)
    neg_causal = jnp.asarray(np.where(ri[None, :] <= ri[:, None], 0.0, NEG_INF), jnp.float32)
    k_off = n_kheads * HEAD_DIM // kw
    v_off = 2 * n_kheads * HEAD_DIM // vw
    z_off = v_off + n_vheads * HEAD_DIM // vw

    def row(b, h, t):
        return b * nt + t

    def halo_row(b, h, t):
        return jnp.maximum((b * nt + t) * hb - 1, 0)

    in_specs = [
        pl.BlockSpec((tb, kw), lambda b, h, t: (row(b, h, t), h)),
        pl.BlockSpec((tb, kw), lambda b, h, t: (row(b, h, t), k_off + h)),
        pl.BlockSpec((tb, vw), lambda b, h, t: (row(b, h, t), v_off + h)),
        pl.BlockSpec((tb, vw), lambda b, h, t: (row(b, h, t), z_off + h)),
        pl.BlockSpec((HALO, kw), lambda b, h, t: (halo_row(b, h, t), h)),
        pl.BlockSpec((HALO, kw), lambda b, h, t: (halo_row(b, h, t), k_off + h)),
        pl.BlockSpec((HALO, vw), lambda b, h, t: (halo_row(b, h, t), v_off + h)),
        pl.BlockSpec((tb, LANES), lambda b, h, t: (row(b, h, t), 0)),
        pl.BlockSpec((None, LANES, tb), lambda b, h, t: (b, 0, t)),
        pl.BlockSpec((DN_CONV, kw), lambda b, h, t: (0, h)),
        pl.BlockSpec((DN_CONV, kw), lambda b, h, t: (0, k_off + h)),
        pl.BlockSpec((DN_CONV, vw), lambda b, h, t: (0, v_off + h)),
        pl.BlockSpec((1, HEAD_DIM), lambda b, h, t: (0, 0)),
        pl.BlockSpec(level_masks.shape, lambda b, h, t: (0, 0, 0)),
        pl.BlockSpec((tb, tb), lambda b, h, t: (0, 0)),
    ]
    return pl.pallas_call(
        functools.partial(_dn_kernel, n_vheads=n_vheads),
        grid=(bsz, n_kheads // group, nt),
        in_specs=in_specs,
        out_specs=pl.BlockSpec((tb, vw), lambda b, h, t: (row(b, h, t), h)),
        out_shape=jax.ShapeDtypeStruct((bsz * t_len, n_vheads * HEAD_DIM), jnp.bfloat16),
        scratch_shapes=[pltpu.VMEM((vw // HEAD_DIM, HEAD_DIM, HEAD_DIM), jnp.float32)],
        compiler_params=_cparams(("parallel", "parallel", "arbitrary")),
        name="dn_delta_rule",
    )(proj, proj, proj, proj, proj, proj, proj, gates, gates_t, conv_w, conv_w, conv_w, norm_w[None, :],
      level_masks, neg_causal)


def _split3(a):
    hi = a.astype(jnp.bfloat16).astype(jnp.float32)
    mid = (a - hi).astype(jnp.bfloat16).astype(jnp.float32)
    lo = (a - hi - mid).astype(jnp.bfloat16).astype(jnp.float32)
    return hi, mid, lo


def _moba_kernel(slope_ref, q_ref, k_ref, vt_ref, o_ref, kmean_ref, qaug_ref, kaug_ref, s_ref, m_ref, l_ref, acc_ref):
    h = pl.program_id(1)
    own = pl.program_id(2)
    tq = q_ref.shape[0]
    t_len = k_ref.shape[0]
    nb = t_len // MB_BLOCK
    nt = (((1,), (1,)), ((), ()))
    log2e = 1.4426950408889634
    slope2 = slope_ref[h] * log2e

    @pl.when(own == 0)
    def _():
        kf = k_ref[...].astype(jnp.float32).reshape(nb, MB_BLOCK, HEAD_DIM)
        kmean = jnp.sum(kf, axis=1) * (1.0 / MB_BLOCK)
        kmean_hi = kmean.astype(jnp.bfloat16)
        kmean_lo = (kmean - kmean_hi.astype(jnp.float32)).astype(jnp.bfloat16)
        kmean_ref[...] = jnp.concatenate([kmean_hi, kmean_lo], axis=0)
        pos = lax.broadcasted_iota(jnp.int32, (MB_BLOCK, LANES), 0).astype(jnp.float32)
        col = lax.broadcasted_iota(jnp.int32, (MB_BLOCK, LANES), 1)
        q_hi, q_mid, q_lo = _split3(pos * (-slope2))
        k_hi, k_mid, k_lo = _split3(pos * slope2)
        one = jnp.where(col < 6, 1.0, 0.0)
        qaug = jnp.where(col == 0, q_hi, jnp.where(col == 1, q_mid, jnp.where(col == 2, q_lo, one)))
        kaug = jnp.where(col == 3, k_hi, jnp.where(col == 4, k_mid, jnp.where(col == 5, k_lo, one)))
        qaug_ref[...] = qaug.astype(qaug_ref.dtype)
        kaug_ref[...] = kaug.astype(kaug_ref.dtype)

    q = q_ref[...]
    qf = q.astype(jnp.float32)
    gate2 = lax.dot_general(kmean_ref[...], q, nt, preferred_element_type=jnp.float32)
    gate = gate2[:nb] + gate2[nb:]
    blk = lax.broadcasted_iota(jnp.int32, gate.shape, 0).astype(jnp.float32)
    gate = jnp.where(blk < own.astype(jnp.float32), gate, -jnp.inf)
    picks = jnp.zeros((1, tq), jnp.int32)
    for r in range(MB_TOPK):
        mx = jnp.max(gate, axis=0, keepdims=True)
        idx = jnp.min(jnp.where(gate == mx, blk, float(nb)), axis=0, keepdims=True)
        bit = jnp.left_shift(1, idx.astype(jnp.int32))
        picks = picks | jnp.where(r < own, bit, 0)
        gate = jnp.where(blk == idx, -jnp.inf, gate)

    q2 = jnp.concatenate([(qf * (HEAD_DIM ** -0.5 * log2e)).astype(jnp.bfloat16), qaug_ref[...]], axis=1)
    kaug = kaug_ref[...]

    def scores(n):
        start = pl.multiple_of(n * MB_BLOCK, MB_BLOCK)
        k2 = jnp.concatenate([k_ref[pl.ds(start, MB_BLOCK), :], kaug], axis=1)
        return lax.dot_general(k2, q2, nt, preferred_element_type=jnp.float32)

    def values_t(n):
        return vt_ref[:, pl.ds(pl.multiple_of(n * MB_BLOCK, MB_BLOCK), MB_BLOCK)]

    def past_scores(n):
        picked = (picks & jnp.left_shift(1, n)) != 0
        far = (own - n).astype(jnp.float32) * (-slope2 * MB_BLOCK)
        return scores(n) + jnp.where(picked, far, NEG_INF)

    def fold(op, x):
        parts = [x[r:r + 8] for r in range(0, x.shape[0], 8)]
        while len(parts) > 1:
            parts = [op(parts[a], parts[a + 1]) for a in range(0, len(parts) - 1, 2)] + parts[len(parts) & ~1:]
        return parts[0]

    def stage_pair(slot, i):
        s_ref[slot, :MB_BLOCK, :] = past_scores(jnp.minimum(2 * i, nb - 1))
        s_ref[slot, MB_BLOCK:, :] = past_scores(jnp.minimum(2 * i + 1, nb - 1))

    def pair_values_t(i):
        return vt_ref[:, pl.ds(pl.multiple_of(2 * i * MB_BLOCK, 2 * MB_BLOCK), 2 * MB_BLOCK)]

    def absorb(s, v_t):
        m = m_ref[...]
        m_new = jnp.maximum(m, jnp.max(fold(jnp.maximum, s), axis=0, keepdims=True))
        p = jnp.exp2(s - m_new)
        alpha = jnp.exp2(m - m_new)
        m_ref[...] = m_new
        l_ref[...] = alpha * l_ref[...] + fold(jnp.add, p)
        acc_ref[...] = alpha * acc_ref[...] + jnp.dot(v_t, p.astype(jnp.bfloat16),
                                                      preferred_element_type=jnp.float32)

    n_pairs = (own + 1) // 2

    def two_pairs(j, carry):
        i = 2 * j
        stage_pair(1, i + 1)
        absorb(s_ref[0], pair_values_t(i))

        @pl.when(i + 1 < n_pairs)
        def _():
            stage_pair(0, i + 2)
            absorb(s_ref[1], pair_values_t(i + 1))
        return carry

    m_ref[...] = jnp.full(m_ref.shape, NEG_INF, jnp.float32)
    l_ref[...] = jnp.zeros_like(l_ref)
    acc_ref[...] = jnp.zeros_like(acc_ref)
    stage_pair(0, 0)
    lax.fori_loop(0, (n_pairs + 1) // 2, two_pairs, 0)
    key_pos = lax.broadcasted_iota(jnp.int32, (MB_BLOCK, tq), 0)
    query_pos = lax.broadcasted_iota(jnp.int32, (MB_BLOCK, tq), 1)
    absorb(jnp.where(key_pos <= query_pos, scores(own), NEG_INF), values_t(own))
    l = jnp.sum(l_ref[...], axis=0, keepdims=True)
    o_ref[...] = (acc_ref[...] / l).T.astype(o_ref.dtype)


def _moba_attention(proj, bsz, t_len, n_heads):
    tq = MB_BLOCK
    nq = t_len // tq
    nb = t_len // MB_BLOCK
    assert nb <= 32, "per-row picks are kept as one int32 bit mask"
    slopes = jnp.exp2(-8.0 * jnp.arange(1, n_heads + 1, dtype=jnp.float32) / n_heads)
    v_t = jnp.transpose(proj[:, 2 * n_heads * HEAD_DIM:])
    grid_spec = pltpu.PrefetchScalarGridSpec(
        num_scalar_prefetch=1,
        grid=(bsz, n_heads, nq),
        in_specs=[pl.BlockSpec((tq, HEAD_DIM), lambda b, h, i, s: (b * nq + i, h)),
                  pl.BlockSpec((t_len, HEAD_DIM), lambda b, h, i, s: (b, n_heads + h)),
                  pl.BlockSpec((HEAD_DIM, t_len), lambda b, h, i, s: (h, b))],
        out_specs=pl.BlockSpec((tq, HEAD_DIM), lambda b, h, i, s: (b * nq + i, h)),
        scratch_shapes=[pltpu.VMEM((2 * nb, HEAD_DIM), jnp.bfloat16),
                        pltpu.VMEM((MB_BLOCK, LANES), jnp.bfloat16),
                        pltpu.VMEM((MB_BLOCK, LANES), jnp.bfloat16),
                        pltpu.VMEM((2, 2 * MB_BLOCK, tq), jnp.float32),
                        pltpu.VMEM((1, tq), jnp.float32),
                        pltpu.VMEM((8, tq), jnp.float32),
                        pltpu.VMEM((HEAD_DIM, tq), jnp.float32)],
    )
    return pl.pallas_call(
        _moba_kernel,
        grid_spec=grid_spec,
        out_shape=jax.ShapeDtypeStruct((bsz * t_len, n_heads * HEAD_DIM), jnp.bfloat16),
        compiler_params=_cparams(("parallel", "parallel", "arbitrary")),
        name="moba_attention",
    )(slopes, proj, proj, v_t)


def _pack_bf16_pairs(x):
    half = x.shape[1] // 2
    bits = lax.bitcast_convert_type(x.astype(jnp.float32), jnp.uint32)
    return (bits[:, :half] >> 16) | bits[:, half:]


def _unpack_bf16_pairs(p):
    lo = lax.bitcast_convert_type(p << 16, jnp.float32)
    hi = lax.bitcast_convert_type(p & jnp.uint32(0xFFFF0000), jnp.float32)
    return jnp.concatenate([lo, hi], axis=1).astype(jnp.bfloat16)


def _layer_norm_rows(y, g, b):
    mu = jnp.mean(y, axis=-1, keepdims=True)
    yc = y - mu
    var = jnp.mean(yc * yc, axis=-1, keepdims=True)
    return yc * lax.rsqrt(var + LN_EPS) * g + b


def _post_mixer_kernel(h_ref, w_ref, xres_ref, g_ref, b_ref, rw_ref, rb_ref,
                       x1_ref, x1b_ref, topi_ref, topg_ref, rank_ref, count_ref, *, alpha, n_experts):
    i = pl.program_id(0)
    j = pl.program_id(1)
    tm = h_ref.shape[0]
    tn = w_ref.shape[1]
    col = pl.multiple_of(j * tn, tn)
    x1_ref[:, pl.ds(col, tn)] = jnp.dot(h_ref[...], w_ref[...], preferred_element_type=jnp.float32)

    @pl.when((i == 0) & (j == 0))
    def _():
        count_ref[...] = jnp.zeros_like(count_ref)

    @pl.when(j == pl.num_programs(1) - 1)
    def _():
        x1 = _layer_norm_rows(alpha * xres_ref[...] + x1_ref[...], g_ref[...], b_ref[...])
        x1_ref[...] = x1
        x1_hi = x1.astype(jnp.bfloat16)
        x1b_ref[...] = _pack_bf16_pairs(x1_hi)
        x1_lo = (x1 - x1_hi.astype(jnp.float32)).astype(jnp.bfloat16)
        hi_terms = jnp.dot(x1_hi, rw_ref[...], preferred_element_type=jnp.float32)
        lo_term = jnp.dot(x1_lo, rw_ref[:, :LANES], preferred_element_type=jnp.float32)
        logits = hi_terms[:, :LANES] + hi_terms[:, LANES:] + lo_term + rb_ref[...]
        lane_i = lax.broadcasted_iota(jnp.int32, logits.shape, 1)
        lane = lane_i.astype(jnp.float32)
        logits = jnp.where(lane_i < n_experts, logits, -jnp.inf)
        topi = jnp.zeros(logits.shape, jnp.float32)
        topv = jnp.full(logits.shape, -jnp.inf, jnp.float32)
        chosen = []
        for r in range(TOP_K):
            mx = jnp.max(logits, axis=-1, keepdims=True)
            idx = jnp.min(jnp.where(logits == mx, lane, float(LANES)), axis=-1, keepdims=True)
            hit = lane == idx
            chosen.append(hit)
            topi = jnp.where(lane_i == r, idx, topi)
            topv = jnp.where(lane_i == r, mx, topv)
            logits = jnp.where(hit, -jnp.inf, logits)
        e = jnp.exp(topv - jnp.max(topv, axis=-1, keepdims=True))
        topi_ref[...] = topi.astype(jnp.int32)
        topg_ref[...] = e / jnp.sum(e, axis=-1, keepdims=True)
        onehot = jnp.where(functools.reduce(jnp.logical_or, chosen), 1.0, 0.0)
        earlier = jnp.where(lax.broadcasted_iota(jnp.int32, (tm, tm), 1) < lax.broadcasted_iota(jnp.int32, (tm, tm), 0),
                            1.0, 0.0).astype(jnp.bfloat16)
        before = count_ref[...] + jnp.dot(earlier, onehot.astype(jnp.bfloat16), preferred_element_type=jnp.float32)
        rank = jnp.zeros(logits.shape, jnp.float32)
        for r in range(TOP_K):
            rank = jnp.where(lane_i == r, jnp.sum(jnp.where(chosen[r], before, 0.0), axis=-1, keepdims=True), rank)
        rank_ref[...] = rank.astype(jnp.int32)
        count_ref[...] += jnp.sum(onehot, axis=0, keepdims=True)


def _post_mixer(h, w_out, xres, ln_g, ln_b, router_w, router_b, alpha):
    n, kdim = h.shape
    d = w_out.shape[1]
    n_experts = router_w.shape[1]
    tm, tn = min(256, n), min(512, d)
    rw = jnp.pad(router_w, ((0, 0), (0, LANES - n_experts)))
    rw_hi = rw.astype(jnp.bfloat16)
    rw_lo = (rw - rw_hi.astype(jnp.float32)).astype(jnp.bfloat16)
    rw = jnp.concatenate([rw_hi, rw_lo], axis=1)
    rb = jnp.pad(router_b, (0, LANES - n_experts))[None, :]
    row = lambda i, j: (i, 0)
    fixed = lambda i, j: (0, 0)
    return pl.pallas_call(
        functools.partial(_post_mixer_kernel, alpha=alpha, n_experts=n_experts),
        grid=(n // tm, d // tn),
        in_specs=[pl.BlockSpec((tm, kdim), row),
                  pl.BlockSpec((kdim, tn), lambda i, j: (0, j)),
                  pl.BlockSpec((tm, d), row),
                  pl.BlockSpec((1, d), fixed),
                  pl.BlockSpec((1, d), fixed),
                  pl.BlockSpec((d, 2 * LANES), fixed),
                  pl.BlockSpec((1, LANES), fixed)],
        out_specs=[pl.BlockSpec((tm, d), row),
                   pl.BlockSpec((tm, d // 2), row),
                   pl.BlockSpec((tm, LANES), row),
                   pl.BlockSpec((tm, LANES), row),
                   pl.BlockSpec((tm, LANES), row),
                   pl.BlockSpec((1, LANES), fixed)],
        out_shape=[jax.ShapeDtypeStruct((n, d), jnp.float32),
                   jax.ShapeDtypeStruct((n, d // 2), jnp.uint32),
                   jax.ShapeDtypeStruct((n, LANES), jnp.int32),
                   jax.ShapeDtypeStruct((n, LANES), jnp.float32),
                   jax.ShapeDtypeStruct((n, LANES), jnp.int32),
                   jax.ShapeDtypeStruct((1, LANES), jnp.float32)],
        compiler_params=_cparams(("arbitrary", "arbitrary")),
        name="post_mixer",
    )(h, w_out, xres, ln_g[None, :], ln_b[None, :], rw, rb)


def _moe_kernel(ge_ref, meta_ref, odd_ref, next_odd_ref, rows_ref, x_hbm, wg_ref, wu_ref, wd_ref, bg_ref, bu_ref,
                bd_ref, y_ref, hid_ref, xbuf_ref, xs_ref, sem, *, nf):
    g = pl.program_id(0)
    s = pl.program_id(1)
    t = pl.program_id(2)
    n_groups = meta_ref[0]
    first_odd = meta_ref[1]
    valid = (g < n_groups) & ((t == 0) | (odd_ref[g] != 0))
    tm = xs_ref.shape[1]
    tf = wg_ref.shape[1]
    bf16 = jnp.bfloat16

    def gather_copy(tile, r, slot):
        token = rows_ref[tile * tm + r]
        return pltpu.make_async_copy(x_hbm.at[pl.ds(token, 1)], xbuf_ref.at[slot, pl.ds(r, 1)], sem.at[slot])

    def start_gather(tile, slot):
        def issue(r, carry):
            gather_copy(tile, r, slot).start()
            return carry
        lax.fori_loop(0, tm, issue, 0, unroll=8)

    def wait_gather(slot):
        pltpu.make_async_copy(xbuf_ref.at[slot], xbuf_ref.at[slot], sem.at[slot]).wait()

    @pl.when(valid & (s == 0))
    def _():
        @pl.when((g == 0) & (t == 0))
        def _():
            start_gather(0, 0)

            @pl.when(first_odd >= 0)
            def _():
                start_gather(2 * first_odd + 1, 1)

        wait_gather(t)
        xs_ref[t] = _unpack_bf16_pairs(xbuf_ref[t])

        @pl.when((t == 0) & (g + 1 < n_groups))
        def _():
            start_gather(2 * (g + 1), 0)

        @pl.when((t == 1) & (next_odd_ref[g] >= 0))
        def _():
            start_gather(2 * next_odd_ref[g] + 1, 1)

    @pl.when(valid & (s < nf))
    def _():
        xs = xs_ref[t]
        h_gate = jnp.dot(xs, wg_ref[...].astype(bf16), preferred_element_type=jnp.float32) + bg_ref[...]
        h_up = jnp.dot(xs, wu_ref[...].astype(bf16), preferred_element_type=jnp.float32) + bu_ref[...]
        h_gate = jnp.minimum(h_gate, SWIGLU_LIMIT)
        h_up = jnp.clip(h_up, -SWIGLU_LIMIT, SWIGLU_LIMIT)
        hid = (h_up + 1.0) * (h_gate * _sigmoid(SWIGLU_ALPHA * h_gate))
        hid_ref[t, :, pl.ds(pl.multiple_of(s * tf, tf), tf)] = hid.astype(bf16)

    @pl.when(valid & (s >= nf))
    def _():
        y = jnp.dot(hid_ref[t], wd_ref[...].astype(bf16), preferred_element_type=jnp.float32) + bd_ref[...]
        y_ref[...] = y.astype(y_ref.dtype)

    @pl.when(jnp.logical_not(valid) & (s >= nf))
    def _():
        y_ref[...] = jnp.zeros_like(y_ref)


def _moe_experts(x_packed, rows, group_e, meta, odd, next_odd, layer, w_gate, w_up, w_down, b_gate, b_up, b_down, tm):
    n_slots = rows.shape[0]
    _, n_experts, d, f = w_gate.shape
    tf, td = min(256, f), min(2048, d)
    nf, nd = f // tf, d // td
    n_groups = n_slots // (MOE_GROUP * tm)

    def f_tile(g, s, meta):
        return jnp.where(g < meta[0], jnp.minimum(s, nf - 1), nf - 1)

    def d_tile(g, s, meta):
        return jnp.where(g < meta[0], jnp.clip(s - nf, 0, nd - 1), nd - 1)

    def w_spec(shape, tile):
        return pl.BlockSpec((None, None) + shape,
                            lambda g, s, t, ge, meta, odd, nxt, rw: (layer, ge[g], 0, tile(g, s, meta)))

    def y_index(g, s, t, ge, meta, odd, nxt, rw):
        return (MOE_GROUP * g + jnp.where(s >= nf, t, 0), jnp.clip(s - nf, 0, nd - 1))

    grid_spec = pltpu.PrefetchScalarGridSpec(
        num_scalar_prefetch=5,
        grid=(n_groups, nf + nd, MOE_GROUP),
        in_specs=[pl.BlockSpec(memory_space=pl.ANY),
                  w_spec((d, tf), f_tile), w_spec((d, tf), f_tile), w_spec((f, td), d_tile),
                  w_spec((1, tf), f_tile), w_spec((1, tf), f_tile), w_spec((1, td), d_tile)],
        out_specs=pl.BlockSpec((tm, td), y_index),
        scratch_shapes=[pltpu.VMEM((MOE_GROUP, tm, f), jnp.bfloat16),
                        pltpu.VMEM((MOE_GROUP, tm, d // 2), jnp.uint32),
                        pltpu.VMEM((MOE_GROUP, tm, d), jnp.bfloat16),
                        pltpu.SemaphoreType.DMA((MOE_GROUP,))],
    )
    return pl.pallas_call(
        functools.partial(_moe_kernel, nf=nf),
        grid_spec=grid_spec,
        out_shape=jax.ShapeDtypeStruct((n_slots, d), jnp.bfloat16),
        compiler_params=_cparams(("arbitrary", "arbitrary", "arbitrary")),
        name="moe_experts",
    )(group_e, meta, odd, next_odd, rows, x_packed, w_gate, w_up, w_down,
      b_gate[:, :, None, :], b_up[:, :, None, :], b_down[:, :, None, :])


def _combine_kernel(yg_ref, gate_ref, x1_ref, g_ref, b_ref, o_ref, ob_ref, *, alpha):
    gates = gate_ref[...]
    f = yg_ref[0].astype(jnp.float32) * gates[:, 0:1]
    for r in range(1, TOP_K):
        f = f + yg_ref[r].astype(jnp.float32) * gates[:, r:r + 1]
    out = _layer_norm_rows(alpha * x1_ref[...] + f, g_ref[...], b_ref[...])
    o_ref[...] = out
    ob_ref[...] = out.astype(ob_ref.dtype)


def _moe_combine(yg, gates, x1, ln_g, ln_b, alpha):
    n, d = x1.shape
    tm = min(256, n)
    return pl.pallas_call(
        functools.partial(_combine_kernel, alpha=alpha),
        grid=(n // tm,),
        in_specs=[pl.BlockSpec((TOP_K, tm, d), lambda i: (0, i, 0)),
                  pl.BlockSpec((tm, LANES), lambda i: (i, 0)),
                  pl.BlockSpec((tm, d), lambda i: (i, 0)),
                  pl.BlockSpec((1, d), lambda i: (0, 0)),
                  pl.BlockSpec((1, d), lambda i: (0, 0))],
        out_specs=[pl.BlockSpec((tm, d), lambda i: (i, 0)),
                   pl.BlockSpec((tm, d), lambda i: (i, 0))],
        out_shape=[jax.ShapeDtypeStruct((n, d), jnp.float32),
                   jax.ShapeDtypeStruct((n, d), jnp.bfloat16)],
        compiler_params=_cparams(("parallel",)),
        name="moe_combine",
    )(yg, gates, x1, ln_g[None, :], ln_b[None, :])


def _moe_layer(x1, x1b, topi, topg, rank, counts, ln_g, ln_b, layer, w_gate, b_gate, w_up, b_up, w_down, b_down,
               alpha):
    n_tok, d = x1.shape
    n_experts = w_gate.shape[1]
    n_assign = n_tok * TOP_K
    tm = min(512, n_assign // n_experts)
    flat_e = topi[:, :TOP_K].reshape(-1)
    rank = rank[:, :TOP_K].reshape(-1)
    counts = counts[0, :n_experts].astype(jnp.int32)
    gm = MOE_GROUP * tm
    padded = (counts + gm - 1) // gm * gm
    pends = jnp.cumsum(padded)
    pstarts = pends - padded
    dest = pstarts[flat_e] + rank
    n_groups = n_assign // gm + n_experts
    flat_tok = jnp.arange(n_assign, dtype=jnp.int32) // TOP_K
    rows = jnp.zeros((n_groups * gm,), jnp.int32).at[dest].set(flat_tok, mode="promise_in_bounds",
                                                               unique_indices=True)
    gidx = jnp.arange(n_groups, dtype=jnp.int32)
    group_e = jnp.minimum(jnp.searchsorted(pends, gidx * gm, side='right'), n_experts - 1).astype(jnp.int32)
    n_used = pends[-1] // gm
    odd = ((gidx < n_used) & (gidx * gm + tm < (pstarts + counts)[group_e])).astype(jnp.int32)
    odd_pos = jnp.where(odd != 0, gidx, n_groups)
    later = jnp.flip(lax.cummin(jnp.flip(odd_pos)))
    first_odd = jnp.where(later[0] < n_groups, later[0], -1)
    nxt = jnp.concatenate([later[1:], jnp.full((1,), n_groups, jnp.int32)])
    next_odd = jnp.where(nxt < n_groups, nxt, -1).astype(jnp.int32)
    meta = jnp.stack([n_used, first_odd]).astype(jnp.int32)
    y = _moe_experts(x1b, rows, group_e, meta, odd, next_odd, layer, w_gate, w_up, w_down, b_gate, b_up, b_down, tm)
    yg = y.at[dest.reshape(n_tok, TOP_K).T].get(mode="promise_in_bounds")
    return _moe_combine(yg, topg, x1, ln_g, ln_b, alpha)


def kernel(x, dn_w_in, dn_conv_w, dn_a_log, dn_dt_bias, dn_norm_w, dn_w_out, mb_w_in, mb_w_out, ln_g, ln_b,
           router_w, router_b, w_gate, b_gate, w_up, b_up, w_down, b_down):
    bsz, t_len, d = x.shape
    depth = ln_g.shape[0]
    alpha = float((2 * depth) ** 0.25)
    n = bsz * t_len
    xf = x.reshape(n, d)
    xb = xf.astype(jnp.bfloat16)
    for i in range(depth):
        j = i // 2
        if i % 2 == 0:
            n_vheads = d // HEAD_DIM
            main = dn_w_in.shape[2] - 2 * n_vheads
            proj = _matmul(xb, dn_w_in[j].astype(jnp.bfloat16), main, jnp.bfloat16, 1024, 512)
            gates = _dn_gates(xf, dn_w_in[j][:, main:], dn_a_log[j], dn_dt_bias[j], n_vheads)
            gates_t = jnp.transpose(gates.reshape(bsz, t_len, LANES), (0, 2, 1))
            h = _dn_delta_rule(proj, gates, gates_t, dn_conv_w[j], dn_norm_w[j], bsz, t_len, d)
            w_out = dn_w_out[j]
        else:
            proj = _matmul(xb, mb_w_in[j].astype(jnp.bfloat16), mb_w_in.shape[2], jnp.bfloat16, 1024, 512)
            h = _moba_attention(proj, bsz, t_len, d // HEAD_DIM)
            w_out = mb_w_out[j]
        x1, x1b, topi, topg, rank, counts = _post_mixer(h, w_out.astype(jnp.bfloat16), xf, ln_g[i, 0], ln_b[i, 0],
                                                        router_w[i], router_b[i], alpha)
        xf, xb = _moe_layer(x1, x1b, topi, topg, rank, counts, ln_g[i, 1], ln_b[i, 1], i, w_gate, b_gate,
                            w_up, b_up, w_down, b_down, alpha)
    return xf.reshape(bsz, t_len, d)
```

```python
import functools

import jax
import jax.numpy as jnp
import numpy as np
from jax import lax
from jax.experimental import pallas as pl
from jax.experimental.pallas import tpu as pltpu

HEAD_DIM = 128
DN_CONV = 4
DN_CHUNK = 256
DN_GROUP = 2
MOE_GROUP = 2
MB_BLOCK = 256
MB_TOPK = 3
TOP_K = 4
SWIGLU_LIMIT = 7.0
SWIGLU_ALPHA = 1.702
LN_EPS = 1e-5
NORM_EPS = 1e-6
NEG_INF = -1e30
LANES = 128
HALO = 16
VMEM_LIMIT = 56 * 1024 * 1024

_HI = lax.Precision.HIGHEST


def _cparams(sem):
    return pltpu.CompilerParams(dimension_semantics=sem, vmem_limit_bytes=VMEM_LIMIT)


def _sigmoid(v):
    return 1.0 / (1.0 + jnp.exp(-v))


def _silu(v):
    return v * _sigmoid(v)


def _mm_kernel(x_ref, w_ref, o_ref):
    o_ref[...] = jnp.dot(x_ref[...], w_ref[...], preferred_element_type=jnp.float32).astype(o_ref.dtype)


def _matmul(x, w, n, out_dtype, tm, tn):
    m, k = x.shape
    tm, tn = min(tm, m), min(tn, n)
    assert n % tn == 0 and m % tm == 0
    return pl.pallas_call(
        _mm_kernel,
        grid=(m // tm, n // tn),
        in_specs=[pl.BlockSpec((tm, k), lambda i, j: (i, 0)),
                  pl.BlockSpec((k, tn), lambda i, j: (0, j))],
        out_specs=pl.BlockSpec((tm, tn), lambda i, j: (i, j)),
        out_shape=jax.ShapeDtypeStruct((m, n), out_dtype),
        compiler_params=_cparams(("parallel", "arbitrary")),
        name="dense_matmul",
    )(x, w)


def _dn_gates_kernel(x_ref, w_ref, alog_ref, dtb_ref, o_ref, *, n_heads):
    tm = x_ref.shape[0]
    ba = jnp.dot(x_ref[...], w_ref[...], precision=_HI, preferred_element_type=jnp.float32)
    lane = lax.broadcasted_iota(jnp.int32, ba.shape, 1)
    beta = _sigmoid(ba)
    v = ba + dtb_ref[...]
    softplus = jnp.maximum(v, 0.0) + jnp.log(1.0 + jnp.exp(-jnp.abs(v)))
    g = -jnp.exp(alog_ref[...]) * softplus
    g = jnp.where((lane >= n_heads) & (lane < 2 * n_heads), g, 0.0)
    ri = lax.broadcasted_iota(jnp.int32, (tm, tm), 0)
    ci = lax.broadcasted_iota(jnp.int32, (tm, tm), 1)
    shift = DN_CHUNK.bit_length() - 1
    tril = jnp.where((ci <= ri) & ((ri >> shift) == (ci >> shift)), 1.0, 0.0)
    gc = jnp.dot(tril, g, precision=_HI, preferred_element_type=jnp.float32)
    o_ref[...] = jnp.where(lane < n_heads, beta, gc)


def _dn_gates(x2d, w_ba, a_log, dt_bias, n_heads):
    n, d = x2d.shape
    tm = min(256, n)
    pad = LANES - 2 * n_heads
    w_p = jnp.pad(w_ba, ((0, 0), (0, pad)))
    alog_p = jnp.pad(a_log, (n_heads, pad))[None, :]
    dtb_p = jnp.pad(dt_bias, (n_heads, pad))[None, :]
    return pl.pallas_call(
        functools.partial(_dn_gates_kernel, n_heads=n_heads),
        grid=(n // tm,),
        in_specs=[pl.BlockSpec((tm, d), lambda i: (i, 0)),
                  pl.BlockSpec((d, LANES), lambda i: (0, 0)),
                  pl.BlockSpec((1, LANES), lambda i: (0, 0)),
                  pl.BlockSpec((1, LANES), lambda i: (0, 0))],
        out_specs=pl.BlockSpec((tm, LANES), lambda i: (i, 0)),
        out_shape=jax.ShapeDtypeStruct((n, LANES), jnp.float32),
        compiler_params=_cparams(("parallel",)),
        name="dn_gates",
    )(x2d, w_p, alog_p, dtb_p)


def _causal_conv_silu(x_ref, halo_ref, w_ref, first):
    tb = x_ref.shape[0]
    halo = jnp.where(first, 0.0, halo_ref[...].astype(jnp.float32))
    xcat = jnp.concatenate([halo, x_ref[...].astype(jnp.float32)], axis=0)
    w = w_ref[...]
    acc = xcat[HALO:] * w[DN_CONV - 1:DN_CONV, :]
    for s in range(1, DN_CONV):
        acc = acc + pltpu.roll(xcat, s, 0)[HALO:] * w[DN_CONV - 1 - s:DN_CONV - s, :]
    del tb
    return _silu(acc)


def _l2norm(v):
    return v * lax.rsqrt(jnp.sum(v * v, axis=-1, keepdims=True) + NORM_EPS)


def _inverse_level_masks(c):
    ri = np.arange(c)[:, None]
    ci = np.arange(c)[None, :]
    masks = [((ri >> 1) == (ci >> 1)) & (ri > ci)]
    s = 2
    while s < c:
        masks.append(((ri // (2 * s)) == (ci // (2 * s))) & ((ri & s) != 0) & ((ci & s) == 0))
        s *= 2
    return jnp.asarray(np.stack(masks), jnp.bfloat16)


def _dn_kernel(q_ref, k_ref, v_ref, z_ref, qh_ref, kh_ref, vh_ref, g_ref, gt_ref,
               wq_ref, wk_ref, wv_ref, nw_ref, lm_ref, negc_ref, o_ref, state_ref, *, n_vheads):
    hg = pl.program_id(1)
    t = pl.program_id(2)
    first = t == 0
    tb = q_ref.shape[0]
    group = q_ref.shape[1] // HEAD_DIM
    rep = v_ref.shape[1] // q_ref.shape[1]
    bf16, f32 = jnp.bfloat16, jnp.float32

    @pl.when(first)
    def _():
        state_ref[...] = jnp.zeros_like(state_ref)

    def head(x, c):
        return x[:, c * HEAD_DIM:(c + 1) * HEAD_DIM]

    q_all = _causal_conv_silu(q_ref, qh_ref, wq_ref, first)
    k_all = _causal_conv_silu(k_ref, kh_ref, wk_ref, first)
    v = _causal_conv_silu(v_ref, vh_ref, wv_ref, first)
    q = [_l2norm(head(q_all, g)) * (HEAD_DIM ** -0.5) for g in range(group)]
    k = [_l2norm(head(k_all, g)) for g in range(group)]
    gates = g_ref[...]
    lane = lax.broadcasted_iota(jnp.int32, gates.shape, 1)
    eye = jnp.where(lax.broadcasted_iota(jnp.int32, (tb, tb), 0) == lax.broadcasted_iota(jnp.int32, (tb, tb), 1),
                    1.0, 0.0)
    nt = (((1,), (1,)), ((), ()))
    k_b = [k[g].astype(bf16) for g in range(group)]
    kk = [lax.dot_general(k_b[g], k_b[g], nt, preferred_element_type=f32) for g in range(group)]
    qk = [lax.dot_general(q[g].astype(bf16), k_b[g], nt, preferred_element_type=f32) for g in range(group)]

    heads = range(group * rep)
    beta, gcol, g_last, decay, a_b, inv = [], [], [], [], [], []
    for c in heads:
        hv = hg * group * rep + c
        beta.append(jnp.sum(jnp.where(lane == hv, gates, 0.0), axis=-1, keepdims=True))
        gcol.append(jnp.sum(jnp.where(lane == n_vheads + hv, gates, 0.0), axis=-1, keepdims=True))
        grow = gt_ref[pl.ds(n_vheads + hv, 1), :]
        g_last.append(grow[:, tb - 1:tb])
        decay.append(jnp.exp(gcol[c] - grow + negc_ref[...]))
        a_b.append((kk[c // rep] * beta[c] * decay[c]).astype(bf16))
        inv.append(eye - (a_b[c] * lm_ref[0]).astype(f32))
    for lvl in range(1, lm_ref.shape[0]):
        tmp = [jnp.dot(a_b[c] * lm_ref[lvl], inv[c].astype(bf16), preferred_element_type=f32) for c in heads]
        inv = [inv[c] - jnp.dot(inv[c].astype(bf16), tmp[c].astype(bf16), preferred_element_type=f32)
               for c in heads]
    eg = [jnp.exp(gcol[c]) for c in heads]
    uw = [jnp.dot(inv[c].astype(bf16),
                  jnp.concatenate([head(v, c) * beta[c], k[c // rep] * (beta[c] * eg[c])], axis=1).astype(bf16),
                  preferred_element_type=f32) for c in heads]
    state = [state_ref[c] for c in heads]
    ws = [jnp.dot(jnp.concatenate([uw[c][:, HEAD_DIM:], q[c // rep] * eg[c]], axis=0).astype(bf16),
                  state[c].astype(bf16), preferred_element_type=f32) for c in heads]
    v_new_b = [(uw[c][:, :HEAD_DIM] - ws[c][:tb]).astype(bf16) for c in heads]
    outs = []
    for c in heads:
        o = ws[c][tb:] + jnp.dot((qk[c // rep] * decay[c]).astype(bf16), v_new_b[c], preferred_element_type=f32)
        k_dec_t = (k[c // rep] * jnp.exp(g_last[c] - gcol[c])).T.astype(bf16)
        state_ref[c] = state[c] * jnp.exp(g_last[c]) + jnp.dot(k_dec_t, v_new_b[c], preferred_element_type=f32)
        zc = head(z_ref[...], c).astype(f32)
        outs.append(o * lax.rsqrt(jnp.mean(o * o, axis=-1, keepdims=True) + NORM_EPS) * nw_ref[...] * _silu(zc))
    o_ref[...] = jnp.concatenate(outs, axis=1).astype(o_ref.dtype)


def _dn_delta_rule(proj, gates, gates_t, conv_w, norm_w, bsz, t_len, d_model):
    n_kheads = d_model // (2 * HEAD_DIM)
    n_vheads = d_model // HEAD_DIM
    group = min(DN_GROUP, n_kheads)
    kw = group * HEAD_DIM
    vw = kw * (n_vheads // n_kheads)
    tb = DN_CHUNK
    nt = t_len // tb
    hb = tb // HALO
    level_masks = _inverse_level_masks(tb)
    ri = np.arange(tb)
    neg_causal = jnp.asarray(np.where(ri[None, :] <= ri[:, None], 0.0, NEG_INF), jnp.float32)
    k_off = n_kheads * HEAD_DIM // kw
    v_off = 2 * n_kheads * HEAD_DIM // vw
    z_off = v_off + n_vheads * HEAD_DIM // vw

    def row(b, h, t):
        return b * nt + t

    def halo_row(b, h, t):
        return jnp.maximum((b * nt + t) * hb - 1, 0)

    in_specs = [
        pl.BlockSpec((tb, kw), lambda b, h, t: (row(b, h, t), h)),
        pl.BlockSpec((tb, kw), lambda b, h, t: (row(b, h, t), k_off + h)),
        pl.BlockSpec((tb, vw), lambda b, h, t: (row(b, h, t), v_off + h)),
        pl.BlockSpec((tb, vw), lambda b, h, t: (row(b, h, t), z_off + h)),
        pl.BlockSpec((HALO, kw), lambda b, h, t: (halo_row(b, h, t), h)),
        pl.BlockSpec((HALO, kw), lambda b, h, t: (halo_row(b, h, t), k_off + h)),
        pl.BlockSpec((HALO, vw), lambda b, h, t: (halo_row(b, h, t), v_off + h)),
        pl.BlockSpec((tb, LANES), lambda b, h, t: (row(b, h, t), 0)),
        pl.BlockSpec((None, LANES, tb), lambda b, h, t: (b, 0, t)),
        pl.BlockSpec((DN_CONV, kw), lambda b, h, t: (0, h)),
        pl.BlockSpec((DN_CONV, kw), lambda b, h, t: (0, k_off + h)),
        pl.BlockSpec((DN_CONV, vw), lambda b, h, t: (0, v_off + h)),
        pl.BlockSpec((1, HEAD_DIM), lambda b, h, t: (0, 0)),
        pl.BlockSpec(level_masks.shape, lambda b, h, t: (0, 0, 0)),
        pl.BlockSpec((tb, tb), lambda b, h, t: (0, 0)),
    ]
    return pl.pallas_call(
        functools.partial(_dn_kernel, n_vheads=n_vheads),
        grid=(bsz, n_kheads // group, nt),
        in_specs=in_specs,
        out_specs=pl.BlockSpec((tb, vw), lambda b, h, t: (row(b, h, t), h)),
        out_shape=jax.ShapeDtypeStruct((bsz * t_len, n_vheads * HEAD_DIM), jnp.bfloat16),
        scratch_shapes=[pltpu.VMEM((vw // HEAD_DIM, HEAD_DIM, HEAD_DIM), jnp.float32)],
        compiler_params=_cparams(("parallel", "parallel", "arbitrary")),
        name="dn_delta_rule",
    )(proj, proj, proj, proj, proj, proj, proj, gates, gates_t, conv_w, conv_w, conv_w, norm_w[None, :],
      level_masks, neg_causal)


def _split3(a):
    hi = a.astype(jnp.bfloat16).astype(jnp.float32)
    mid = (a - hi).astype(jnp.bfloat16).astype(jnp.float32)
    lo = (a - hi - mid).astype(jnp.bfloat16).astype(jnp.float32)
    return hi, mid, lo


def _moba_kernel(slope_ref, q_ref, k_ref, vt_ref, o_ref, kmean_ref, qaug_ref, kaug_ref, s_ref, m_ref, l_ref, acc_ref):
    h = pl.program_id(1)
    own = pl.program_id(2)
    tq = q_ref.shape[0]
    t_len = k_ref.shape[0]
    nb = t_len // MB_BLOCK
    nt = (((1,), (1,)), ((), ()))
    log2e = 1.4426950408889634
    slope2 = slope_ref[h] * log2e

    @pl.when(own == 0)
    def _():
        kf = k_ref[...].astype(jnp.float32).reshape(nb, MB_BLOCK, HEAD_DIM)
        kmean = jnp.sum(kf, axis=1) * (1.0 / MB_BLOCK)
        kmean_hi = kmean.astype(jnp.bfloat16)
        kmean_lo = (kmean - kmean_hi.astype(jnp.float32)).astype(jnp.bfloat16)
        kmean_ref[...] = jnp.concatenate([kmean_hi, kmean_lo], axis=0)
        pos = lax.broadcasted_iota(jnp.int32, (MB_BLOCK, LANES), 0).astype(jnp.float32)
        col = lax.broadcasted_iota(jnp.int32, (MB_BLOCK, LANES), 1)
        q_hi, q_mid, q_lo = _split3(pos * (-slope2))
        k_hi, k_mid, k_lo = _split3(pos * slope2)
        one = jnp.where(col < 6, 1.0, 0.0)
        qaug = jnp.where(col == 0, q_hi, jnp.where(col == 1, q_mid, jnp.where(col == 2, q_lo, one)))
        kaug = jnp.where(col == 3, k_hi, jnp.where(col == 4, k_mid, jnp.where(col == 5, k_lo, one)))
        qaug_ref[...] = qaug.astype(qaug_ref.dtype)
        kaug_ref[...] = kaug.astype(kaug_ref.dtype)

    q = q_ref[...]
    qf = q.astype(jnp.float32)
    gate2 = lax.dot_general(kmean_ref[...], q, nt, preferred_element_type=jnp.float32)
    gate = gate2[:nb] + gate2[nb:]
    blk = lax.broadcasted_iota(jnp.int32, gate.shape, 0).astype(jnp.float32)
    gate = jnp.where(blk < own.astype(jnp.float32), gate, -jnp.inf)
    picks = jnp.zeros((1, tq), jnp.int32)
    for r in range(MB_TOPK):
        mx = jnp.max(gate, axis=0, keepdims=True)
        idx = jnp.min(jnp.where(gate == mx, blk, float(nb)), axis=0, keepdims=True)
        bit = jnp.left_shift(1, idx.astype(jnp.int32))
        picks = picks | jnp.where(r < own, bit, 0)
        gate = jnp.where(blk == idx, -jnp.inf, gate)

    q2 = jnp.concatenate([(qf * (HEAD_DIM ** -0.5 * log2e)).astype(jnp.bfloat16), qaug_ref[...]], axis=1)
    kaug = kaug_ref[...]

    def scores(n):
        start = pl.multiple_of(n * MB_BLOCK, MB_BLOCK)
        k2 = jnp.concatenate([k_ref[pl.ds(start, MB_BLOCK), :], kaug], axis=1)
        return lax.dot_general(k2, q2, nt, preferred_element_type=jnp.float32)

    def values_t(n):
        return vt_ref[:, pl.ds(pl.multiple_of(n * MB_BLOCK, MB_BLOCK), MB_BLOCK)]

    def past_scores(n):
        picked = (picks & jnp.left_shift(1, n)) != 0
        far = (own - n).astype(jnp.float32) * (-slope2 * MB_BLOCK)
        return scores(n) + jnp.where(picked, far, NEG_INF)

    def fold(op, x):
        parts = [x[r:r + 8] for r in range(0, x.shape[0], 8)]
        while len(parts) > 1:
            parts = [op(parts[a], parts[a + 1]) for a in range(0, len(parts) - 1, 2)] + parts[len(parts) & ~1:]
        return parts[0]

    def stage_pair(slot, i):
        s_ref[slot, :MB_BLOCK, :] = past_scores(jnp.minimum(2 * i, nb - 1))
        s_ref[slot, MB_BLOCK:, :] = past_scores(jnp.minimum(2 * i + 1, nb - 1))

    def pair_values_t(i):
        return vt_ref[:, pl.ds(pl.multiple_of(2 * i * MB_BLOCK, 2 * MB_BLOCK), 2 * MB_BLOCK)]

    def absorb(s, v_t):
        m = m_ref[...]
        m_new = jnp.maximum(m, jnp.max(fold(jnp.maximum, s), axis=0, keepdims=True))
        p = jnp.exp2(s - m_new)
        alpha = jnp.exp2(m - m_new)
        m_ref[...] = m_new
        l_ref[...] = alpha * l_ref[...] + fold(jnp.add, p)
        acc_ref[...] = alpha * acc_ref[...] + jnp.dot(v_t, p.astype(jnp.bfloat16),
                                                      preferred_element_type=jnp.float32)

    n_pairs = (own + 1) // 2

    def two_pairs(j, carry):
        i = 2 * j
        stage_pair(1, i + 1)
        absorb(s_ref[0], pair_values_t(i))

        @pl.when(i + 1 < n_pairs)
        def _():
            stage_pair(0, i + 2)
            absorb(s_ref[1], pair_values_t(i + 1))
        return carry

    m_ref[...] = jnp.full(m_ref.shape, NEG_INF, jnp.float32)
    l_ref[...] = jnp.zeros_like(l_ref)
    acc_ref[...] = jnp.zeros_like(acc_ref)
    stage_pair(0, 0)
    lax.fori_loop(0, (n_pairs + 1) // 2, two_pairs, 0)
    key_pos = lax.broadcasted_iota(jnp.int32, (MB_BLOCK, tq), 0)
    query_pos = lax.broadcasted_iota(jnp.int32, (MB_BLOCK, tq), 1)
    absorb(jnp.where(key_pos <= query_pos, scores(own), NEG_INF), values_t(own))
    l = jnp.sum(l_ref[...], axis=0, keepdims=True)
    o_ref[...] = (acc_ref[...] / l).T.astype(o_ref.dtype)


def _moba_attention(proj, bsz, t_len, n_heads):
    tq = MB_BLOCK
    nq = t_len // tq
    nb = t_len // MB_BLOCK
    assert nb <= 32, "per-row picks are kept as one int32 bit mask"
    slopes = jnp.exp2(-8.0 * jnp.arange(1, n_heads + 1, dtype=jnp.float32) / n_heads)
    v_t = jnp.transpose(proj[:, 2 * n_heads * HEAD_DIM:])
    grid_spec = pltpu.PrefetchScalarGridSpec(
        num_scalar_prefetch=1,
        grid=(bsz, n_heads, nq),
        in_specs=[pl.BlockSpec((tq, HEAD_DIM), lambda b, h, i, s: (b * nq + i, h)),
                  pl.BlockSpec((t_len, HEAD_DIM), lambda b, h, i, s: (b, n_heads + h)),
                  pl.BlockSpec((HEAD_DIM, t_len), lambda b, h, i, s: (h, b))],
        out_specs=pl.BlockSpec((tq, HEAD_DIM), lambda b, h, i, s: (b * nq + i, h)),
        scratch_shapes=[pltpu.VMEM((2 * nb, HEAD_DIM), jnp.bfloat16),
                        pltpu.VMEM((MB_BLOCK, LANES), jnp.bfloat16),
                        pltpu.VMEM((MB_BLOCK, LANES), jnp.bfloat16),
                        pltpu.VMEM((2, 2 * MB_BLOCK, tq), jnp.float32),
                        pltpu.VMEM((1, tq), jnp.float32),
                        pltpu.VMEM((8, tq), jnp.float32),
                        pltpu.VMEM((HEAD_DIM, tq), jnp.float32)],
    )
    return pl.pallas_call(
        _moba_kernel,
        grid_spec=grid_spec,
        out_shape=jax.ShapeDtypeStruct((bsz * t_len, n_heads * HEAD_DIM), jnp.bfloat16),
        compiler_params=_cparams(("parallel", "parallel", "arbitrary")),
        name="moba_attention",
    )(slopes, proj, proj, v_t)


def _pack_bf16_pairs(x):
    half = x.shape[1] // 2
    bits = lax.bitcast_convert_type(x.astype(jnp.float32), jnp.uint32)
    return (bits[:, :half] >> 16) | bits[:, half:]


def _layer_norm_rows(y, g, b):
    mu = jnp.mean(y, axis=-1, keepdims=True)
    yc = y - mu
    var = jnp.mean(yc * yc, axis=-1, keepdims=True)
    return yc * lax.rsqrt(var + LN_EPS) * g + b


def _post_mixer_kernel(h_ref, w_ref, xres_ref, g_ref, b_ref, rw_ref, rb_ref,
                       x1_ref, x1b_ref, topi_ref, topg_ref, rank_ref, count_ref, *, alpha, n_experts):
    i = pl.program_id(0)
    j = pl.program_id(1)
    tm = h_ref.shape[0]
    tn = w_ref.shape[1]
    col = pl.multiple_of(j * tn, tn)
    x1_ref[:, pl.ds(col, tn)] = jnp.dot(h_ref[...], w_ref[...], preferred_element_type=jnp.float32)

    @pl.when((i == 0) & (j == 0))
    def _():
        count_ref[...] = jnp.zeros_like(count_ref)

    @pl.when(j == pl.num_programs(1) - 1)
    def _():
        x1 = _layer_norm_rows(alpha * xres_ref[...] + x1_ref[...], g_ref[...], b_ref[...])
        x1_ref[...] = x1
        x1_hi = x1.astype(jnp.bfloat16)
        packed = _pack_bf16_pairs(x1_hi)
        chunks = packed.shape[1] // LANES
        for c in range(chunks):
            x1b_ref[pl.ds(c, tm, stride=chunks), :] = packed[:, c * LANES:(c + 1) * LANES]
        x1_lo = (x1 - x1_hi.astype(jnp.float32)).astype(jnp.bfloat16)
        hi_terms = jnp.dot(x1_hi, rw_ref[...], preferred_element_type=jnp.float32)
        lo_term = jnp.dot(x1_lo, rw_ref[:, :LANES], preferred_element_type=jnp.float32)
        logits = hi_terms[:, :LANES] + hi_terms[:, LANES:] + lo_term + rb_ref[...]
        lane_i = lax.broadcasted_iota(jnp.int32, logits.shape, 1)
        lane = lane_i.astype(jnp.float32)
        logits = jnp.where(lane_i < n_experts, logits, -jnp.inf)
        topi = jnp.zeros(logits.shape, jnp.float32)
        topv = jnp.full(logits.shape, -jnp.inf, jnp.float32)
        chosen = []
        for r in range(TOP_K):
            mx = jnp.max(logits, axis=-1, keepdims=True)
            idx = jnp.min(jnp.where(logits == mx, lane, float(LANES)), axis=-1, keepdims=True)
            hit = lane == idx
            chosen.append(hit)
            topi = jnp.where(lane_i == r, idx, topi)
            topv = jnp.where(lane_i == r, mx, topv)
            logits = jnp.where(hit, -jnp.inf, logits)
        e = jnp.exp(topv - jnp.max(topv, axis=-1, keepdims=True))
        topi_ref[...] = topi.astype(jnp.int32)
        topg_ref[...] = e / jnp.sum(e, axis=-1, keepdims=True)
        onehot = jnp.where(functools.reduce(jnp.logical_or, chosen), 1.0, 0.0)
        earlier = jnp.where(lax.broadcasted_iota(jnp.int32, (tm, tm), 1) < lax.broadcasted_iota(jnp.int32, (tm, tm), 0),
                            1.0, 0.0).astype(jnp.bfloat16)
        before = count_ref[...] + jnp.dot(earlier, onehot.astype(jnp.bfloat16), preferred_element_type=jnp.float32)
        rank = jnp.zeros(logits.shape, jnp.float32)
        for r in range(TOP_K):
            rank = jnp.where(lane_i == r, jnp.sum(jnp.where(chosen[r], before, 0.0), axis=-1, keepdims=True), rank)
        rank_ref[...] = rank.astype(jnp.int32)
        count_ref[...] += jnp.sum(onehot, axis=0, keepdims=True)


def _post_mixer(h, w_out, xres, ln_g, ln_b, router_w, router_b, alpha):
    n, kdim = h.shape
    d = w_out.shape[1]
    n_experts = router_w.shape[1]
    tm, tn = min(256, n), min(512, d)
    rw = jnp.pad(router_w, ((0, 0), (0, LANES - n_experts)))
    rw_hi = rw.astype(jnp.bfloat16)
    rw_lo = (rw - rw_hi.astype(jnp.float32)).astype(jnp.bfloat16)
    rw = jnp.concatenate([rw_hi, rw_lo], axis=1)
    rb = jnp.pad(router_b, (0, LANES - n_experts))[None, :]
    row = lambda i, j: (i, 0)
    fixed = lambda i, j: (0, 0)
    return pl.pallas_call(
        functools.partial(_post_mixer_kernel, alpha=alpha, n_experts=n_experts),
        grid=(n // tm, d // tn),
        in_specs=[pl.BlockSpec((tm, kdim), row),
                  pl.BlockSpec((kdim, tn), lambda i, j: (0, j)),
                  pl.BlockSpec((tm, d), row),
                  pl.BlockSpec((1, d), fixed),
                  pl.BlockSpec((1, d), fixed),
                  pl.BlockSpec((d, 2 * LANES), fixed),
                  pl.BlockSpec((1, LANES), fixed)],
        out_specs=[pl.BlockSpec((tm, d), row),
                   pl.BlockSpec((tm * (d // 2 // LANES), LANES), row),
                   pl.BlockSpec((tm, LANES), row),
                   pl.BlockSpec((tm, LANES), row),
                   pl.BlockSpec((tm, LANES), row),
                   pl.BlockSpec((1, LANES), fixed)],
        out_shape=[jax.ShapeDtypeStruct((n, d), jnp.float32),
                   jax.ShapeDtypeStruct((n * (d // 2 // LANES), LANES), jnp.uint32),
                   jax.ShapeDtypeStruct((n, LANES), jnp.int32),
                   jax.ShapeDtypeStruct((n, LANES), jnp.float32),
                   jax.ShapeDtypeStruct((n, LANES), jnp.int32),
                   jax.ShapeDtypeStruct((1, LANES), jnp.float32)],
        compiler_params=_cparams(("arbitrary", "arbitrary")),
        name="post_mixer",
    )(h, w_out, xres, ln_g[None, :], ln_b[None, :], rw, rb)


def _moe_kernel(ge_ref, meta_ref, odd_ref, next_odd_ref, rows_ref, x_hbm, wg_ref, wu_ref, wd_ref, bg_ref, bu_ref,
                bd_ref, y_ref, hid_ref, xbuf_ref, xs_ref, sem, *, nf):
    g = pl.program_id(0)
    s = pl.program_id(1)
    t = pl.program_id(2)
    n_groups = meta_ref[0]
    first_odd = meta_ref[1]
    valid = (g < n_groups) & ((t == 0) | (odd_ref[g] != 0))
    tm = xs_ref.shape[1]
    tf = wg_ref.shape[1]
    bf16 = jnp.bfloat16

    chunks = xs_ref.shape[2] // (2 * LANES)
    pitch = chunks + 1

    def gather_copy(tile, r, slot):
        token = rows_ref[tile * tm + r]
        return pltpu.make_async_copy(x_hbm.at[pl.ds(token * chunks, chunks)],
                                     xbuf_ref.at[slot, pl.ds(r * pitch, chunks)], sem.at[slot])

    def start_gather(tile, slot):
        def issue(r, carry):
            gather_copy(tile, r, slot).start()
            return carry
        lax.fori_loop(0, tm, issue, 0, unroll=8)

    def wait_gather(slot):
        done = xbuf_ref.at[slot, pl.ds(0, tm * chunks)]
        pltpu.make_async_copy(done, done, sem.at[slot]).wait()

    def unpack_rows(slot):
        half = chunks * LANES
        for c in range(chunks):
            p = xbuf_ref[slot, pl.ds(c, tm, stride=pitch), :]
            lo = lax.bitcast_convert_type(p << 16, jnp.float32)
            hi = lax.bitcast_convert_type(p & jnp.uint32(0xFFFF0000), jnp.float32)
            xs_ref[slot, :, c * LANES:(c + 1) * LANES] = lo.astype(bf16)
            xs_ref[slot, :, half + c * LANES:half + (c + 1) * LANES] = hi.astype(bf16)

    @pl.when(valid & (s == 0))
    def _():
        @pl.when((g == 0) & (t == 0))
        def _():
            start_gather(0, 0)

            @pl.when(first_odd >= 0)
            def _():
                start_gather(2 * first_odd + 1, 1)

        wait_gather(t)
        unpack_rows(t)

        @pl.when((t == 0) & (g + 1 < n_groups))
        def _():
            start_gather(2 * (g + 1), 0)

        @pl.when((t == 1) & (next_odd_ref[g] >= 0))
        def _():
            start_gather(2 * next_odd_ref[g] + 1, 1)

    @pl.when(valid & (s < nf))
    def _():
        xs = xs_ref[t]
        h_gate = jnp.dot(xs, wg_ref[...].astype(bf16), preferred_element_type=jnp.float32) + bg_ref[...]
        h_up = jnp.dot(xs, wu_ref[...].astype(bf16), preferred_element_type=jnp.float32) + bu_ref[...]
        h_gate = jnp.minimum(h_gate, SWIGLU_LIMIT)
        h_up = jnp.clip(h_up, -SWIGLU_LIMIT, SWIGLU_LIMIT)
        hid = (h_up + 1.0) * (h_gate * _sigmoid(SWIGLU_ALPHA * h_gate))
        hid_ref[t, :, pl.ds(pl.multiple_of(s * tf, tf), tf)] = hid.astype(bf16)

    @pl.when(valid & (s >= nf))
    def _():
        y = jnp.dot(hid_ref[t], wd_ref[...].astype(bf16), preferred_element_type=jnp.float32) + bd_ref[...]
        y_ref[...] = y.astype(y_ref.dtype)

    @pl.when(jnp.logical_not(valid) & (s >= nf))
    def _():
        y_ref[...] = jnp.zeros_like(y_ref)


def _moe_experts(x_packed, rows, group_e, meta, odd, next_odd, layer, w_gate, w_up, w_down, b_gate, b_up, b_down, tm):
    n_slots = rows.shape[0]
    _, n_experts, d, f = w_gate.shape
    tf, td = min(256, f), min(2048, d)
    nf, nd = f // tf, d // td
    n_groups = n_slots // (MOE_GROUP * tm)

    def f_tile(g, s, meta):
        return jnp.where(g < meta[0], jnp.minimum(s, nf - 1), nf - 1)

    def d_tile(g, s, meta):
        return jnp.where(g < meta[0], jnp.clip(s - nf, 0, nd - 1), nd - 1)

    def w_spec(shape, tile):
        return pl.BlockSpec((None, None) + shape,
                            lambda g, s, t, ge, meta, odd, nxt, rw: (layer, ge[g], 0, tile(g, s, meta)))

    def y_index(g, s, t, ge, meta, odd, nxt, rw):
        return (MOE_GROUP * g + jnp.where(s >= nf, t, 0), jnp.clip(s - nf, 0, nd - 1))

    grid_spec = pltpu.PrefetchScalarGridSpec(
        num_scalar_prefetch=5,
        grid=(n_groups, nf + nd, MOE_GROUP),
        in_specs=[pl.BlockSpec(memory_space=pl.ANY),
                  w_spec((d, tf), f_tile), w_spec((d, tf), f_tile), w_spec((f, td), d_tile),
                  w_spec((1, tf), f_tile), w_spec((1, tf), f_tile), w_spec((1, td), d_tile)],
        out_specs=pl.BlockSpec((tm, td), y_index),
        scratch_shapes=[pltpu.VMEM((MOE_GROUP, tm, f), jnp.bfloat16),
                        pltpu.VMEM((MOE_GROUP, tm * (d // 2 // LANES + 1), LANES), jnp.uint32),
                        pltpu.VMEM((MOE_GROUP, tm, d), jnp.bfloat16),
                        pltpu.SemaphoreType.DMA((MOE_GROUP,))],
    )
    return pl.pallas_call(
        functools.partial(_moe_kernel, nf=nf),
        grid_spec=grid_spec,
        out_shape=jax.ShapeDtypeStruct((n_slots, d), jnp.bfloat16),
        compiler_params=_cparams(("arbitrary", "arbitrary", "arbitrary")),
        name="moe_experts",
    )(group_e, meta, odd, next_odd, rows, x_packed, w_gate, w_up, w_down,
      b_gate[:, :, None, :], b_up[:, :, None, :], b_down[:, :, None, :])


def _combine_kernel(yg_ref, gate_ref, x1_ref, g_ref, b_ref, o_ref, ob_ref, *, alpha):
    gates = gate_ref[...]
    f = yg_ref[0].astype(jnp.float32) * gates[:, 0:1]
    for r in range(1, TOP_K):
        f = f + yg_ref[r].astype(jnp.float32) * gates[:, r:r + 1]
    out = _layer_norm_rows(alpha * x1_ref[...] + f, g_ref[...], b_ref[...])
    o_ref[...] = out
    ob_ref[...] = out.astype(ob_ref.dtype)


def _moe_combine(yg, gates, x1, ln_g, ln_b, alpha):
    n, d = x1.shape
    tm = min(256, n)
    return pl.pallas_call(
        functools.partial(_combine_kernel, alpha=alpha),
        grid=(n // tm,),
        in_specs=[pl.BlockSpec((TOP_K, tm, d), lambda i: (0, i, 0)),
                  pl.BlockSpec((tm, LANES), lambda i: (i, 0)),
                  pl.BlockSpec((tm, d), lambda i: (i, 0)),
                  pl.BlockSpec((1, d), lambda i: (0, 0)),
                  pl.BlockSpec((1, d), lambda i: (0, 0))],
        out_specs=[pl.BlockSpec((tm, d), lambda i: (i, 0)),
                   pl.BlockSpec((tm, d), lambda i: (i, 0))],
        out_shape=[jax.ShapeDtypeStruct((n, d), jnp.float32),
                   jax.ShapeDtypeStruct((n, d), jnp.bfloat16)],
        compiler_params=_cparams(("parallel",)),
        name="moe_combine",
    )(yg, gates, x1, ln_g[None, :], ln_b[None, :])


def _moe_layer(x1, x1b, topi, topg, rank, counts, ln_g, ln_b, layer, w_gate, b_gate, w_up, b_up, w_down, b_down,
               alpha):
    n_tok, d = x1.shape
    n_experts = w_gate.shape[1]
    n_assign = n_tok * TOP_K
    tm = min(512, n_assign // n_experts)
    flat_e = topi[:, :TOP_K].reshape(-1)
    rank = rank[:, :TOP_K].reshape(-1)
    counts = counts[0, :n_experts].astype(jnp.int32)
    gm = MOE_GROUP * tm
    padded = (counts + gm - 1) // gm * gm
    pends = jnp.cumsum(padded)
    pstarts = pends - padded
    dest = pstarts[flat_e] + rank
    n_groups = n_assign // gm + n_experts
    flat_tok = jnp.arange(n_assign, dtype=jnp.int32) // TOP_K
    rows = jnp.zeros((n_groups * gm,), jnp.int32).at[dest].set(flat_tok, mode="promise_in_bounds",
                                                               unique_indices=True)
    gidx = jnp.arange(n_groups, dtype=jnp.int32)
    group_e = jnp.minimum(jnp.searchsorted(pends, gidx * gm, side='right'), n_experts - 1).astype(jnp.int32)
    n_used = pends[-1] // gm
    odd = ((gidx < n_used) & (gidx * gm + tm < (pstarts + counts)[group_e])).astype(jnp.int32)
    odd_pos = jnp.where(odd != 0, gidx, n_groups)
    later = jnp.flip(lax.cummin(jnp.flip(odd_pos)))
    first_odd = jnp.where(later[0] < n_groups, later[0], -1)
    nxt = jnp.concatenate([later[1:], jnp.full((1,), n_groups, jnp.int32)])
    next_odd = jnp.where(nxt < n_groups, nxt, -1).astype(jnp.int32)
    meta = jnp.stack([n_used, first_odd]).astype(jnp.int32)
    y = _moe_experts(x1b, rows, group_e, meta, odd, next_odd, layer, w_gate, w_up, w_down, b_gate, b_up, b_down, tm)
    yg = y.at[dest.reshape(n_tok, TOP_K).T].get(mode="promise_in_bounds")
    return _moe_combine(yg, topg, x1, ln_g, ln_b, alpha)


def kernel(x, dn_w_in, dn_conv_w, dn_a_log, dn_dt_bias, dn_norm_w, dn_w_out, mb_w_in, mb_w_out, ln_g, ln_b,
           router_w, router_b, w_gate, b_gate, w_up, b_up, w_down, b_down):
    bsz, t_len, d = x.shape
    depth = ln_g.shape[0]
    alpha = float((2 * depth) ** 0.25)
    n = bsz * t_len
    xf = x.reshape(n, d)
    xb = xf.astype(jnp.bfloat16)
    for i in range(depth):
        j = i // 2
        if i % 2 == 0:
            n_vheads = d // HEAD_DIM
            main = dn_w_in.shape[2] - 2 * n_vheads
            proj = _matmul(xb, dn_w_in[j].astype(jnp.bfloat16), main, jnp.bfloat16, 1024, 512)
            gates = _dn_gates(xf, dn_w_in[j][:, main:], dn_a_log[j], dn_dt_bias[j], n_vheads)
            gates_t = jnp.transpose(gates.reshape(bsz, t_len, LANES), (0, 2, 1))
            h = _dn_delta_rule(proj, gates, gates_t, dn_conv_w[j], dn_norm_w[j], bsz, t_len, d)
            w_out = dn_w_out[j]
        else:
            proj = _matmul(xb, mb_w_in[j].astype(jnp.bfloat16), mb_w_in.shape[2], jnp.bfloat16, 1024, 512)
            h = _moba_attention(proj, bsz, t_len, d // HEAD_DIM)
            w_out = mb_w_out[j]
        x1, x1b, topi, topg, rank, counts = _post_mixer(h, w_out.astype(jnp.bfloat16), xf, ln_g[i, 0], ln_b[i, 0],
                                                        router_w[i], router_b[i], alpha)
        xf, xb = _moe_layer(x1, x1b, topi, topg, rank, counts, ln_g[i, 1], ln_b[i, 1], i, w_gate, b_gate,
                            w_up, b_up, w_down, b_down, alpha)
    return xf.reshape(bsz, t_len, d)
```

```python
import functools

import jax
import jax.numpy as jnp
import numpy as np
from jax import lax
from jax.experimental import pallas as pl
from jax.experimental.pallas import tpu as pltpu

HEAD_DIM = 128
DN_CONV = 4
DN_CHUNK = 256
DN_GROUP = 4
MOE_GROUP = 2
MB_BLOCK = 256
MB_TOPK = 3
TOP_K = 4
SWIGLU_LIMIT = 7.0
SWIGLU_ALPHA = 1.702
LN_EPS = 1e-5
NORM_EPS = 1e-6
NEG_INF = -1e30
LANES = 128
HALO = 16
VMEM_LIMIT = 56 * 1024 * 1024

_HI = lax.Precision.HIGHEST


def _cparams(sem):
    return pltpu.CompilerParams(dimension_semantics=sem, vmem_limit_bytes=VMEM_LIMIT)


def _sigmoid(v):
    return 1.0 / (1.0 + jnp.exp(-v))


def _silu(v):
    return v * _sigmoid(v)


def _mm_kernel(x_ref, w_ref, o_ref):
    o_ref[...] = jnp.dot(x_ref[...], w_ref[...], preferred_element_type=jnp.float32).astype(o_ref.dtype)


def _matmul(x, w, n, out_dtype, tm, tn):
    m, k = x.shape
    tm, tn = min(tm, m), min(tn, n)
    assert n % tn == 0 and m % tm == 0
    return pl.pallas_call(
        _mm_kernel,
        grid=(m // tm, n // tn),
        in_specs=[pl.BlockSpec((tm, k), lambda i, j: (i, 0)),
                  pl.BlockSpec((k, tn), lambda i, j: (0, j))],
        out_specs=pl.BlockSpec((tm, tn), lambda i, j: (i, j)),
        out_shape=jax.ShapeDtypeStruct((m, n), out_dtype),
        compiler_params=_cparams(("parallel", "arbitrary")),
        name="dense_matmul",
    )(x, w)


def _dn_gates_kernel(x_ref, w_ref, alog_ref, dtb_ref, o_ref, *, n_heads):
    tm = x_ref.shape[0]
    ba = jnp.dot(x_ref[...], w_ref[...], precision=_HI, preferred_element_type=jnp.float32)
    lane = lax.broadcasted_iota(jnp.int32, ba.shape, 1)
    beta = _sigmoid(ba)
    v = ba + dtb_ref[...]
    softplus = jnp.maximum(v, 0.0) + jnp.log(1.0 + jnp.exp(-jnp.abs(v)))
    g = -jnp.exp(alog_ref[...]) * softplus
    g = jnp.where((lane >= n_heads) & (lane < 2 * n_heads), g, 0.0)
    ri = lax.broadcasted_iota(jnp.int32, (tm, tm), 0)
    ci = lax.broadcasted_iota(jnp.int32, (tm, tm), 1)
    shift = DN_CHUNK.bit_length() - 1
    tril = jnp.where((ci <= ri) & ((ri >> shift) == (ci >> shift)), 1.0, 0.0)
    gc = jnp.dot(tril, g, precision=_HI, preferred_element_type=jnp.float32)
    o_ref[...] = jnp.where(lane < n_heads, beta, gc)


def _dn_gates(x2d, w_ba, a_log, dt_bias, n_heads):
    n, d = x2d.shape
    tm = min(256, n)
    pad = LANES - 2 * n_heads
    w_p = jnp.pad(w_ba, ((0, 0), (0, pad)))
    alog_p = jnp.pad(a_log, (n_heads, pad))[None, :]
    dtb_p = jnp.pad(dt_bias, (n_heads, pad))[None, :]
    return pl.pallas_call(
        functools.partial(_dn_gates_kernel, n_heads=n_heads),
        grid=(n // tm,),
        in_specs=[pl.BlockSpec((tm, d), lambda i: (i, 0)),
                  pl.BlockSpec((d, LANES), lambda i: (0, 0)),
                  pl.BlockSpec((1, LANES), lambda i: (0, 0)),
                  pl.BlockSpec((1, LANES), lambda i: (0, 0))],
        out_specs=pl.BlockSpec((tm, LANES), lambda i: (i, 0)),
        out_shape=jax.ShapeDtypeStruct((n, LANES), jnp.float32),
        compiler_params=_cparams(("parallel",)),
        name="dn_gates",
    )(x2d, w_p, alog_p, dtb_p)


def _causal_conv_silu(x_ref, halo_ref, w_ref, first):
    tb = x_ref.shape[0]
    halo = jnp.where(first, 0.0, halo_ref[...].astype(jnp.float32))
    xcat = jnp.concatenate([halo, x_ref[...].astype(jnp.float32)], axis=0)
    w = w_ref[...]
    acc = xcat[HALO:] * w[DN_CONV - 1:DN_CONV, :]
    for s in range(1, DN_CONV):
        acc = acc + pltpu.roll(xcat, s, 0)[HALO:] * w[DN_CONV - 1 - s:DN_CONV - s, :]
    del tb
    return _silu(acc)


def _l2norm(v):
    return v * lax.rsqrt(jnp.sum(v * v, axis=-1, keepdims=True) + NORM_EPS)


def _inverse_level_masks(c):
    ri = np.arange(c)[:, None]
    ci = np.arange(c)[None, :]
    masks = [((ri >> 1) == (ci >> 1)) & (ri > ci)]
    s = 2
    while s < c:
        masks.append(((ri // (2 * s)) == (ci // (2 * s))) & ((ri & s) != 0) & ((ci & s) == 0))
        s *= 2
    return jnp.asarray(np.stack(masks), jnp.bfloat16)


def _dn_kernel(q_ref, k_ref, v_ref, z_ref, qh_ref, kh_ref, vh_ref, g_ref, gt_ref,
               wq_ref, wk_ref, wv_ref, nw_ref, lm_ref, negc_ref, o_ref, state_ref, *, n_vheads):
    hg = pl.program_id(1)
    t = pl.program_id(2)
    first = t == 0
    tb = q_ref.shape[0]
    group = q_ref.shape[1] // HEAD_DIM
    rep = v_ref.shape[1] // q_ref.shape[1]
    bf16, f32 = jnp.bfloat16, jnp.float32

    @pl.when(first)
    def _():
        state_ref[...] = jnp.zeros_like(state_ref)

    def head(x, c):
        return x[:, c * HEAD_DIM:(c + 1) * HEAD_DIM]

    q_all = _causal_conv_silu(q_ref, qh_ref, wq_ref, first)
    k_all = _causal_conv_silu(k_ref, kh_ref, wk_ref, first)
    v = _causal_conv_silu(v_ref, vh_ref, wv_ref, first)
    q = [_l2norm(head(q_all, g)) * (HEAD_DIM ** -0.5) for g in range(group)]
    k = [_l2norm(head(k_all, g)) for g in range(group)]
    gates = g_ref[...]
    lane = lax.broadcasted_iota(jnp.int32, gates.shape, 1)
    eye = jnp.where(lax.broadcasted_iota(jnp.int32, (tb, tb), 0) == lax.broadcasted_iota(jnp.int32, (tb, tb), 1),
                    1.0, 0.0)
    nt = (((1,), (1,)), ((), ()))
    k_b = [k[g].astype(bf16) for g in range(group)]
    kk = [lax.dot_general(k_b[g], k_b[g], nt, preferred_element_type=f32) for g in range(group)]
    qk = [lax.dot_general(q[g].astype(bf16), k_b[g], nt, preferred_element_type=f32) for g in range(group)]

    heads = range(group * rep)
    beta, gcol, g_last, decay, a_b, inv = [], [], [], [], [], []
    for c in heads:
        hv = hg * group * rep + c
        beta.append(jnp.sum(jnp.where(lane == hv, gates, 0.0), axis=-1, keepdims=True))
        gcol.append(jnp.sum(jnp.where(lane == n_vheads + hv, gates, 0.0), axis=-1, keepdims=True))
        grow = gt_ref[pl.ds(n_vheads + hv, 1), :]
        g_last.append(grow[:, tb - 1:tb])
        decay.append(jnp.exp(gcol[c] - grow + negc_ref[...]))
        a_b.append((kk[c // rep] * beta[c] * decay[c]).astype(bf16))
        inv.append(eye - (a_b[c] * lm_ref[0]).astype(f32))
    for lvl in range(1, lm_ref.shape[0]):
        tmp = [jnp.dot(a_b[c] * lm_ref[lvl], inv[c].astype(bf16), preferred_element_type=f32) for c in heads]
        inv = [inv[c] - jnp.dot(inv[c].astype(bf16), tmp[c].astype(bf16), preferred_element_type=f32)
               for c in heads]
    eg = [jnp.exp(gcol[c]) for c in heads]
    uw = [jnp.dot(inv[c].astype(bf16),
                  jnp.concatenate([head(v, c) * beta[c], k[c // rep] * (beta[c] * eg[c])], axis=1).astype(bf16),
                  preferred_element_type=f32) for c in heads]
    state = [state_ref[c] for c in heads]
    ws = [jnp.dot(jnp.concatenate([uw[c][:, HEAD_DIM:], q[c // rep] * eg[c]], axis=0).astype(bf16),
                  state[c].astype(bf16), preferred_element_type=f32) for c in heads]
    v_new_b = [(uw[c][:, :HEAD_DIM] - ws[c][:tb]).astype(bf16) for c in heads]
    outs = []
    for c in heads:
        o = ws[c][tb:] + jnp.dot((qk[c // rep] * decay[c]).astype(bf16), v_new_b[c], preferred_element_type=f32)
        k_dec_t = (k[c // rep] * jnp.exp(g_last[c] - gcol[c])).T.astype(bf16)
        state_ref[c] = state[c] * jnp.exp(g_last[c]) + jnp.dot(k_dec_t, v_new_b[c], preferred_element_type=f32)
        zc = head(z_ref[...], c).astype(f32)
        outs.append(o * lax.rsqrt(jnp.mean(o * o, axis=-1, keepdims=True) + NORM_EPS) * nw_ref[...] * _silu(zc))
    o_ref[...] = jnp.concatenate(outs, axis=1).astype(o_ref.dtype)


def _dn_delta_rule(proj, gates, gates_t, conv_w, norm_w, bsz, t_len, d_model):
    n_kheads = d_model // (2 * HEAD_DIM)
    n_vheads = d_model // HEAD_DIM
    group = min(DN_GROUP, n_kheads)
    kw = group * HEAD_DIM
    vw = kw * (n_vheads // n_kheads)
    tb = DN_CHUNK
    nt = t_len // tb
    hb = tb // HALO
    level_masks = _inverse_level_masks(tb)
    ri = np.arange(tb)
    neg_causal = jnp.asarray(np.where(ri[None, :] <= ri[:, None], 0.0, NEG_INF), jnp.float32)
    k_off = n_kheads * HEAD_DIM // kw
    v_off = 2 * n_kheads * HEAD_DIM // vw
    z_off = v_off + n_vheads * HEAD_DIM // vw

    def row(b, h, t):
        return b * nt + t

    def halo_row(b, h, t):
        return jnp.maximum((b * nt + t) * hb - 1, 0)

    in_specs = [
        pl.BlockSpec((tb, kw), lambda b, h, t: (row(b, h, t), h)),
        pl.BlockSpec((tb, kw), lambda b, h, t: (row(b, h, t), k_off + h)),
        pl.BlockSpec((tb, vw), lambda b, h, t: (row(b, h, t), v_off + h)),
        pl.BlockSpec((tb, vw), lambda b, h, t: (row(b, h, t), z_off + h)),
        pl.BlockSpec((HALO, kw), lambda b, h, t: (halo_row(b, h, t), h)),
        pl.BlockSpec((HALO, kw), lambda b, h, t: (halo_row(b, h, t), k_off + h)),
        pl.BlockSpec((HALO, vw), lambda b, h, t: (halo_row(b, h, t), v_off + h)),
        pl.BlockSpec((tb, LANES), lambda b, h, t: (row(b, h, t), 0)),
        pl.BlockSpec((None, LANES, tb), lambda b, h, t: (b, 0, t)),
        pl.BlockSpec((DN_CONV, kw), lambda b, h, t: (0, h)),
        pl.BlockSpec((DN_CONV, kw), lambda b, h, t: (0, k_off + h)),
        pl.BlockSpec((DN_CONV, vw), lambda b, h, t: (0, v_off + h)),
        pl.BlockSpec((1, HEAD_DIM), lambda b, h, t: (0, 0)),
        pl.BlockSpec(level_masks.shape, lambda b, h, t: (0, 0, 0)),
        pl.BlockSpec((tb, tb), lambda b, h, t: (0, 0)),
    ]
    return pl.pallas_call(
        functools.partial(_dn_kernel, n_vheads=n_vheads),
        grid=(bsz, n_kheads // group, nt),
        in_specs=in_specs,
        out_specs=pl.BlockSpec((tb, vw), lambda b, h, t: (row(b, h, t), h)),
        out_shape=jax.ShapeDtypeStruct((bsz * t_len, n_vheads * HEAD_DIM), jnp.bfloat16),
        scratch_shapes=[pltpu.VMEM((vw // HEAD_DIM, HEAD_DIM, HEAD_DIM), jnp.float32)],
        compiler_params=_cparams(("parallel", "parallel", "arbitrary")),
        name="dn_delta_rule",
    )(proj, proj, proj, proj, proj, proj, proj, gates, gates_t, conv_w, conv_w, conv_w, norm_w[None, :],
      level_masks, neg_causal)


def _split3(a):
    hi = a.astype(jnp.bfloat16).astype(jnp.float32)
    mid = (a - hi).astype(jnp.bfloat16).astype(jnp.float32)
    lo = (a - hi - mid).astype(jnp.bfloat16).astype(jnp.float32)
    return hi, mid, lo


def _moba_kernel(slope_ref, q_ref, k_ref, vt_ref, o_ref, kmean_ref, qaug_ref, kaug_ref, s_ref, s_own_ref, m_ref, l_ref,
                 acc_ref):
    h = pl.program_id(1)
    own = pl.program_id(2)
    tq = q_ref.shape[0]
    t_len = k_ref.shape[0]
    nb = t_len // MB_BLOCK
    nt = (((1,), (1,)), ((), ()))
    log2e = 1.4426950408889634
    slope2 = slope_ref[h] * log2e

    @pl.when(own == 0)
    def _():
        kf = k_ref[...].astype(jnp.float32).reshape(nb, MB_BLOCK, HEAD_DIM)
        kmean = jnp.sum(kf, axis=1) * (1.0 / MB_BLOCK)
        kmean_hi = kmean.astype(jnp.bfloat16)
        kmean_lo = (kmean - kmean_hi.astype(jnp.float32)).astype(jnp.bfloat16)
        kmean_ref[...] = jnp.concatenate([kmean_hi, kmean_lo], axis=0)
        pos = lax.broadcasted_iota(jnp.int32, (MB_BLOCK, LANES), 0).astype(jnp.float32)
        col = lax.broadcasted_iota(jnp.int32, (MB_BLOCK, LANES), 1)
        q_hi, q_mid, q_lo = _split3(pos * (-slope2))
        k_hi, k_mid, k_lo = _split3(pos * slope2)
        one = jnp.where(col < 6, 1.0, 0.0)
        qaug = jnp.where(col == 0, q_hi, jnp.where(col == 1, q_mid, jnp.where(col == 2, q_lo, one)))
        kaug = jnp.where(col == 3, k_hi, jnp.where(col == 4, k_mid, jnp.where(col == 5, k_lo, one)))
        qaug_ref[...] = qaug.astype(qaug_ref.dtype)
        kaug_ref[...] = kaug.astype(kaug_ref.dtype)

    q = q_ref[...]
    qf = q.astype(jnp.float32)
    gate2 = lax.dot_general(kmean_ref[...], q, nt, preferred_element_type=jnp.float32)
    gate = gate2[:nb] + gate2[nb:]
    blk = lax.broadcasted_iota(jnp.int32, gate.shape, 0).astype(jnp.float32)
    gate = jnp.where(blk < own.astype(jnp.float32), gate, -jnp.inf)
    picks = jnp.zeros((1, tq), jnp.int32)
    for r in range(MB_TOPK):
        mx = jnp.max(gate, axis=0, keepdims=True)
        idx = jnp.min(jnp.where(gate == mx, blk, float(nb)), axis=0, keepdims=True)
        bit = jnp.left_shift(1, idx.astype(jnp.int32))
        picks = picks | jnp.where(r < own, bit, 0)
        gate = jnp.where(blk == idx, -jnp.inf, gate)

    q2 = jnp.concatenate([(qf * (HEAD_DIM ** -0.5 * log2e)).astype(jnp.bfloat16), qaug_ref[...]], axis=1)
    kaug = kaug_ref[...]

    def scores(n):
        start = pl.multiple_of(n * MB_BLOCK, MB_BLOCK)
        k2 = jnp.concatenate([k_ref[pl.ds(start, MB_BLOCK), :], kaug], axis=1)
        return lax.dot_general(k2, q2, nt, preferred_element_type=jnp.float32)

    def values_t(n):
        return vt_ref[:, pl.ds(pl.multiple_of(n * MB_BLOCK, MB_BLOCK), MB_BLOCK)]

    def past_scores(n):
        picked = (picks & jnp.left_shift(1, n)) != 0
        far = (own - n).astype(jnp.float32) * (-slope2 * MB_BLOCK)
        return scores(n) + jnp.where(picked, far, NEG_INF)

    def fold(op, x):
        parts = [x[r:r + 8] for r in range(0, x.shape[0], 8)]
        while len(parts) > 1:
            parts = [op(parts[a], parts[a + 1]) for a in range(0, len(parts) - 1, 2)] + parts[len(parts) & ~1:]
        return parts[0]

    def stage_pair(slot, i):
        s_ref[slot, :MB_BLOCK, :] = past_scores(jnp.minimum(2 * i, nb - 1))
        s_ref[slot, MB_BLOCK:, :] = past_scores(jnp.minimum(2 * i + 1, nb - 1))

    def pair_values_t(i):
        return vt_ref[:, pl.ds(pl.multiple_of(2 * i * MB_BLOCK, 2 * MB_BLOCK), 2 * MB_BLOCK)]

    def absorb(s, v_t):
        m = m_ref[...]
        m_new = jnp.maximum(m, jnp.max(fold(jnp.maximum, s), axis=0, keepdims=True))
        p = jnp.exp2(s - m_new)
        alpha = jnp.exp2(m - m_new)
        m_ref[...] = m_new
        l_ref[...] = alpha * l_ref[...] + fold(jnp.add, p)
        acc_ref[...] = alpha * acc_ref[...] + jnp.dot(v_t, p.astype(jnp.bfloat16),
                                                      preferred_element_type=jnp.float32)

    n_pairs = (own + 1) // 2

    def two_pairs(j, carry):
        i = 2 * j
        stage_pair(1, i + 1)
        absorb(s_ref[0], pair_values_t(i))

        @pl.when(i + 1 < n_pairs)
        def _():
            stage_pair(0, i + 2)
            absorb(s_ref[1], pair_values_t(i + 1))
        return carry

    m_ref[...] = jnp.full(m_ref.shape, NEG_INF, jnp.float32)
    l_ref[...] = jnp.zeros_like(l_ref)
    acc_ref[...] = jnp.zeros_like(acc_ref)
    stage_pair(0, 0)
    key_pos = lax.broadcasted_iota(jnp.int32, (MB_BLOCK, tq), 0)
    query_pos = lax.broadcasted_iota(jnp.int32, (MB_BLOCK, tq), 1)
    s_own_ref[...] = jnp.where(key_pos <= query_pos, scores(own), NEG_INF)
    lax.fori_loop(0, (n_pairs + 1) // 2, two_pairs, 0)
    absorb(s_own_ref[...], values_t(own))
    l = jnp.sum(l_ref[...], axis=0, keepdims=True)
    o_ref[...] = (acc_ref[...] / l).T.astype(o_ref.dtype)


def _moba_attention(proj, bsz, t_len, n_heads):
    tq = MB_BLOCK
    nq = t_len // tq
    nb = t_len // MB_BLOCK
    assert nb <= 32, "per-row picks are kept as one int32 bit mask"
    slopes = jnp.exp2(-8.0 * jnp.arange(1, n_heads + 1, dtype=jnp.float32) / n_heads)
    v_t = jnp.transpose(proj[:, 2 * n_heads * HEAD_DIM:])
    grid_spec = pltpu.PrefetchScalarGridSpec(
        num_scalar_prefetch=1,
        grid=(bsz, n_heads, nq),
        in_specs=[pl.BlockSpec((tq, HEAD_DIM), lambda b, h, i, s: (b * nq + i, h)),
                  pl.BlockSpec((t_len, HEAD_DIM), lambda b, h, i, s: (b, n_heads + h)),
                  pl.BlockSpec((HEAD_DIM, t_len), lambda b, h, i, s: (h, b))],
        out_specs=pl.BlockSpec((tq, HEAD_DIM), lambda b, h, i, s: (b * nq + i, h)),
        scratch_shapes=[pltpu.VMEM((2 * nb, HEAD_DIM), jnp.bfloat16),
                        pltpu.VMEM((MB_BLOCK, LANES), jnp.bfloat16),
                        pltpu.VMEM((MB_BLOCK, LANES), jnp.bfloat16),
                        pltpu.VMEM((2, 2 * MB_BLOCK, tq), jnp.float32),
                        pltpu.VMEM((MB_BLOCK, tq), jnp.float32),
                        pltpu.VMEM((1, tq), jnp.float32),
                        pltpu.VMEM((8, tq), jnp.float32),
                        pltpu.VMEM((HEAD_DIM, tq), jnp.float32)],
    )
    return pl.pallas_call(
        _moba_kernel,
        grid_spec=grid_spec,
        out_shape=jax.ShapeDtypeStruct((bsz * t_len, n_heads * HEAD_DIM), jnp.bfloat16),
        compiler_params=_cparams(("parallel", "parallel", "arbitrary")),
        name="moba_attention",
    )(slopes, proj, proj, v_t)


def _pack_bf16_pairs(x):
    half = x.shape[1] // 2
    bits = lax.bitcast_convert_type(x.astype(jnp.float32), jnp.uint32)
    return (bits[:, :half] >> 16) | bits[:, half:]


def _layer_norm_rows(y, g, b):
    mu = jnp.mean(y, axis=-1, keepdims=True)
    yc = y - mu
    var = jnp.mean(yc * yc, axis=-1, keepdims=True)
    return yc * lax.rsqrt(var + LN_EPS) * g + b


def _post_mixer_kernel(h_ref, w_ref, xres_ref, g_ref, b_ref, rw_ref, rb_ref,
                       x1_ref, x1b_ref, topi_ref, topg_ref, rank_ref, count_ref, *, alpha, n_experts):
    i = pl.program_id(0)
    j = pl.program_id(1)
    tm = h_ref.shape[0]
    tn = w_ref.shape[1]
    col = pl.multiple_of(j * tn, tn)
    x1_ref[:, pl.ds(col, tn)] = jnp.dot(h_ref[...], w_ref[...], preferred_element_type=jnp.float32)

    @pl.when((i == 0) & (j == 0))
    def _():
        count_ref[...] = jnp.zeros_like(count_ref)

    @pl.when(j == pl.num_programs(1) - 1)
    def _():
        x1 = _layer_norm_rows(alpha * xres_ref[...] + x1_ref[...], g_ref[...], b_ref[...])
        x1_ref[...] = x1
        x1_hi = x1.astype(jnp.bfloat16)
        packed = _pack_bf16_pairs(x1_hi)
        chunks = packed.shape[1] // LANES
        for c in range(chunks):
            x1b_ref[pl.ds(c, tm, stride=chunks), :] = packed[:, c * LANES:(c + 1) * LANES]
        x1_lo = (x1 - x1_hi.astype(jnp.float32)).astype(jnp.bfloat16)
        hi_terms = jnp.dot(x1_hi, rw_ref[...], preferred_element_type=jnp.float32)
        lo_term = jnp.dot(x1_lo, rw_ref[:, :LANES], preferred_element_type=jnp.float32)
        logits = hi_terms[:, :LANES] + hi_terms[:, LANES:] + lo_term + rb_ref[...]
        lane_i = lax.broadcasted_iota(jnp.int32, logits.shape, 1)
        lane = lane_i.astype(jnp.float32)
        logits = jnp.where(lane_i < n_experts, logits, -jnp.inf)
        topi = jnp.zeros(logits.shape, jnp.float32)
        topv = jnp.full(logits.shape, -jnp.inf, jnp.float32)
        chosen = []
        for r in range(TOP_K):
            mx = jnp.max(logits, axis=-1, keepdims=True)
            idx = jnp.min(jnp.where(logits == mx, lane, float(LANES)), axis=-1, keepdims=True)
            hit = lane == idx
            chosen.append(hit)
            topi = jnp.where(lane_i == r, idx, topi)
            topv = jnp.where(lane_i == r, mx, topv)
            logits = jnp.where(hit, -jnp.inf, logits)
        e = jnp.exp(topv - jnp.max(topv, axis=-1, keepdims=True))
        topi_ref[...] = topi.astype(jnp.int32)
        topg_ref[...] = e / jnp.sum(e, axis=-1, keepdims=True)
        onehot = jnp.where(functools.reduce(jnp.logical_or, chosen), 1.0, 0.0)
        earlier = jnp.where(lax.broadcasted_iota(jnp.int32, (tm, tm), 1) < lax.broadcasted_iota(jnp.int32, (tm, tm), 0),
                            1.0, 0.0).astype(jnp.bfloat16)
        before = count_ref[...] + jnp.dot(earlier, onehot.astype(jnp.bfloat16), preferred_element_type=jnp.float32)
        rank = jnp.zeros(logits.shape, jnp.float32)
        for r in range(TOP_K):
            rank = jnp.where(lane_i == r, jnp.sum(jnp.where(chosen[r], before, 0.0), axis=-1, keepdims=True), rank)
        rank_ref[...] = rank.astype(jnp.int32)
        count_ref[...] += jnp.sum(onehot, axis=0, keepdims=True)


def _post_mixer(h, w_out, xres, ln_g, ln_b, router_w, router_b, alpha):
    n, kdim = h.shape
    d = w_out.shape[1]
    n_experts = router_w.shape[1]
    tm, tn = min(256, n), min(512, d)
    rw = jnp.pad(router_w, ((0, 0), (0, LANES - n_experts)))
    rw_hi = rw.astype(jnp.bfloat16)
    rw_lo = (rw - rw_hi.astype(jnp.float32)).astype(jnp.bfloat16)
    rw = jnp.concatenate([rw_hi, rw_lo], axis=1)
    rb = jnp.pad(router_b, (0, LANES - n_experts))[None, :]
    row = lambda i, j: (i, 0)
    fixed = lambda i, j: (0, 0)
    return pl.pallas_call(
        functools.partial(_post_mixer_kernel, alpha=alpha, n_experts=n_experts),
        grid=(n // tm, d // tn),
        in_specs=[pl.BlockSpec((tm, kdim), row),
                  pl.BlockSpec((kdim, tn), lambda i, j: (0, j)),
                  pl.BlockSpec((tm, d), row),
                  pl.BlockSpec((1, d), fixed),
                  pl.BlockSpec((1, d), fixed),
                  pl.BlockSpec((d, 2 * LANES), fixed),
                  pl.BlockSpec((1, LANES), fixed)],
        out_specs=[pl.BlockSpec((tm, d), row),
                   pl.BlockSpec((tm * (d // 2 // LANES), LANES), row),
                   pl.BlockSpec((tm, LANES), row),
                   pl.BlockSpec((tm, LANES), row),
                   pl.BlockSpec((tm, LANES), row),
                   pl.BlockSpec((1, LANES), fixed)],
        out_shape=[jax.ShapeDtypeStruct((n, d), jnp.float32),
                   jax.ShapeDtypeStruct((n * (d // 2 // LANES), LANES), jnp.uint32),
                   jax.ShapeDtypeStruct((n, LANES), jnp.int32),
                   jax.ShapeDtypeStruct((n, LANES), jnp.float32),
                   jax.ShapeDtypeStruct((n, LANES), jnp.int32),
                   jax.ShapeDtypeStruct((1, LANES), jnp.float32)],
        compiler_params=_cparams(("arbitrary", "arbitrary")),
        name="post_mixer",
    )(h, w_out, xres, ln_g[None, :], ln_b[None, :], rw, rb)


def _moe_kernel(ge_ref, meta_ref, odd_ref, next_odd_ref, rows_ref, x_hbm, wg_ref, wu_ref, wd_ref, bg_ref, bu_ref,
                bd_ref, y_ref, hid_ref, xbuf_ref, xs_ref, sem, *, nf):
    g = pl.program_id(0)
    s = pl.program_id(1)
    t = pl.program_id(2)
    n_groups = meta_ref[0]
    first_odd = meta_ref[1]
    valid = (g < n_groups) & ((t == 0) | (odd_ref[g] != 0))
    tm = xs_ref.shape[1]
    tf = wg_ref.shape[1]
    bf16 = jnp.bfloat16

    chunks = xs_ref.shape[2] // (2 * LANES)
    pitch = chunks + 1

    def gather_copy(tile, r, slot):
        token = rows_ref[tile * tm + r]
        return pltpu.make_async_copy(x_hbm.at[pl.ds(token * chunks, chunks)],
                                     xbuf_ref.at[slot, pl.ds(r * pitch, chunks)], sem.at[slot])

    def start_gather(tile, slot):
        def issue(r, carry):
            gather_copy(tile, r, slot).start()
            return carry
        lax.fori_loop(0, tm, issue, 0, unroll=8)

    def wait_gather(slot):
        done = xbuf_ref.at[slot, pl.ds(0, tm * chunks)]
        pltpu.make_async_copy(done, done, sem.at[slot]).wait()

    def unpack_rows(slot):
        half = chunks * LANES
        for c in range(chunks):
            p = xbuf_ref[slot, pl.ds(c, tm, stride=pitch), :]
            lo = lax.bitcast_convert_type(p << 16, jnp.float32)
            hi = lax.bitcast_convert_type(p & jnp.uint32(0xFFFF0000), jnp.float32)
            xs_ref[slot, :, c * LANES:(c + 1) * LANES] = lo.astype(bf16)
            xs_ref[slot, :, half + c * LANES:half + (c + 1) * LANES] = hi.astype(bf16)

    @pl.when(valid & (s == 0))
    def _():
        @pl.when((g == 0) & (t == 0))
        def _():
            start_gather(0, 0)

            @pl.when(first_odd >= 0)
            def _():
                start_gather(2 * first_odd + 1, 1)

        wait_gather(t)
        unpack_rows(t)

        @pl.when((t == 0) & (g + 1 < n_groups))
        def _():
            start_gather(2 * (g + 1), 0)

        @pl.when((t == 1) & (next_odd_ref[g] >= 0))
        def _():
            start_gather(2 * next_odd_ref[g] + 1, 1)

    @pl.when(valid & (s < nf))
    def _():
        xs = xs_ref[t]
        h_gate = jnp.dot(xs, wg_ref[...].astype(bf16), preferred_element_type=jnp.float32) + bg_ref[...]
        h_up = jnp.dot(xs, wu_ref[...].astype(bf16), preferred_element_type=jnp.float32) + bu_ref[...]
        h_gate = jnp.minimum(h_gate, SWIGLU_LIMIT)
        h_up = jnp.clip(h_up, -SWIGLU_LIMIT, SWIGLU_LIMIT)
        hid = (h_up + 1.0) * (h_gate * _sigmoid(SWIGLU_ALPHA * h_gate))
        hid_ref[t, :, pl.ds(pl.multiple_of(s * tf, tf), tf)] = hid.astype(bf16)

    @pl.when(valid & (s >= nf))
    def _():
        y = jnp.dot(hid_ref[t], wd_ref[...].astype(bf16), preferred_element_type=jnp.float32) + bd_ref[...]
        y_ref[...] = y.astype(y_ref.dtype)

    @pl.when(jnp.logical_not(valid) & (s >= nf))
    def _():
        y_ref[...] = jnp.zeros_like(y_ref)


def _moe_experts(x_packed, rows, group_e, meta, odd, next_odd, layer, w_gate, w_up, w_down, b_gate, b_up, b_down, tm):
    n_slots = rows.shape[0]
    _, n_experts, d, f = w_gate.shape
    tf, td = min(256, f), min(2048, d)
    nf, nd = f // tf, d // td
    n_groups = n_slots // (MOE_GROUP * tm)

    def f_tile(g, s, meta):
        return jnp.where(g < meta[0], jnp.minimum(s, nf - 1), nf - 1)

    def d_tile(g, s, meta):
        return jnp.where(g < meta[0], jnp.clip(s - nf, 0, nd - 1), nd - 1)

    def w_spec(shape, tile):
        return pl.BlockSpec((None, None) + shape,
                            lambda g, s, t, ge, meta, odd, nxt, rw: (layer, ge[g], 0, tile(g, s, meta)))

    def y_index(g, s, t, ge, meta, odd, nxt, rw):
        return (MOE_GROUP * g + jnp.where(s >= nf, t, 0), jnp.clip(s - nf, 0, nd - 1))

    grid_spec = pltpu.PrefetchScalarGridSpec(
        num_scalar_prefetch=5,
        grid=(n_groups, nf + nd, MOE_GROUP),
        in_specs=[pl.BlockSpec(memory_space=pl.ANY),
                  w_spec((d, tf), f_tile), w_spec((d, tf), f_tile), w_spec((f, td), d_tile),
                  w_spec((1, tf), f_tile), w_spec((1, tf), f_tile), w_spec((1, td), d_tile)],
        out_specs=pl.BlockSpec((tm, td), y_index),
        scratch_shapes=[pltpu.VMEM((MOE_GROUP, tm, f), jnp.bfloat16),
                        pltpu.VMEM((MOE_GROUP, tm * (d // 2 // LANES + 1), LANES), jnp.uint32),
                        pltpu.VMEM((MOE_GROUP, tm, d), jnp.bfloat16),
                        pltpu.SemaphoreType.DMA((MOE_GROUP,))],
    )
    return pl.pallas_call(
        functools.partial(_moe_kernel, nf=nf),
        grid_spec=grid_spec,
        out_shape=jax.ShapeDtypeStruct((n_slots, d), jnp.bfloat16),
        compiler_params=_cparams(("arbitrary", "arbitrary", "arbitrary")),
        name="moe_experts",
    )(group_e, meta, odd, next_odd, rows, x_packed, w_gate, w_up, w_down,
      b_gate[:, :, None, :], b_up[:, :, None, :], b_down[:, :, None, :])


def _combine_kernel(yg_ref, gate_ref, x1_ref, g_ref, b_ref, o_ref, ob_ref, *, alpha):
    gates = gate_ref[...]
    f = yg_ref[0].astype(jnp.float32) * gates[:, 0:1]
    for r in range(1, TOP_K):
        f = f + yg_ref[r].astype(jnp.float32) * gates[:, r:r + 1]
    out = _layer_norm_rows(alpha * x1_ref[...] + f, g_ref[...], b_ref[...])
    o_ref[...] = out
    ob_ref[...] = out.astype(ob_ref.dtype)


def _moe_combine(yg, gates, x1, ln_g, ln_b, alpha):
    n, d = x1.shape
    tm = min(256, n)
    return pl.pallas_call(
        functools.partial(_combine_kernel, alpha=alpha),
        grid=(n // tm,),
        in_specs=[pl.BlockSpec((TOP_K, tm, d), lambda i: (0, i, 0)),
                  pl.BlockSpec((tm, LANES), lambda i: (i, 0)),
                  pl.BlockSpec((tm, d), lambda i: (i, 0)),
                  pl.BlockSpec((1, d), lambda i: (0, 0)),
                  pl.BlockSpec((1, d), lambda i: (0, 0))],
        out_specs=[pl.BlockSpec((tm, d), lambda i: (i, 0)),
                   pl.BlockSpec((tm, d), lambda i: (i, 0))],
        out_shape=[jax.ShapeDtypeStruct((n, d), jnp.float32),
                   jax.ShapeDtypeStruct((n, d), jnp.bfloat16)],
        compiler_params=_cparams(("parallel",)),
        name="moe_combine",
    )(yg, gates, x1, ln_g[None, :], ln_b[None, :])


def _moe_layer(x1, x1b, topi, topg, rank, counts, ln_g, ln_b, layer, w_gate, b_gate, w_up, b_up, w_down, b_down,
               alpha):
    n_tok, d = x1.shape
    n_experts = w_gate.shape[1]
    n_assign = n_tok * TOP_K
    tm = min(512, n_assign // n_experts)
    flat_e = topi[:, :TOP_K].reshape(-1)
    rank = rank[:, :TOP_K].reshape(-1)
    counts = counts[0, :n_experts].astype(jnp.int32)
    gm = MOE_GROUP * tm
    padded = (counts + gm - 1) // gm * gm
    pends = jnp.cumsum(padded)
    pstarts = pends - padded
    dest = pstarts[flat_e] + rank
    n_groups = n_assign // gm + n_experts
    flat_tok = jnp.arange(n_assign, dtype=jnp.int32) // TOP_K
    rows = jnp.zeros((n_groups * gm,), jnp.int32).at[dest].set(flat_tok, mode="promise_in_bounds",
                                                               unique_indices=True)
    gidx = jnp.arange(n_groups, dtype=jnp.int32)
    group_e = jnp.minimum(jnp.searchsorted(pends, gidx * gm, side='right'), n_experts - 1).astype(jnp.int32)
    n_used = pends[-1] // gm
    odd = ((gidx < n_used) & (gidx * gm + tm < (pstarts + counts)[group_e])).astype(jnp.int32)
    odd_pos = jnp.where(odd != 0, gidx, n_groups)
    later = jnp.flip(lax.cummin(jnp.flip(odd_pos)))
    first_odd = jnp.where(later[0] < n_groups, later[0], -1)
    nxt = jnp.concatenate([later[1:], jnp.full((1,), n_groups, jnp.int32)])
    next_odd = jnp.where(nxt < n_groups, nxt, -1).astype(jnp.int32)
    meta = jnp.stack([n_used, first_odd]).astype(jnp.int32)
    y = _moe_experts(x1b, rows, group_e, meta, odd, next_odd, layer, w_gate, w_up, w_down, b_gate, b_up, b_down, tm)
    yg = y.at[dest.reshape(n_tok, TOP_K).T].get(mode="promise_in_bounds")
    return _moe_combine(yg, topg, x1, ln_g, ln_b, alpha)


def kernel(x, dn_w_in, dn_conv_w, dn_a_log, dn_dt_bias, dn_norm_w, dn_w_out, mb_w_in, mb_w_out, ln_g, ln_b,
           router_w, router_b, w_gate, b_gate, w_up, b_up, w_down, b_down):
    bsz, t_len, d = x.shape
    depth = ln_g.shape[0]
    alpha = float((2 * depth) ** 0.25)
    n = bsz * t_len
    xf = x.reshape(n, d)
    xb = xf.astype(jnp.bfloat16)
    for i in range(depth):
        j = i // 2
        if i % 2 == 0:
            n_vheads = d // HEAD_DIM
            main = dn_w_in.shape[2] - 2 * n_vheads
            proj = _matmul(xb, dn_w_in[j].astype(jnp.bfloat16), main, jnp.bfloat16, 1024, 512)
            gates = _dn_gates(xf, dn_w_in[j][:, main:], dn_a_log[j], dn_dt_bias[j], n_vheads)
            gates_t = jnp.transpose(gates.reshape(bsz, t_len, LANES), (0, 2, 1))
            h = _dn_delta_rule(proj, gates, gates_t, dn_conv_w[j], dn_norm_w[j], bsz, t_len, d)
            w_out = dn_w_out[j]
        else:
            proj = _matmul(xb, mb_w_in[j].astype(jnp.bfloat16), mb_w_in.shape[2], jnp.bfloat16, 1024, 512)
            h = _moba_attention(proj, bsz, t_len, d // HEAD_DIM)
            w_out = mb_w_out[j]
        x1, x1b, topi, topg, rank, counts = _post_mixer(h, w_out.astype(jnp.bfloat16), xf, ln_g[i, 0], ln_b[i, 0],
                                                        router_w[i], router_b[i], alpha)
        xf, xb = _moe_layer(x1, x1b, topi, topg, rank, counts, ln_g[i, 1], ln_b[i, 1], i, w_gate, b_gate,
                            w_up, b_up, w_down, b_down, alpha)
    return xf.reshape(bsz, t_len, d)
```

```python
import functools

import jax
import jax.numpy as jnp
import numpy as np
from jax import lax
from jax.experimental import pallas as pl
from jax.experimental.pallas import tpu as pltpu

HEAD_DIM = 128
DN_CONV = 4
DN_CHUNK = 256
DN_GROUP = 4
MOE_GROUP = 2
MB_BLOCK = 256
MB_TOPK = 3
TOP_K = 4
SWIGLU_LIMIT = 7.0
SWIGLU_ALPHA = 1.702
LN_EPS = 1e-5
NORM_EPS = 1e-6
NEG_INF = -1e30
LANES = 128
HALO = 16
VMEM_LIMIT = 56 * 1024 * 1024

_HI = lax.Precision.HIGHEST


def _cparams(sem):
    return pltpu.CompilerParams(dimension_semantics=sem, vmem_limit_bytes=VMEM_LIMIT)


def _sigmoid(v):
    return 1.0 / (1.0 + jnp.exp(-v))


def _silu(v):
    return v * _sigmoid(v)


def _mm_kernel(x_ref, w_ref, o_ref):
    o_ref[...] = jnp.dot(x_ref[...], w_ref[...], preferred_element_type=jnp.float32).astype(o_ref.dtype)


def _matmul(x, w, n, out_dtype, tm, tn):
    m, k = x.shape
    tm, tn = min(tm, m), min(tn, n)
    assert n % tn == 0 and m % tm == 0
    return pl.pallas_call(
        _mm_kernel,
        grid=(m // tm, n // tn),
        in_specs=[pl.BlockSpec((tm, k), lambda i, j: (i, 0)),
                  pl.BlockSpec((k, tn), lambda i, j: (0, j))],
        out_specs=pl.BlockSpec((tm, tn), lambda i, j: (i, j)),
        out_shape=jax.ShapeDtypeStruct((m, n), out_dtype),
        compiler_params=_cparams(("parallel", "arbitrary")),
        name="dense_matmul",
    )(x, w)


def _dn_gates_kernel(x_ref, w_ref, alog_ref, dtb_ref, o_ref, *, n_heads):
    tm = x_ref.shape[0]
    ba = jnp.dot(x_ref[...], w_ref[...], precision=_HI, preferred_element_type=jnp.float32)
    lane = lax.broadcasted_iota(jnp.int32, ba.shape, 1)
    beta = _sigmoid(ba)
    v = ba + dtb_ref[...]
    softplus = jnp.maximum(v, 0.0) + jnp.log(1.0 + jnp.exp(-jnp.abs(v)))
    g = -jnp.exp(alog_ref[...]) * softplus
    g = jnp.where((lane >= n_heads) & (lane < 2 * n_heads), g, 0.0)
    ri = lax.broadcasted_iota(jnp.int32, (tm, tm), 0)
    ci = lax.broadcasted_iota(jnp.int32, (tm, tm), 1)
    shift = DN_CHUNK.bit_length() - 1
    tril = jnp.where((ci <= ri) & ((ri >> shift) == (ci >> shift)), 1.0, 0.0)
    gc = jnp.dot(tril, g, precision=_HI, preferred_element_type=jnp.float32)
    o_ref[...] = jnp.where(lane < n_heads, beta, gc)


def _dn_gates(x2d, w_ba, a_log, dt_bias, n_heads):
    n, d = x2d.shape
    tm = min(256, n)
    pad = LANES - 2 * n_heads
    w_p = jnp.pad(w_ba, ((0, 0), (0, pad)))
    alog_p = jnp.pad(a_log, (n_heads, pad))[None, :]
    dtb_p = jnp.pad(dt_bias, (n_heads, pad))[None, :]
    return pl.pallas_call(
        functools.partial(_dn_gates_kernel, n_heads=n_heads),
        grid=(n // tm,),
        in_specs=[pl.BlockSpec((tm, d), lambda i: (i, 0)),
                  pl.BlockSpec((d, LANES), lambda i: (0, 0)),
                  pl.BlockSpec((1, LANES), lambda i: (0, 0)),
                  pl.BlockSpec((1, LANES), lambda i: (0, 0))],
        out_specs=pl.BlockSpec((tm, LANES), lambda i: (i, 0)),
        out_shape=jax.ShapeDtypeStruct((n, LANES), jnp.float32),
        compiler_params=_cparams(("parallel",)),
        name="dn_gates",
    )(x2d, w_p, alog_p, dtb_p)


def _causal_conv_silu(x_ref, halo_ref, w_ref, first):
    halo = jnp.where(first, 0.0, halo_ref[...].astype(jnp.float32))
    xcat = jnp.concatenate([halo, x_ref[...].astype(jnp.float32)], axis=0)
    w = w_ref[...]
    acc = xcat[HALO:] * w[DN_CONV - 1:DN_CONV, :]
    for s in range(1, DN_CONV):
        acc = acc + pltpu.roll(xcat, s, 0)[HALO:] * w[DN_CONV - 1 - s:DN_CONV - s, :]
    return _silu(acc)


def _l2norm(v):
    return v * lax.rsqrt(jnp.sum(v * v, axis=-1, keepdims=True) + NORM_EPS)


def _inverse_level_masks(c):
    ri = np.arange(c)[:, None]
    ci = np.arange(c)[None, :]
    masks = [((ri >> 1) == (ci >> 1)) & (ri > ci)]
    s = 2
    while s < c:
        masks.append(((ri // (2 * s)) == (ci // (2 * s))) & ((ri & s) != 0) & ((ci & s) == 0))
        s *= 2
    return jnp.asarray(np.stack(masks), jnp.bfloat16)


def _dn_kernel(q_ref, k_ref, v_ref, z_ref, qh_ref, kh_ref, vh_ref, g_ref, gt_ref,
               wq_ref, wk_ref, wv_ref, nw_ref, lm_ref, negc_ref, o_ref, state_ref, *, n_vheads):
    hg = pl.program_id(1)
    t = pl.program_id(2)
    first = t == 0
    tb = q_ref.shape[0]
    group = q_ref.shape[1] // HEAD_DIM
    rep = v_ref.shape[1] // q_ref.shape[1]
    bf16, f32 = jnp.bfloat16, jnp.float32

    @pl.when(first)
    def _():
        state_ref[...] = jnp.zeros_like(state_ref)

    def head(x, c):
        return x[:, c * HEAD_DIM:(c + 1) * HEAD_DIM]

    q_all = _causal_conv_silu(q_ref, qh_ref, wq_ref, first)
    k_all = _causal_conv_silu(k_ref, kh_ref, wk_ref, first)
    v = _causal_conv_silu(v_ref, vh_ref, wv_ref, first)
    q = [_l2norm(head(q_all, g)) * (HEAD_DIM ** -0.5) for g in range(group)]
    k = [_l2norm(head(k_all, g)) for g in range(group)]
    gates = g_ref[...]
    lane = lax.broadcasted_iota(jnp.int32, gates.shape, 1)
    eye = jnp.where(lax.broadcasted_iota(jnp.int32, (tb, tb), 0) == lax.broadcasted_iota(jnp.int32, (tb, tb), 1),
                    1.0, 0.0)
    nt = (((1,), (1,)), ((), ()))
    k_b = [k[g].astype(bf16) for g in range(group)]
    kk = [lax.dot_general(k_b[g], k_b[g], nt, preferred_element_type=f32) for g in range(group)]
    qk = [lax.dot_general(q[g].astype(bf16), k_b[g], nt, preferred_element_type=f32) for g in range(group)]

    heads = range(group * rep)
    beta, gcol, g_last, decay, a_b, inv = [], [], [], [], [], []
    for c in heads:
        hv = hg * group * rep + c
        beta.append(jnp.sum(jnp.where(lane == hv, gates, 0.0), axis=-1, keepdims=True))
        gcol.append(jnp.sum(jnp.where(lane == n_vheads + hv, gates, 0.0), axis=-1, keepdims=True))
        grow = gt_ref[pl.ds(n_vheads + hv, 1), :]
        g_last.append(grow[:, tb - 1:tb])
        decay.append(jnp.exp(gcol[c] - grow + negc_ref[...]))
        a_b.append((kk[c // rep] * beta[c] * decay[c]).astype(bf16))
        inv.append(eye - (a_b[c] * lm_ref[0]).astype(f32))
    for lvl in range(1, lm_ref.shape[0]):
        tmp = [jnp.dot(a_b[c] * lm_ref[lvl], inv[c].astype(bf16), preferred_element_type=f32) for c in heads]
        inv = [inv[c] - jnp.dot(inv[c].astype(bf16), tmp[c].astype(bf16), preferred_element_type=f32)
               for c in heads]
    eg = [jnp.exp(gcol[c]) for c in heads]
    uw = [jnp.dot(inv[c].astype(bf16),
                  jnp.concatenate([head(v, c) * beta[c], k[c // rep] * (beta[c] * eg[c])], axis=1).astype(bf16),
                  preferred_element_type=f32) for c in heads]
    state = [state_ref[c] for c in heads]
    ws = [jnp.dot(jnp.concatenate([uw[c][:, HEAD_DIM:], q[c // rep] * eg[c]], axis=0).astype(bf16),
                  state[c].astype(bf16), preferred_element_type=f32) for c in heads]
    v_new_b = [(uw[c][:, :HEAD_DIM] - ws[c][:tb]).astype(bf16) for c in heads]
    outs = []
    for c in heads:
        o = ws[c][tb:] + jnp.dot((qk[c // rep] * decay[c]).astype(bf16), v_new_b[c], preferred_element_type=f32)
        k_dec_t = (k[c // rep] * jnp.exp(g_last[c] - gcol[c])).T.astype(bf16)
        state_ref[c] = state[c] * jnp.exp(g_last[c]) + jnp.dot(k_dec_t, v_new_b[c], preferred_element_type=f32)
        zc = head(z_ref[...], c).astype(f32)
        outs.append(o * lax.rsqrt(jnp.mean(o * o, axis=-1, keepdims=True) + NORM_EPS) * nw_ref[...] * _silu(zc))
    o_ref[...] = jnp.concatenate(outs, axis=1).astype(o_ref.dtype)


def _dn_delta_rule(proj, gates, gates_t, conv_w, norm_w, bsz, t_len, d_model):
    n_kheads = d_model // (2 * HEAD_DIM)
    n_vheads = d_model // HEAD_DIM
    group = min(DN_GROUP, n_kheads)
    kw = group * HEAD_DIM
    vw = kw * (n_vheads // n_kheads)
    tb = DN_CHUNK
    nt = t_len // tb
    hb = tb // HALO
    level_masks = _inverse_level_masks(tb)
    ri = np.arange(tb)
    neg_causal = jnp.asarray(np.where(ri[None, :] <= ri[:, None], 0.0, NEG_INF), jnp.float32)
    k_off = n_kheads * HEAD_DIM // kw
    v_off = 2 * n_kheads * HEAD_DIM // vw
    z_off = v_off + n_vheads * HEAD_DIM // vw

    def row(b, h, t):
        return b * nt + t

    def halo_row(b, h, t):
        return jnp.maximum((b * nt + t) * hb - 1, 0)

    in_specs = [
        pl.BlockSpec((tb, kw), lambda b, h, t: (row(b, h, t), h)),
        pl.BlockSpec((tb, kw), lambda b, h, t: (row(b, h, t), k_off + h)),
        pl.BlockSpec((tb, vw), lambda b, h, t: (row(b, h, t), v_off + h)),
        pl.BlockSpec((tb, vw), lambda b, h, t: (row(b, h, t), z_off + h)),
        pl.BlockSpec((HALO, kw), lambda b, h, t: (halo_row(b, h, t), h)),
        pl.BlockSpec((HALO, kw), lambda b, h, t: (halo_row(b, h, t), k_off + h)),
        pl.BlockSpec((HALO, vw), lambda b, h, t: (halo_row(b, h, t), v_off + h)),
        pl.BlockSpec((tb, LANES), lambda b, h, t: (row(b, h, t), 0)),
        pl.BlockSpec((None, LANES, tb), lambda b, h, t: (b, 0, t)),
        pl.BlockSpec((DN_CONV, kw), lambda b, h, t: (0, h)),
        pl.BlockSpec((DN_CONV, kw), lambda b, h, t: (0, k_off + h)),
        pl.BlockSpec((DN_CONV, vw), lambda b, h, t: (0, v_off + h)),
        pl.BlockSpec((1, HEAD_DIM), lambda b, h, t: (0, 0)),
        pl.BlockSpec(level_masks.shape, lambda b, h, t: (0, 0, 0)),
        pl.BlockSpec((tb, tb), lambda b, h, t: (0, 0)),
    ]
    return pl.pallas_call(
        functools.partial(_dn_kernel, n_vheads=n_vheads),
        grid=(bsz, n_kheads // group, nt),
        in_specs=in_specs,
        out_specs=pl.BlockSpec((tb, vw), lambda b, h, t: (row(b, h, t), h)),
        out_shape=jax.ShapeDtypeStruct((bsz * t_len, n_vheads * HEAD_DIM), jnp.bfloat16),
        scratch_shapes=[pltpu.VMEM((vw // HEAD_DIM, HEAD_DIM, HEAD_DIM), jnp.float32)],
        compiler_params=_cparams(("parallel", "parallel", "arbitrary")),
        name="dn_delta_rule",
    )(proj, proj, proj, proj, proj, proj, proj, gates, gates_t, conv_w, conv_w, conv_w, norm_w[None, :],
      level_masks, neg_causal)


def _split3(a):
    hi = a.astype(jnp.bfloat16).astype(jnp.float32)
    mid = (a - hi).astype(jnp.bfloat16).astype(jnp.float32)
    lo = (a - hi - mid).astype(jnp.bfloat16).astype(jnp.float32)
    return hi, mid, lo


def _moba_kernel(slope_ref, q_ref, k_ref, vt_ref, o_ref, kmean_ref, qaug_ref, kaug_ref, s_ref, s_own_ref, m_ref, l_ref,
                 acc_ref):
    h = pl.program_id(1)
    own = pl.program_id(2)
    tq = q_ref.shape[0]
    t_len = k_ref.shape[0]
    nb = t_len // MB_BLOCK
    nt = (((1,), (1,)), ((), ()))
    log2e = 1.4426950408889634
    slope2 = slope_ref[h] * log2e

    @pl.when(own == 0)
    def _():
        kf = k_ref[...].astype(jnp.float32).reshape(nb, MB_BLOCK, HEAD_DIM)
        kmean = jnp.sum(kf, axis=1) * (1.0 / MB_BLOCK)
        kmean_hi = kmean.astype(jnp.bfloat16)
        kmean_lo = (kmean - kmean_hi.astype(jnp.float32)).astype(jnp.bfloat16)
        kmean_ref[...] = jnp.concatenate([kmean_hi, kmean_lo], axis=0)
        pos = lax.broadcasted_iota(jnp.int32, (MB_BLOCK, LANES), 0).astype(jnp.float32)
        col = lax.broadcasted_iota(jnp.int32, (MB_BLOCK, LANES), 1)
        q_hi, q_mid, q_lo = _split3(pos * (-slope2))
        k_hi, k_mid, k_lo = _split3(pos * slope2)
        one = jnp.where(col < 6, 1.0, 0.0)
        qaug = jnp.where(col == 0, q_hi, jnp.where(col == 1, q_mid, jnp.where(col == 2, q_lo, one)))
        kaug = jnp.where(col == 3, k_hi, jnp.where(col == 4, k_mid, jnp.where(col == 5, k_lo, one)))
        qaug_ref[...] = qaug.astype(qaug_ref.dtype)
        kaug_ref[...] = kaug.astype(kaug_ref.dtype)

    q = q_ref[...]
    qf = q.astype(jnp.float32)
    gate2 = lax.dot_general(kmean_ref[...], q, nt, preferred_element_type=jnp.float32)
    gate = gate2[:nb] + gate2[nb:]
    blk = lax.broadcasted_iota(jnp.int32, gate.shape, 0).astype(jnp.float32)
    gate = jnp.where(blk < own.astype(jnp.float32), gate, -jnp.inf)
    picks = jnp.zeros((1, tq), jnp.int32)
    for r in range(MB_TOPK):
        mx = jnp.max(gate, axis=0, keepdims=True)
        idx = jnp.min(jnp.where(gate == mx, blk, float(nb)), axis=0, keepdims=True)
        bit = jnp.left_shift(1, idx.astype(jnp.int32))
        picks = picks | jnp.where(r < own, bit, 0)
        gate = jnp.where(blk == idx, -jnp.inf, gate)

    q2 = jnp.concatenate([(qf * (HEAD_DIM ** -0.5 * log2e)).astype(jnp.bfloat16), qaug_ref[...]], axis=1)
    kaug = kaug_ref[...]

    def scores(n):
        start = pl.multiple_of(n * MB_BLOCK, MB_BLOCK)
        k2 = jnp.concatenate([k_ref[pl.ds(start, MB_BLOCK), :], kaug], axis=1)
        return lax.dot_general(k2, q2, nt, preferred_element_type=jnp.float32)

    def values_t(n):
        return vt_ref[:, pl.ds(pl.multiple_of(n * MB_BLOCK, MB_BLOCK), MB_BLOCK)]

    def past_scores(n):
        picked = (picks & jnp.left_shift(1, n)) != 0
        far = (own - n).astype(jnp.float32) * (-slope2 * MB_BLOCK)
        return scores(n) + jnp.where(picked, far, NEG_INF)

    def fold(op, x):
        parts = [x[r:r + 8] for r in range(0, x.shape[0], 8)]
        while len(parts) > 1:
            parts = [op(parts[a], parts[a + 1]) for a in range(0, len(parts) - 1, 2)] + parts[len(parts) & ~1:]
        return parts[0]

    def stage_pair(slot, i):
        s_ref[slot, :MB_BLOCK, :] = past_scores(jnp.minimum(2 * i, nb - 1))
        s_ref[slot, MB_BLOCK:, :] = past_scores(jnp.minimum(2 * i + 1, nb - 1))

    def pair_values_t(i):
        return vt_ref[:, pl.ds(pl.multiple_of(2 * i * MB_BLOCK, 2 * MB_BLOCK), 2 * MB_BLOCK)]

    def absorb(s, v_t):
        m = m_ref[...]
        m_new = jnp.maximum(m, jnp.max(fold(jnp.maximum, s), axis=0, keepdims=True))
        p = jnp.exp2(s - m_new)
        alpha = jnp.exp2(m - m_new)
        m_ref[...] = m_new
        l_ref[...] = alpha * l_ref[...] + fold(jnp.add, p)
        acc_ref[...] = alpha * acc_ref[...] + jnp.dot(v_t, p.astype(jnp.bfloat16),
                                                      preferred_element_type=jnp.float32)

    n_pairs = (own + 1) // 2

    def two_pairs(j, carry):
        i = 2 * j
        stage_pair(1, i + 1)
        absorb(s_ref[0], pair_values_t(i))

        @pl.when(i + 1 < n_pairs)
        def _():
            stage_pair(0, i + 2)
            absorb(s_ref[1], pair_values_t(i + 1))
        return carry

    m_ref[...] = jnp.full(m_ref.shape, NEG_INF, jnp.float32)
    l_ref[...] = jnp.zeros_like(l_ref)
    acc_ref[...] = jnp.zeros_like(acc_ref)
    stage_pair(0, 0)
    key_pos = lax.broadcasted_iota(jnp.int32, (MB_BLOCK, tq), 0)
    query_pos = lax.broadcasted_iota(jnp.int32, (MB_BLOCK, tq), 1)
    s_own_ref[...] = jnp.where(key_pos <= query_pos, scores(own), NEG_INF)
    lax.fori_loop(0, (n_pairs + 1) // 2, two_pairs, 0)
    absorb(s_own_ref[...], values_t(own))
    l = jnp.sum(l_ref[...], axis=0, keepdims=True)
    o_ref[...] = (acc_ref[...] / l).T.astype(o_ref.dtype)


def _moba_attention(proj, bsz, t_len, n_heads):
    tq = MB_BLOCK
    nq = t_len // tq
    nb = t_len // MB_BLOCK
    assert nb <= 32, "per-row picks are kept as one int32 bit mask"
    slopes = jnp.exp2(-8.0 * jnp.arange(1, n_heads + 1, dtype=jnp.float32) / n_heads)
    v_t = jnp.transpose(proj[:, 2 * n_heads * HEAD_DIM:])
    grid_spec = pltpu.PrefetchScalarGridSpec(
        num_scalar_prefetch=1,
        grid=(bsz, n_heads, nq),
        in_specs=[pl.BlockSpec((tq, HEAD_DIM), lambda b, h, i, s: (b * nq + i, h)),
                  pl.BlockSpec((t_len, HEAD_DIM), lambda b, h, i, s: (b, n_heads + h)),
                  pl.BlockSpec((HEAD_DIM, t_len), lambda b, h, i, s: (h, b))],
        out_specs=pl.BlockSpec((tq, HEAD_DIM), lambda b, h, i, s: (b * nq + i, h)),
        scratch_shapes=[pltpu.VMEM((2 * nb, HEAD_DIM), jnp.bfloat16),
                        pltpu.VMEM((MB_BLOCK, LANES), jnp.bfloat16),
                        pltpu.VMEM((MB_BLOCK, LANES), jnp.bfloat16),
                        pltpu.VMEM((2, 2 * MB_BLOCK, tq), jnp.float32),
                        pltpu.VMEM((MB_BLOCK, tq), jnp.float32),
                        pltpu.VMEM((1, tq), jnp.float32),
                        pltpu.VMEM((8, tq), jnp.float32),
                        pltpu.VMEM((HEAD_DIM, tq), jnp.float32)],
    )
    return pl.pallas_call(
        _moba_kernel,
        grid_spec=grid_spec,
        out_shape=jax.ShapeDtypeStruct((bsz * t_len, n_heads * HEAD_DIM), jnp.bfloat16),
        compiler_params=_cparams(("parallel", "parallel", "arbitrary")),
        name="moba_attention",
    )(slopes, proj, proj, v_t)


def _pack_bf16_pairs(x):
    half = x.shape[1] // 2
    bits = lax.bitcast_convert_type(x.astype(jnp.float32), jnp.uint32)
    return (bits[:, :half] >> 16) | bits[:, half:]


def _layer_norm_rows(y, g, b):
    mu = jnp.mean(y, axis=-1, keepdims=True)
    yc = y - mu
    var = jnp.mean(yc * yc, axis=-1, keepdims=True)
    return yc * lax.rsqrt(var + LN_EPS) * g + b


def _post_mixer_kernel(h_ref, w_ref, xres_ref, g_ref, b_ref, rw_ref, rb_ref,
                       x1_ref, x1b_ref, topi_ref, topg_ref, rank_ref, count_ref, *, alpha, n_experts):
    i = pl.program_id(0)
    j = pl.program_id(1)
    tm = h_ref.shape[0]
    tn = w_ref.shape[1]
    col = pl.multiple_of(j * tn, tn)
    x1_ref[:, pl.ds(col, tn)] = jnp.dot(h_ref[...], w_ref[...], preferred_element_type=jnp.float32)

    @pl.when((i == 0) & (j == 0))
    def _():
        count_ref[...] = jnp.zeros_like(count_ref)

    @pl.when(j == pl.num_programs(1) - 1)
    def _():
        x1 = _layer_norm_rows(alpha * xres_ref[...] + x1_ref[...], g_ref[...], b_ref[...])
        x1_ref[...] = x1
        x1_hi = x1.astype(jnp.bfloat16)
        packed = _pack_bf16_pairs(x1_hi)
        chunks = packed.shape[1] // LANES
        for c in range(chunks):
            x1b_ref[pl.ds(c, tm, stride=chunks), :] = packed[:, c * LANES:(c + 1) * LANES]
        x1_lo = (x1 - x1_hi.astype(jnp.float32)).astype(jnp.bfloat16)
        hi_terms = jnp.dot(x1_hi, rw_ref[...], preferred_element_type=jnp.float32)
        lo_term = jnp.dot(x1_lo, rw_ref[:, :LANES], preferred_element_type=jnp.float32)
        logits = hi_terms[:, :LANES] + hi_terms[:, LANES:] + lo_term + rb_ref[...]
        lane_i = lax.broadcasted_iota(jnp.int32, logits.shape, 1)
        lane = lane_i.astype(jnp.float32)
        logits = jnp.where(lane_i < n_experts, logits, -jnp.inf)
        topi = jnp.zeros(logits.shape, jnp.float32)
        topv = jnp.full(logits.shape, -jnp.inf, jnp.float32)
        chosen = []
        for r in range(TOP_K):
            mx = jnp.max(logits, axis=-1, keepdims=True)
            idx = jnp.min(jnp.where(logits == mx, lane, float(LANES)), axis=-1, keepdims=True)
            hit = lane == idx
            chosen.append(hit)
            topi = jnp.where(lane_i == r, idx, topi)
            topv = jnp.where(lane_i == r, mx, topv)
            logits = jnp.where(hit, -jnp.inf, logits)
        e = jnp.exp(topv - jnp.max(topv, axis=-1, keepdims=True))
        topi_ref[...] = topi.astype(jnp.int32)
        topg_ref[...] = e / jnp.sum(e, axis=-1, keepdims=True)
        onehot = jnp.where(functools.reduce(jnp.logical_or, chosen), 1.0, 0.0)
        earlier = jnp.where(lax.broadcasted_iota(jnp.int32, (tm, tm), 1) < lax.broadcasted_iota(jnp.int32, (tm, tm), 0),
                            1.0, 0.0).astype(jnp.bfloat16)
        before = count_ref[...] + jnp.dot(earlier, onehot.astype(jnp.bfloat16), preferred_element_type=jnp.float32)
        rank = jnp.zeros(logits.shape, jnp.float32)
        for r in range(TOP_K):
            rank = jnp.where(lane_i == r, jnp.sum(jnp.where(chosen[r], before, 0.0), axis=-1, keepdims=True), rank)
        rank_ref[...] = rank.astype(jnp.int32)
        count_ref[...] += jnp.sum(onehot, axis=0, keepdims=True)


def _post_mixer(h, w_out, xres, ln_g, ln_b, router_w, router_b, alpha):
    n, kdim = h.shape
    d = w_out.shape[1]
    n_experts = router_w.shape[1]
    tm, tn = min(256, n), min(1024, d)
    rw = jnp.pad(router_w, ((0, 0), (0, LANES - n_experts)))
    rw_hi = rw.astype(jnp.bfloat16)
    rw_lo = (rw - rw_hi.astype(jnp.float32)).astype(jnp.bfloat16)
    rw = jnp.concatenate([rw_hi, rw_lo], axis=1)
    rb = jnp.pad(router_b, (0, LANES - n_experts))[None, :]
    row = lambda i, j: (i, 0)
    fixed = lambda i, j: (0, 0)
    return pl.pallas_call(
        functools.partial(_post_mixer_kernel, alpha=alpha, n_experts=n_experts),
        grid=(n // tm, d // tn),
        in_specs=[pl.BlockSpec((tm, kdim), row),
                  pl.BlockSpec((kdim, tn), lambda i, j: (0, j)),
                  pl.BlockSpec((tm, d), row),
                  pl.BlockSpec((1, d), fixed),
                  pl.BlockSpec((1, d), fixed),
                  pl.BlockSpec((d, 2 * LANES), fixed),
                  pl.BlockSpec((1, LANES), fixed)],
        out_specs=[pl.BlockSpec((tm, d), row),
                   pl.BlockSpec((tm * (d // 2 // LANES), LANES), row),
                   pl.BlockSpec((tm, LANES), row),
                   pl.BlockSpec((tm, LANES), row),
                   pl.BlockSpec((tm, LANES), row),
                   pl.BlockSpec((1, LANES), fixed)],
        out_shape=[jax.ShapeDtypeStruct((n, d), jnp.float32),
                   jax.ShapeDtypeStruct((n * (d // 2 // LANES), LANES), jnp.uint32),
                   jax.ShapeDtypeStruct((n, LANES), jnp.int32),
                   jax.ShapeDtypeStruct((n, LANES), jnp.float32),
                   jax.ShapeDtypeStruct((n, LANES), jnp.int32),
                   jax.ShapeDtypeStruct((1, LANES), jnp.float32)],
        compiler_params=_cparams(("arbitrary", "arbitrary")),
        name="post_mixer",
    )(h, w_out, xres, ln_g[None, :], ln_b[None, :], rw, rb)


def _moe_kernel(ge_ref, meta_ref, odd_ref, next_odd_ref, rows_ref, x_hbm, wg_ref, wu_ref, wd_ref, bg_ref, bu_ref,
                bd_ref, y_ref, hid_ref, xbuf_ref, xs_ref, sem, *, nf):
    g = pl.program_id(0)
    s = pl.program_id(1)
    t = pl.program_id(2)
    n_groups = meta_ref[0]
    first_odd = meta_ref[1]
    valid = (g < n_groups) & ((t == 0) | (odd_ref[g] != 0))
    tm = xs_ref.shape[1]
    tf = wg_ref.shape[1]
    bf16 = jnp.bfloat16

    chunks = xs_ref.shape[2] // (2 * LANES)
    pitch = chunks + 1

    def gather_copy(tile, r, slot):
        token = rows_ref[tile * tm + r]
        return pltpu.make_async_copy(x_hbm.at[pl.ds(token * chunks, chunks)],
                                     xbuf_ref.at[slot, pl.ds(r * pitch, chunks)], sem.at[slot])

    def start_gather(tile, slot):
        def issue(r, carry):
            gather_copy(tile, r, slot).start()
            return carry
        lax.fori_loop(0, tm, issue, 0, unroll=8)

    def wait_gather(slot):
        done = xbuf_ref.at[slot, pl.ds(0, tm * chunks)]
        pltpu.make_async_copy(done, done, sem.at[slot]).wait()

    def unpack_rows(slot):
        half = chunks * LANES
        for c in range(chunks):
            p = xbuf_ref[slot, pl.ds(c, tm, stride=pitch), :]
            lo = lax.bitcast_convert_type(p << 16, jnp.float32)
            hi = lax.bitcast_convert_type(p & jnp.uint32(0xFFFF0000), jnp.float32)
            xs_ref[slot, :, c * LANES:(c + 1) * LANES] = lo.astype(bf16)
            xs_ref[slot, :, half + c * LANES:half + (c + 1) * LANES] = hi.astype(bf16)

    @pl.when(valid & (s == 0))
    def _():
        @pl.when((g == 0) & (t == 0))
        def _():
            start_gather(0, 0)

            @pl.when(first_odd >= 0)
            def _():
                start_gather(2 * first_odd + 1, 1)

        wait_gather(t)
        unpack_rows(t)

        @pl.when((t == 0) & (g + 1 < n_groups))
        def _():
            start_gather(2 * (g + 1), 0)

        @pl.when((t == 1) & (next_odd_ref[g] >= 0))
        def _():
            start_gather(2 * next_odd_ref[g] + 1, 1)

    @pl.when(valid & (s < nf))
    def _():
        xs = xs_ref[t]
        h_gate = jnp.dot(xs, wg_ref[...].astype(bf16), preferred_element_type=jnp.float32) + bg_ref[...]
        h_up = jnp.dot(xs, wu_ref[...].astype(bf16), preferred_element_type=jnp.float32) + bu_ref[...]
        h_gate = jnp.minimum(h_gate, SWIGLU_LIMIT)
        h_up = jnp.clip(h_up, -SWIGLU_LIMIT, SWIGLU_LIMIT)
        hid = (h_up + 1.0) * (h_gate * _sigmoid(SWIGLU_ALPHA * h_gate))
        hid_ref[t, :, pl.ds(pl.multiple_of(s * tf, tf), tf)] = hid.astype(bf16)

    @pl.when(valid & (s >= nf))
    def _():
        y = jnp.dot(hid_ref[t], wd_ref[...].astype(bf16), preferred_element_type=jnp.float32) + bd_ref[...]
        y_ref[...] = y.astype(y_ref.dtype)

    @pl.when(jnp.logical_not(valid) & (s >= nf))
    def _():
        y_ref[...] = jnp.zeros_like(y_ref)


def _moe_experts(x_packed, rows, group_e, meta, odd, next_odd, layer, w_gate, w_up, w_down, b_gate, b_up, b_down, tm):
    n_slots = rows.shape[0]
    _, n_experts, d, f = w_gate.shape
    tf, td = min(256, f), min(2048, d)
    nf, nd = f // tf, d // td
    n_groups = n_slots // (MOE_GROUP * tm)

    def f_tile(g, s, meta):
        return jnp.where(g < meta[0], jnp.minimum(s, nf - 1), nf - 1)

    def d_tile(g, s, meta):
        return jnp.where(g < meta[0], jnp.clip(s - nf, 0, nd - 1), nd - 1)

    def w_spec(shape, tile):
        return pl.BlockSpec((None, None) + shape,
                            lambda g, s, t, ge, meta, odd, nxt, rw: (layer, ge[g], 0, tile(g, s, meta)))

    def y_index(g, s, t, ge, meta, odd, nxt, rw):
        return (MOE_GROUP * g + jnp.where(s >= nf, t, 0), jnp.clip(s - nf, 0, nd - 1))

    grid_spec = pltpu.PrefetchScalarGridSpec(
        num_scalar_prefetch=5,
        grid=(n_groups, nf + nd, MOE_GROUP),
        in_specs=[pl.BlockSpec(memory_space=pl.ANY),
                  w_spec((d, tf), f_tile), w_spec((d, tf), f_tile), w_spec((f, td), d_tile),
                  w_spec((1, tf), f_tile), w_spec((1, tf), f_tile), w_spec((1, td), d_tile)],
        out_specs=pl.BlockSpec((tm, td), y_index),
        scratch_shapes=[pltpu.VMEM((MOE_GROUP, tm, f), jnp.bfloat16),
                        pltpu.VMEM((MOE_GROUP, tm * (d // 2 // LANES + 1), LANES), jnp.uint32),
                        pltpu.VMEM((MOE_GROUP, tm, d), jnp.bfloat16),
                        pltpu.SemaphoreType.DMA((MOE_GROUP,))],
    )
    return pl.pallas_call(
        functools.partial(_moe_kernel, nf=nf),
        grid_spec=grid_spec,
        out_shape=jax.ShapeDtypeStruct((n_slots, d), jnp.bfloat16),
        compiler_params=_cparams(("arbitrary", "arbitrary", "arbitrary")),
        name="moe_experts",
    )(group_e, meta, odd, next_odd, rows, x_packed, w_gate, w_up, w_down,
      b_gate[:, :, None, :], b_up[:, :, None, :], b_down[:, :, None, :])


def _combine_kernel(yg_ref, gate_ref, x1_ref, g_ref, b_ref, o_ref, ob_ref, *, alpha):
    gates = gate_ref[...]
    f = yg_ref[0].astype(jnp.float32) * gates[:, 0:1]
    for r in range(1, TOP_K):
        f = f + yg_ref[r].astype(jnp.float32) * gates[:, r:r + 1]
    out = _layer_norm_rows(alpha * x1_ref[...] + f, g_ref[...], b_ref[...])
    o_ref[...] = out
    ob_ref[...] = out.astype(ob_ref.dtype)


def _moe_combine(yg, gates, x1, ln_g, ln_b, alpha):
    n, d = x1.shape
    tm = min(256, n)
    return pl.pallas_call(
        functools.partial(_combine_kernel, alpha=alpha),
        grid=(n // tm,),
        in_specs=[pl.BlockSpec((TOP_K, tm, d), lambda i: (0, i, 0)),
                  pl.BlockSpec((tm, LANES), lambda i: (i, 0)),
                  pl.BlockSpec((tm, d), lambda i: (i, 0)),
                  pl.BlockSpec((1, d), lambda i: (0, 0)),
                  pl.BlockSpec((1, d), lambda i: (0, 0))],
        out_specs=[pl.BlockSpec((tm, d), lambda i: (i, 0)),
                   pl.BlockSpec((tm, d), lambda i: (i, 0))],
        out_shape=[jax.ShapeDtypeStruct((n, d), jnp.float32),
                   jax.ShapeDtypeStruct((n, d), jnp.bfloat16)],
        compiler_params=_cparams(("parallel",)),
        name="moe_combine",
    )(yg, gates, x1, ln_g[None, :], ln_b[None, :])


def _moe_layer(x1, x1b, topi, topg, rank, counts, ln_g, ln_b, layer, w_gate, b_gate, w_up, b_up, w_down, b_down,
               alpha):
    n_tok, d = x1.shape
    n_experts = w_gate.shape[1]
    n_assign = n_tok * TOP_K
    tm = min(512, n_assign // n_experts)
    flat_e = topi[:, :TOP_K].reshape(-1)
    rank = rank[:, :TOP_K].reshape(-1)
    counts = counts[0, :n_experts].astype(jnp.int32)
    gm = MOE_GROUP * tm
    padded = (counts + gm - 1) // gm * gm
    pends = jnp.cumsum(padded)
    pstarts = pends - padded
    dest = pstarts[flat_e] + rank
    n_groups = n_assign // gm + n_experts
    flat_tok = jnp.arange(n_assign, dtype=jnp.int32) // TOP_K
    rows = jnp.zeros((n_groups * gm,), jnp.int32).at[dest].set(flat_tok, mode="promise_in_bounds",
                                                               unique_indices=True)
    gidx = jnp.arange(n_groups, dtype=jnp.int32)
    group_e = jnp.minimum(jnp.searchsorted(pends, gidx * gm, side='right'), n_experts - 1).astype(jnp.int32)
    n_used = pends[-1] // gm
    odd = ((gidx < n_used) & (gidx * gm + tm < (pstarts + counts)[group_e])).astype(jnp.int32)
    odd_pos = jnp.where(odd != 0, gidx, n_groups)
    later = jnp.flip(lax.cummin(jnp.flip(odd_pos)))
    first_odd = jnp.where(later[0] < n_groups, later[0], -1)
    nxt = jnp.concatenate([later[1:], jnp.full((1,), n_groups, jnp.int32)])
    next_odd = jnp.where(nxt < n_groups, nxt, -1).astype(jnp.int32)
    meta = jnp.stack([n_used, first_odd]).astype(jnp.int32)
    y = _moe_experts(x1b, rows, group_e, meta, odd, next_odd, layer, w_gate, w_up, w_down, b_gate, b_up, b_down, tm)
    yg = y.at[dest.reshape(n_tok, TOP_K).T].get(mode="promise_in_bounds")
    return _moe_combine(yg, topg, x1, ln_g, ln_b, alpha)


def kernel(x, dn_w_in, dn_conv_w, dn_a_log, dn_dt_bias, dn_norm_w, dn_w_out, mb_w_in, mb_w_out, ln_g, ln_b,
           router_w, router_b, w_gate, b_gate, w_up, b_up, w_down, b_down):
    bsz, t_len, d = x.shape
    depth = ln_g.shape[0]
    alpha = float((2 * depth) ** 0.25)
    n = bsz * t_len
    xf = x.reshape(n, d)
    xb = xf.astype(jnp.bfloat16)
    for i in range(depth):
        j = i // 2
        if i % 2 == 0:
            n_vheads = d // HEAD_DIM
            main = dn_w_in.shape[2] - 2 * n_vheads
            proj = _matmul(xb, dn_w_in[j].astype(jnp.bfloat16), main, jnp.bfloat16, 1024, 512)
            gates = _dn_gates(xf, dn_w_in[j][:, main:], dn_a_log[j], dn_dt_bias[j], n_vheads)
            gates_t = jnp.transpose(gates.reshape(bsz, t_len, LANES), (0, 2, 1))
            h = _dn_delta_rule(proj, gates, gates_t, dn_conv_w[j], dn_norm_w[j], bsz, t_len, d)
            w_out = dn_w_out[j]
        else:
            proj = _matmul(xb, mb_w_in[j].astype(jnp.bfloat16), mb_w_in.shape[2], jnp.bfloat16, 1024, 512)
            h = _moba_attention(proj, bsz, t_len, d // HEAD_DIM)
            w_out = mb_w_out[j]
        x1, x1b, topi, topg, rank, counts = _post_mixer(h, w_out.astype(jnp.bfloat16), xf, ln_g[i, 0], ln_b[i, 0],
                                                        router_w[i], router_b[i], alpha)
        xf, xb = _moe_layer(x1, x1b, topi, topg, rank, counts, ln_g[i, 1], ln_b[i, 1], i, w_gate, b_gate,
                            w_up, b_up, w_down, b_down, alpha)
    return xf.reshape(bsz, t_len, d)
```

```python
import functools

import jax
import jax.numpy as jnp
import numpy as np
from jax import lax
from jax.experimental import pallas as pl
from jax.experimental.pallas import tpu as pltpu

HEAD_DIM = 128
DN_CONV = 4
DN_CHUNK = 256
DN_GROUP = 4
MOE_GROUP = 2
MB_BLOCK = 256
MB_TOPK = 3
TOP_K = 4
SWIGLU_LIMIT = 7.0
SWIGLU_ALPHA = 1.702
LN_EPS = 1e-5
NORM_EPS = 1e-6
NEG_INF = -1e30
LANES = 128
HALO = 16
VMEM_LIMIT = 56 * 1024 * 1024

_HI = lax.Precision.HIGHEST


def _cparams(sem):
    return pltpu.CompilerParams(dimension_semantics=sem, vmem_limit_bytes=VMEM_LIMIT)


def _sigmoid(v):
    return 1.0 / (1.0 + jnp.exp(-v))


def _silu(v):
    return v * _sigmoid(v)


def _mm_kernel(x_ref, w_ref, o_ref):
    o_ref[...] = jnp.dot(x_ref[...], w_ref[...], preferred_element_type=jnp.float32).astype(o_ref.dtype)


def _matmul(x, w, n, out_dtype, tm, tn):
    m, k = x.shape
    tm, tn = min(tm, m), min(tn, n)
    assert n % tn == 0 and m % tm == 0
    return pl.pallas_call(
        _mm_kernel,
        grid=(m // tm, n // tn),
        in_specs=[pl.BlockSpec((tm, k), lambda i, j: (i, 0)),
                  pl.BlockSpec((k, tn), lambda i, j: (0, j))],
        out_specs=pl.BlockSpec((tm, tn), lambda i, j: (i, j)),
        out_shape=jax.ShapeDtypeStruct((m, n), out_dtype),
        compiler_params=_cparams(("parallel", "arbitrary")),
        name="dense_matmul",
    )(x, w)


def _dn_gates_kernel(x_ref, w_ref, alog_ref, dtb_ref, o_ref, *, n_heads):
    tm = x_ref.shape[0]
    ba = jnp.dot(x_ref[...], w_ref[...], precision=_HI, preferred_element_type=jnp.float32)
    lane = lax.broadcasted_iota(jnp.int32, ba.shape, 1)
    beta = _sigmoid(ba)
    v = ba + dtb_ref[...]
    softplus = jnp.maximum(v, 0.0) + jnp.log(1.0 + jnp.exp(-jnp.abs(v)))
    g = -jnp.exp(alog_ref[...]) * softplus
    g = jnp.where((lane >= n_heads) & (lane < 2 * n_heads), g, 0.0)
    ri = lax.broadcasted_iota(jnp.int32, (tm, tm), 0)
    ci = lax.broadcasted_iota(jnp.int32, (tm, tm), 1)
    shift = DN_CHUNK.bit_length() - 1
    tril = jnp.where((ci <= ri) & ((ri >> shift) == (ci >> shift)), 1.0, 0.0)
    gc = jnp.dot(tril, g, precision=_HI, preferred_element_type=jnp.float32)
    o_ref[...] = jnp.where(lane < n_heads, beta, gc)


def _dn_gates(x2d, w_ba, a_log, dt_bias, n_heads):
    n, d = x2d.shape
    tm = min(256, n)
    pad = LANES - 2 * n_heads
    w_p = jnp.pad(w_ba, ((0, 0), (0, pad)))
    alog_p = jnp.pad(a_log, (n_heads, pad))[None, :]
    dtb_p = jnp.pad(dt_bias, (n_heads, pad))[None, :]
    return pl.pallas_call(
        functools.partial(_dn_gates_kernel, n_heads=n_heads),
        grid=(n // tm,),
        in_specs=[pl.BlockSpec((tm, d), lambda i: (i, 0)),
                  pl.BlockSpec((d, LANES), lambda i: (0, 0)),
                  pl.BlockSpec((1, LANES), lambda i: (0, 0)),
                  pl.BlockSpec((1, LANES), lambda i: (0, 0))],
        out_specs=pl.BlockSpec((tm, LANES), lambda i: (i, 0)),
        out_shape=jax.ShapeDtypeStruct((n, LANES), jnp.float32),
        compiler_params=_cparams(("parallel",)),
        name="dn_gates",
    )(x2d, w_p, alog_p, dtb_p)


def _causal_conv_silu(x_ref, halo_ref, w_ref, first):
    halo = jnp.where(first, 0.0, halo_ref[...].astype(jnp.float32))
    xcat = jnp.concatenate([halo, x_ref[...].astype(jnp.float32)], axis=0)
    w = w_ref[...]
    acc = xcat[HALO:] * w[DN_CONV - 1:DN_CONV, :]
    for s in range(1, DN_CONV):
        acc = acc + pltpu.roll(xcat, s, 0)[HALO:] * w[DN_CONV - 1 - s:DN_CONV - s, :]
    return _silu(acc)


def _l2norm(v):
    return v * lax.rsqrt(jnp.sum(v * v, axis=-1, keepdims=True) + NORM_EPS)


def _inverse_level_masks(c):
    ri = np.arange(c)[:, None]
    ci = np.arange(c)[None, :]
    masks = [((ri >> 1) == (ci >> 1)) & (ri > ci)]
    s = 2
    while s < c:
        masks.append(((ri // (2 * s)) == (ci // (2 * s))) & ((ri & s) != 0) & ((ci & s) == 0))
        s *= 2
    return jnp.asarray(np.stack(masks), jnp.bfloat16)


def _dn_kernel(q_ref, k_ref, v_ref, z_ref, qh_ref, kh_ref, vh_ref, g_ref, gt_ref,
               wq_ref, wk_ref, wv_ref, nw_ref, lm_ref, negc_ref, o_ref, state_ref, *, n_vheads):
    hg = pl.program_id(1)
    t = pl.program_id(2)
    first = t == 0
    tb = q_ref.shape[0]
    group = q_ref.shape[1] // HEAD_DIM
    rep = v_ref.shape[1] // q_ref.shape[1]
    bf16, f32 = jnp.bfloat16, jnp.float32

    @pl.when(first)
    def _():
        state_ref[...] = jnp.zeros_like(state_ref)

    def head(x, c):
        return x[:, c * HEAD_DIM:(c + 1) * HEAD_DIM]

    q_all = _causal_conv_silu(q_ref, qh_ref, wq_ref, first)
    k_all = _causal_conv_silu(k_ref, kh_ref, wk_ref, first)
    v = _causal_conv_silu(v_ref, vh_ref, wv_ref, first)
    q = [_l2norm(head(q_all, g)) * (HEAD_DIM ** -0.5) for g in range(group)]
    k = [_l2norm(head(k_all, g)) for g in range(group)]
    gates = g_ref[...]
    lane = lax.broadcasted_iota(jnp.int32, gates.shape, 1)
    eye = jnp.where(lax.broadcasted_iota(jnp.int32, (tb, tb), 0) == lax.broadcasted_iota(jnp.int32, (tb, tb), 1),
                    1.0, 0.0)
    nt = (((1,), (1,)), ((), ()))
    k_b = [k[g].astype(bf16) for g in range(group)]
    kk = [lax.dot_general(k_b[g], k_b[g], nt, preferred_element_type=f32) for g in range(group)]
    qk = [lax.dot_general(q[g].astype(bf16), k_b[g], nt, preferred_element_type=f32) for g in range(group)]

    heads = range(group * rep)
    beta, gcol, g_last, decay, a_b, inv = [], [], [], [], [], []
    for c in heads:
        hv = hg * group * rep + c
        beta.append(jnp.sum(jnp.where(lane == hv, gates, 0.0), axis=-1, keepdims=True))
        gcol.append(jnp.sum(jnp.where(lane == n_vheads + hv, gates, 0.0), axis=-1, keepdims=True))
        grow = gt_ref[pl.ds(n_vheads + hv, 1), :]
        g_last.append(grow[:, tb - 1:tb])
        decay.append(jnp.exp(gcol[c] - grow + negc_ref[...]))
        a_b.append((kk[c // rep] * beta[c] * decay[c]).astype(bf16))
        inv.append(eye - (a_b[c] * lm_ref[0]).astype(f32))
    for lvl in range(1, lm_ref.shape[0]):
        tmp = [jnp.dot(a_b[c] * lm_ref[lvl], inv[c].astype(bf16), preferred_element_type=f32) for c in heads]
        inv = [inv[c] - jnp.dot(inv[c].astype(bf16), tmp[c].astype(bf16), preferred_element_type=f32)
               for c in heads]
    eg = [jnp.exp(gcol[c]) for c in heads]
    uw = [jnp.dot(inv[c].astype(bf16),
                  jnp.concatenate([head(v, c) * beta[c], k[c // rep] * (beta[c] * eg[c])], axis=1).astype(bf16),
                  preferred_element_type=f32) for c in heads]
    state = [state_ref[c] for c in heads]
    ws = [jnp.dot(jnp.concatenate([uw[c][:, HEAD_DIM:], q[c // rep] * eg[c]], axis=0).astype(bf16),
                  state[c].astype(bf16), preferred_element_type=f32) for c in heads]
    v_new_b = [(uw[c][:, :HEAD_DIM] - ws[c][:tb]).astype(bf16) for c in heads]
    outs = []
    for c in heads:
        o = ws[c][tb:] + jnp.dot((qk[c // rep] * decay[c]).astype(bf16), v_new_b[c], preferred_element_type=f32)
        k_dec_t = (k[c // rep] * jnp.exp(g_last[c] - gcol[c])).T.astype(bf16)
        state_ref[c] = state[c] * jnp.exp(g_last[c]) + jnp.dot(k_dec_t, v_new_b[c], preferred_element_type=f32)
        zc = head(z_ref[...], c).astype(f32)
        outs.append(o * lax.rsqrt(jnp.mean(o * o, axis=-1, keepdims=True) + NORM_EPS) * nw_ref[...] * _silu(zc))
    o_ref[...] = jnp.concatenate(outs, axis=1).astype(o_ref.dtype)


def _dn_delta_rule(proj, gates, gates_t, conv_w, norm_w, bsz, t_len, d_model):
    n_kheads = d_model // (2 * HEAD_DIM)
    n_vheads = d_model // HEAD_DIM
    group = min(DN_GROUP, n_kheads)
    kw = group * HEAD_DIM
    vw = kw * (n_vheads // n_kheads)
    tb = DN_CHUNK
    nt = t_len // tb
    hb = tb // HALO
    level_masks = _inverse_level_masks(tb)
    ri = np.arange(tb)
    neg_causal = jnp.asarray(np.where(ri[None, :] <= ri[:, None], 0.0, NEG_INF), jnp.float32)
    k_off = n_kheads * HEAD_DIM // kw
    v_off = 2 * n_kheads * HEAD_DIM // vw
    z_off = v_off + n_vheads * HEAD_DIM // vw

    def row(b, h, t):
        return b * nt + t

    def halo_row(b, h, t):
        return jnp.maximum((b * nt + t) * hb - 1, 0)

    in_specs = [
        pl.BlockSpec((tb, kw), lambda b, h, t: (row(b, h, t), h)),
        pl.BlockSpec((tb, kw), lambda b, h, t: (row(b, h, t), k_off + h)),
        pl.BlockSpec((tb, vw), lambda b, h, t: (row(b, h, t), v_off + h)),
        pl.BlockSpec((tb, vw), lambda b, h, t: (row(b, h, t), z_off + h)),
        pl.BlockSpec((HALO, kw), lambda b, h, t: (halo_row(b, h, t), h)),
        pl.BlockSpec((HALO, kw), lambda b, h, t: (halo_row(b, h, t), k_off + h)),
        pl.BlockSpec((HALO, vw), lambda b, h, t: (halo_row(b, h, t), v_off + h)),
        pl.BlockSpec((tb, LANES), lambda b, h, t: (row(b, h, t), 0)),
        pl.BlockSpec((None, LANES, tb), lambda b, h, t: (b, 0, t)),
        pl.BlockSpec((DN_CONV, kw), lambda b, h, t: (0, h)),
        pl.BlockSpec((DN_CONV, kw), lambda b, h, t: (0, k_off + h)),
        pl.BlockSpec((DN_CONV, vw), lambda b, h, t: (0, v_off + h)),
        pl.BlockSpec((1, HEAD_DIM), lambda b, h, t: (0, 0)),
        pl.BlockSpec(level_masks.shape, lambda b, h, t: (0, 0, 0)),
        pl.BlockSpec((tb, tb), lambda b, h, t: (0, 0)),
    ]
    return pl.pallas_call(
        functools.partial(_dn_kernel, n_vheads=n_vheads),
        grid=(bsz, n_kheads // group, nt),
        in_specs=in_specs,
        out_specs=pl.BlockSpec((tb, vw), lambda b, h, t: (row(b, h, t), h)),
        out_shape=jax.ShapeDtypeStruct((bsz * t_len, n_vheads * HEAD_DIM), jnp.bfloat16),
        scratch_shapes=[pltpu.VMEM((vw // HEAD_DIM, HEAD_DIM, HEAD_DIM), jnp.float32)],
        compiler_params=_cparams(("parallel", "parallel", "arbitrary")),
        name="dn_delta_rule",
    )(proj, proj, proj, proj, proj, proj, proj, gates, gates_t, conv_w, conv_w, conv_w, norm_w[None, :],
      level_masks, neg_causal)


def _split3(a):
    hi = a.astype(jnp.bfloat16).astype(jnp.float32)
    mid = (a - hi).astype(jnp.bfloat16).astype(jnp.float32)
    lo = (a - hi - mid).astype(jnp.bfloat16).astype(jnp.float32)
    return hi, mid, lo


def _moba_kernel(slope_ref, q_ref, k_ref, vt_ref, o_ref, kmean_ref, qaug_ref, kaug_ref, s_ref, s_own_ref, m_ref, l_ref,
                 acc_ref):
    h = pl.program_id(1)
    own = pl.program_id(2)
    tq = q_ref.shape[0]
    t_len = k_ref.shape[0]
    nb = t_len // MB_BLOCK
    nt = (((1,), (1,)), ((), ()))
    log2e = 1.4426950408889634
    slope2 = slope_ref[h] * log2e

    @pl.when(own == 0)
    def _():
        kf = k_ref[...].astype(jnp.float32).reshape(nb, MB_BLOCK, HEAD_DIM)
        kmean = jnp.sum(kf, axis=1) * (1.0 / MB_BLOCK)
        kmean_hi = kmean.astype(jnp.bfloat16)
        kmean_lo = (kmean - kmean_hi.astype(jnp.float32)).astype(jnp.bfloat16)
        kmean_ref[...] = jnp.concatenate([kmean_hi, kmean_lo], axis=0)
        pos = lax.broadcasted_iota(jnp.int32, (MB_BLOCK, LANES), 0).astype(jnp.float32)
        col = lax.broadcasted_iota(jnp.int32, (MB_BLOCK, LANES), 1)
        q_hi, q_mid, q_lo = _split3(pos * (-slope2))
        k_hi, k_mid, k_lo = _split3(pos * slope2)
        one = jnp.where(col < 6, 1.0, 0.0)
        qaug = jnp.where(col == 0, q_hi, jnp.where(col == 1, q_mid, jnp.where(col == 2, q_lo, one)))
        kaug = jnp.where(col == 3, k_hi, jnp.where(col == 4, k_mid, jnp.where(col == 5, k_lo, one)))
        qaug_ref[...] = qaug.astype(qaug_ref.dtype)
        kaug_ref[...] = kaug.astype(kaug_ref.dtype)

    q = q_ref[...]
    qf = q.astype(jnp.float32)
    gate2 = lax.dot_general(kmean_ref[...], q, nt, preferred_element_type=jnp.float32)
    gate = gate2[:nb] + gate2[nb:]
    blk = lax.broadcasted_iota(jnp.int32, gate.shape, 0).astype(jnp.float32)
    gate = jnp.where(blk < own.astype(jnp.float32), gate, -jnp.inf)
    picks = jnp.zeros((1, tq), jnp.int32)
    for r in range(MB_TOPK):
        mx = jnp.max(gate, axis=0, keepdims=True)
        idx = jnp.min(jnp.where(gate == mx, blk, float(nb)), axis=0, keepdims=True)
        bit = jnp.left_shift(1, idx.astype(jnp.int32))
        picks = picks | jnp.where(r < own, bit, 0)
        gate = jnp.where(blk == idx, -jnp.inf, gate)

    q2 = jnp.concatenate([(qf * (HEAD_DIM ** -0.5 * log2e)).astype(jnp.bfloat16), qaug_ref[...]], axis=1)
    kaug = kaug_ref[...]

    def scores(n):
        start = pl.multiple_of(n * MB_BLOCK, MB_BLOCK)
        k2 = jnp.concatenate([k_ref[pl.ds(start, MB_BLOCK), :], kaug], axis=1)
        return lax.dot_general(k2, q2, nt, preferred_element_type=jnp.float32)

    def values_t(n):
        return vt_ref[:, pl.ds(pl.multiple_of(n * MB_BLOCK, MB_BLOCK), MB_BLOCK)]

    def past_scores(n):
        picked = (picks & jnp.left_shift(1, n)) != 0
        far = (own - n).astype(jnp.float32) * (-slope2 * MB_BLOCK)
        return scores(n) + jnp.where(picked, far, NEG_INF)

    def fold(op, x):
        parts = [x[r:r + 8] for r in range(0, x.shape[0], 8)]
        while len(parts) > 1:
            parts = [op(parts[a], parts[a + 1]) for a in range(0, len(parts) - 1, 2)] + parts[len(parts) & ~1:]
        return parts[0]

    def stage_pair(slot, i):
        s_ref[slot, :MB_BLOCK, :] = past_scores(jnp.minimum(2 * i, nb - 1))
        s_ref[slot, MB_BLOCK:, :] = past_scores(jnp.minimum(2 * i + 1, nb - 1))

    def pair_values_t(i):
        return vt_ref[:, pl.ds(pl.multiple_of(2 * i * MB_BLOCK, 2 * MB_BLOCK), 2 * MB_BLOCK)]

    def absorb(s, v_t):
        m = m_ref[...]
        m_new = jnp.maximum(m, jnp.max(fold(jnp.maximum, s), axis=0, keepdims=True))
        p = jnp.exp2(s - m_new)
        alpha = jnp.exp2(m - m_new)
        m_ref[...] = m_new
        l_ref[...] = alpha * l_ref[...] + fold(jnp.add, p)
        acc_ref[...] = alpha * acc_ref[...] + jnp.dot(v_t, p.astype(jnp.bfloat16),
                                                      preferred_element_type=jnp.float32)

    n_pairs = (own + 1) // 2

    def two_pairs(j, carry):
        i = 2 * j
        stage_pair(1, i + 1)
        absorb(s_ref[0], pair_values_t(i))

        @pl.when(i + 1 < n_pairs)
        def _():
            stage_pair(0, i + 2)
            absorb(s_ref[1], pair_values_t(i + 1))
        return carry

    m_ref[...] = jnp.full(m_ref.shape, NEG_INF, jnp.float32)
    l_ref[...] = jnp.zeros_like(l_ref)
    acc_ref[...] = jnp.zeros_like(acc_ref)
    stage_pair(0, 0)
    key_pos = lax.broadcasted_iota(jnp.int32, (MB_BLOCK, tq), 0)
    query_pos = lax.broadcasted_iota(jnp.int32, (MB_BLOCK, tq), 1)
    s_own_ref[...] = jnp.where(key_pos <= query_pos, scores(own), NEG_INF)
    lax.fori_loop(0, (n_pairs + 1) // 2, two_pairs, 0)
    absorb(s_own_ref[...], values_t(own))
    l = jnp.sum(l_ref[...], axis=0, keepdims=True)
    o_ref[...] = (acc_ref[...] / l).T.astype(o_ref.dtype)


def _moba_attention(proj, bsz, t_len, n_heads):
    tq = MB_BLOCK
    nq = t_len // tq
    nb = t_len // MB_BLOCK
    assert nb <= 32, "per-row picks are kept as one int32 bit mask"
    slopes = jnp.exp2(-8.0 * jnp.arange(1, n_heads + 1, dtype=jnp.float32) / n_heads)
    v_t = jnp.transpose(proj[:, 2 * n_heads * HEAD_DIM:])
    grid_spec = pltpu.PrefetchScalarGridSpec(
        num_scalar_prefetch=1,
        grid=(bsz, n_heads, nq),
        in_specs=[pl.BlockSpec((tq, HEAD_DIM), lambda b, h, i, s: (b * nq + i, h)),
                  pl.BlockSpec((t_len, HEAD_DIM), lambda b, h, i, s: (b, n_heads + h)),
                  pl.BlockSpec((HEAD_DIM, t_len), lambda b, h, i, s: (h, b))],
        out_specs=pl.BlockSpec((tq, HEAD_DIM), lambda b, h, i, s: (b * nq + i, h)),
        scratch_shapes=[pltpu.VMEM((2 * nb, HEAD_DIM), jnp.bfloat16),
                        pltpu.VMEM((MB_BLOCK, LANES), jnp.bfloat16),
                        pltpu.VMEM((MB_BLOCK, LANES), jnp.bfloat16),
                        pltpu.VMEM((2, 2 * MB_BLOCK, tq), jnp.float32),
                        pltpu.VMEM((MB_BLOCK, tq), jnp.float32),
                        pltpu.VMEM((1, tq), jnp.float32),
                        pltpu.VMEM((8, tq), jnp.float32),
                        pltpu.VMEM((HEAD_DIM, tq), jnp.float32)],
    )
    return pl.pallas_call(
        _moba_kernel,
        grid_spec=grid_spec,
        out_shape=jax.ShapeDtypeStruct((bsz * t_len, n_heads * HEAD_DIM), jnp.bfloat16),
        compiler_params=_cparams(("parallel", "parallel", "arbitrary")),
        name="moba_attention",
    )(slopes, proj, proj, v_t)


def _pack_bf16_pairs(x):
    half = x.shape[1] // 2
    bits = lax.bitcast_convert_type(x.astype(jnp.float32), jnp.uint32)
    return (bits[:, :half] >> 16) | bits[:, half:]


def _layer_norm_rows(y, g, b):
    mu = jnp.mean(y, axis=-1, keepdims=True)
    yc = y - mu
    var = jnp.mean(yc * yc, axis=-1, keepdims=True)
    return yc * lax.rsqrt(var + LN_EPS) * g + b


def _post_mixer_kernel(h_ref, w_ref, xres_ref, g_ref, b_ref, rw_ref, rb_ref,
                       x1_ref, x1b_ref, topi_ref, topg_ref, rank_ref, count_ref, *, alpha, n_experts):
    i = pl.program_id(0)
    j = pl.program_id(1)
    tm = h_ref.shape[0]
    tn = w_ref.shape[1]
    col = pl.multiple_of(j * tn, tn)
    x1_ref[:, pl.ds(col, tn)] = jnp.dot(h_ref[...], w_ref[...], preferred_element_type=jnp.float32)

    @pl.when((i == 0) & (j == 0))
    def _():
        count_ref[...] = jnp.zeros_like(count_ref)

    @pl.when(j == pl.num_programs(1) - 1)
    def _():
        x1 = _layer_norm_rows(alpha * xres_ref[...] + x1_ref[...], g_ref[...], b_ref[...])
        x1_ref[...] = x1
        x1_hi = x1.astype(jnp.bfloat16)
        packed = _pack_bf16_pairs(x1_hi)
        chunks = packed.shape[1] // LANES
        for c in range(chunks):
            x1b_ref[pl.ds(c, tm, stride=chunks), :] = packed[:, c * LANES:(c + 1) * LANES]
        x1_lo = (x1 - x1_hi.astype(jnp.float32)).astype(jnp.bfloat16)
        hi_terms = jnp.dot(x1_hi, rw_ref[...], preferred_element_type=jnp.float32)
        lo_term = jnp.dot(x1_lo, rw_ref[:, :LANES], preferred_element_type=jnp.float32)
        logits = hi_terms[:, :LANES] + hi_terms[:, LANES:] + lo_term + rb_ref[...]
        lane_i = lax.broadcasted_iota(jnp.int32, logits.shape, 1)
        lane = lane_i.astype(jnp.float32)
        logits = jnp.where(lane_i < n_experts, logits, -jnp.inf)
        topi = jnp.zeros(logits.shape, jnp.float32)
        topv = jnp.full(logits.shape, -jnp.inf, jnp.float32)
        chosen = []
        for r in range(TOP_K):
            mx = jnp.max(logits, axis=-1, keepdims=True)
            idx = jnp.min(jnp.where(logits == mx, lane, float(LANES)), axis=-1, keepdims=True)
            hit = lane == idx
            chosen.append(hit)
            topi = jnp.where(lane_i == r, idx, topi)
            topv = jnp.where(lane_i == r, mx, topv)
            logits = jnp.where(hit, -jnp.inf, logits)
        e = jnp.exp(topv - jnp.max(topv, axis=-1, keepdims=True))
        topi_ref[...] = topi.astype(jnp.int32)
        topg_ref[...] = e / jnp.sum(e, axis=-1, keepdims=True)
        onehot = jnp.where(functools.reduce(jnp.logical_or, chosen), 1.0, 0.0)
        earlier = jnp.where(lax.broadcasted_iota(jnp.int32, (tm, tm), 1) < lax.broadcasted_iota(jnp.int32, (tm, tm), 0),
                            1.0, 0.0).astype(jnp.bfloat16)
        before = count_ref[...] + jnp.dot(earlier, onehot.astype(jnp.bfloat16), preferred_element_type=jnp.float32)
        rank = jnp.zeros(logits.shape, jnp.float32)
        for r in range(TOP_K):
            rank = jnp.where(lane_i == r, jnp.sum(jnp.where(chosen[r], before, 0.0), axis=-1, keepdims=True), rank)
        rank_ref[...] = rank.astype(jnp.int32)
        count_ref[...] += jnp.sum(onehot, axis=0, keepdims=True)


def _post_mixer(h, w_out, xres, ln_g, ln_b, router_w, router_b, alpha):
    n, kdim = h.shape
    d = w_out.shape[1]
    n_experts = router_w.shape[1]
    tm, tn = min(256, n), min(1024, d)
    rw = jnp.pad(router_w, ((0, 0), (0, LANES - n_experts)))
    rw_hi = rw.astype(jnp.bfloat16)
    rw_lo = (rw - rw_hi.astype(jnp.float32)).astype(jnp.bfloat16)
    rw = jnp.concatenate([rw_hi, rw_lo], axis=1)
    rb = jnp.pad(router_b, (0, LANES - n_experts))[None, :]
    row = lambda i, j: (i, 0)
    fixed = lambda i, j: (0, 0)
    return pl.pallas_call(
        functools.partial(_post_mixer_kernel, alpha=alpha, n_experts=n_experts),
        grid=(n // tm, d // tn),
        in_specs=[pl.BlockSpec((tm, kdim), row),
                  pl.BlockSpec((kdim, tn), lambda i, j: (0, j)),
                  pl.BlockSpec((tm, d), row),
                  pl.BlockSpec((1, d), fixed),
                  pl.BlockSpec((1, d), fixed),
                  pl.BlockSpec((d, 2 * LANES), fixed),
                  pl.BlockSpec((1, LANES), fixed)],
        out_specs=[pl.BlockSpec((tm, d), row),
                   pl.BlockSpec((tm * (d // 2 // LANES), LANES), row),
                   pl.BlockSpec((tm, LANES), row),
                   pl.BlockSpec((tm, LANES), row),
                   pl.BlockSpec((tm, LANES), row),
                   pl.BlockSpec((1, LANES), fixed)],
        out_shape=[jax.ShapeDtypeStruct((n, d), jnp.float32),
                   jax.ShapeDtypeStruct((n * (d // 2 // LANES), LANES), jnp.uint32),
                   jax.ShapeDtypeStruct((n, LANES), jnp.int32),
                   jax.ShapeDtypeStruct((n, LANES), jnp.float32),
                   jax.ShapeDtypeStruct((n, LANES), jnp.int32),
                   jax.ShapeDtypeStruct((1, LANES), jnp.float32)],
        compiler_params=_cparams(("arbitrary", "arbitrary")),
        name="post_mixer",
    )(h, w_out, xres, ln_g[None, :], ln_b[None, :], rw, rb)


def _moe_kernel(ge_ref, meta_ref, odd_ref, next_odd_ref, rows_ref, x_hbm, wg_ref, wu_ref, wd_ref, bg_ref, bu_ref,
                bd_ref, y_ref, hid_ref, xbuf_ref, xs_ref, sem, *, nf):
    g = pl.program_id(0)
    s = pl.program_id(1)
    t = pl.program_id(2)
    n_groups = meta_ref[0]
    first_odd = meta_ref[1]
    valid = (g < n_groups) & ((t == 0) | (odd_ref[g] != 0))
    tm = xs_ref.shape[1]
    tf = wg_ref.shape[1]
    bf16 = jnp.bfloat16

    chunks = xs_ref.shape[2] // (2 * LANES)
    pitch = chunks + 1

    def gather_copy(tile, r, slot):
        token = rows_ref[tile * tm + r]
        return pltpu.make_async_copy(x_hbm.at[pl.ds(token * chunks, chunks)],
                                     xbuf_ref.at[slot, pl.ds(r * pitch, chunks)], sem.at[slot])

    def start_gather(tile, slot):
        def issue(r, carry):
            gather_copy(tile, r, slot).start(priority=1)
            return carry
        lax.fori_loop(0, tm, issue, 0, unroll=8)

    def wait_gather(slot):
        done = xbuf_ref.at[slot, pl.ds(0, tm * chunks)]
        pltpu.make_async_copy(done, done, sem.at[slot]).wait()

    def unpack_rows(slot):
        half = chunks * LANES
        for c in range(chunks):
            p = xbuf_ref[slot, pl.ds(c, tm, stride=pitch), :]
            lo = lax.bitcast_convert_type(p << 16, jnp.float32)
            hi = lax.bitcast_convert_type(p & jnp.uint32(0xFFFF0000), jnp.float32)
            xs_ref[slot, :, c * LANES:(c + 1) * LANES] = lo.astype(bf16)
            xs_ref[slot, :, half + c * LANES:half + (c + 1) * LANES] = hi.astype(bf16)

    @pl.when(valid & (s == 0))
    def _():
        @pl.when((g == 0) & (t == 0))
        def _():
            start_gather(0, 0)

            @pl.when(first_odd >= 0)
            def _():
                start_gather(2 * first_odd + 1, 1)

        wait_gather(t)
        unpack_rows(t)

        @pl.when((t == 0) & (g + 1 < n_groups))
        def _():
            start_gather(2 * (g + 1), 0)

        @pl.when((t == 1) & (next_odd_ref[g] >= 0))
        def _():
            start_gather(2 * next_odd_ref[g] + 1, 1)

    @pl.when(valid & (s < nf))
    def _():
        xs = xs_ref[t]
        h_gate = jnp.dot(xs, wg_ref[...].astype(bf16), preferred_element_type=jnp.float32) + bg_ref[...]
        h_up = jnp.dot(xs, wu_ref[...].astype(bf16), preferred_element_type=jnp.float32) + bu_ref[...]
        h_gate = jnp.minimum(h_gate, SWIGLU_LIMIT)
        h_up = jnp.clip(h_up, -SWIGLU_LIMIT, SWIGLU_LIMIT)
        hid = (h_up + 1.0) * (h_gate * _sigmoid(SWIGLU_ALPHA * h_gate))
        hid_ref[t, :, pl.ds(pl.multiple_of(s * tf, tf), tf)] = hid.astype(bf16)

    @pl.when(valid & (s >= nf))
    def _():
        y = jnp.dot(hid_ref[t], wd_ref[...].astype(bf16), preferred_element_type=jnp.float32) + bd_ref[...]
        y_ref[...] = y.astype(y_ref.dtype)

    @pl.when(jnp.logical_not(valid) & (s >= nf))
    def _():
        y_ref[...] = jnp.zeros_like(y_ref)


def _moe_experts(x_packed, rows, group_e, meta, odd, next_odd, layer, w_gate, w_up, w_down, b_gate, b_up, b_down, tm):
    n_slots = rows.shape[0]
    _, n_experts, d, f = w_gate.shape
    tf, td = min(256, f), min(2048, d)
    nf, nd = f // tf, d // td
    n_groups = n_slots // (MOE_GROUP * tm)

    def f_tile(g, s, meta):
        return jnp.where(g < meta[0], jnp.minimum(s, nf - 1), nf - 1)

    def d_tile(g, s, meta):
        return jnp.where(g < meta[0], jnp.clip(s - nf, 0, nd - 1), nd - 1)

    def w_spec(shape, tile):
        return pl.BlockSpec((None, None) + shape,
                            lambda g, s, t, ge, meta, odd, nxt, rw: (layer, ge[g], 0, tile(g, s, meta)))

    def y_index(g, s, t, ge, meta, odd, nxt, rw):
        return (MOE_GROUP * g + jnp.where(s >= nf, t, 0), jnp.clip(s - nf, 0, nd - 1))

    grid_spec = pltpu.PrefetchScalarGridSpec(
        num_scalar_prefetch=5,
        grid=(n_groups, nf + nd, MOE_GROUP),
        in_specs=[pl.BlockSpec(memory_space=pl.ANY),
                  w_spec((d, tf), f_tile), w_spec((d, tf), f_tile), w_spec((f, td), d_tile),
                  w_spec((1, tf), f_tile), w_spec((1, tf), f_tile), w_spec((1, td), d_tile)],
        out_specs=pl.BlockSpec((tm, td), y_index),
        scratch_shapes=[pltpu.VMEM((MOE_GROUP, tm, f), jnp.bfloat16),
                        pltpu.VMEM((MOE_GROUP, tm * (d // 2 // LANES + 1), LANES), jnp.uint32),
                        pltpu.VMEM((MOE_GROUP, tm, d), jnp.bfloat16),
                        pltpu.SemaphoreType.DMA((MOE_GROUP,))],
    )
    return pl.pallas_call(
        functools.partial(_moe_kernel, nf=nf),
        grid_spec=grid_spec,
        out_shape=jax.ShapeDtypeStruct((n_slots, d), jnp.bfloat16),
        compiler_params=_cparams(("arbitrary", "arbitrary", "arbitrary")),
        name="moe_experts",
    )(group_e, meta, odd, next_odd, rows, x_packed, w_gate, w_up, w_down,
      b_gate[:, :, None, :], b_up[:, :, None, :], b_down[:, :, None, :])


def _combine_kernel(yg_ref, gate_ref, x1_ref, g_ref, b_ref, o_ref, ob_ref, *, alpha):
    gates = gate_ref[...]
    f = yg_ref[0].astype(jnp.float32) * gates[:, 0:1]
    for r in range(1, TOP_K):
        f = f + yg_ref[r].astype(jnp.float32) * gates[:, r:r + 1]
    out = _layer_norm_rows(alpha * x1_ref[...] + f, g_ref[...], b_ref[...])
    o_ref[...] = out
    ob_ref[...] = out.astype(ob_ref.dtype)


def _moe_combine(yg, gates, x1, ln_g, ln_b, alpha):
    n, d = x1.shape
    tm = min(256, n)
    return pl.pallas_call(
        functools.partial(_combine_kernel, alpha=alpha),
        grid=(n // tm,),
        in_specs=[pl.BlockSpec((TOP_K, tm, d), lambda i: (0, i, 0)),
                  pl.BlockSpec((tm, LANES), lambda i: (i, 0)),
                  pl.BlockSpec((tm, d), lambda i: (i, 0)),
                  pl.BlockSpec((1, d), lambda i: (0, 0)),
                  pl.BlockSpec((1, d), lambda i: (0, 0))],
        out_specs=[pl.BlockSpec((tm, d), lambda i: (i, 0)),
                   pl.BlockSpec((tm, d), lambda i: (i, 0))],
        out_shape=[jax.ShapeDtypeStruct((n, d), jnp.float32),
                   jax.ShapeDtypeStruct((n, d), jnp.bfloat16)],
        compiler_params=_cparams(("parallel",)),
        name="moe_combine",
    )(yg, gates, x1, ln_g[None, :], ln_b[None, :])


def _moe_layer(x1, x1b, topi, topg, rank, counts, ln_g, ln_b, layer, w_gate, b_gate, w_up, b_up, w_down, b_down,
               alpha):
    n_tok, d = x1.shape
    n_experts = w_gate.shape[1]
    n_assign = n_tok * TOP_K
    tm = min(512, n_assign // n_experts)
    flat_e = topi[:, :TOP_K].reshape(-1)
    rank = rank[:, :TOP_K].reshape(-1)
    counts = counts[0, :n_experts].astype(jnp.int32)
    gm = MOE_GROUP * tm
    padded = (counts + gm - 1) // gm * gm
    pends = jnp.cumsum(padded)
    pstarts = pends - padded
    dest = pstarts[flat_e] + rank
    n_groups = n_assign // gm + n_experts
    flat_tok = jnp.arange(n_assign, dtype=jnp.int32) // TOP_K
    rows = jnp.zeros((n_groups * gm,), jnp.int32).at[dest].set(flat_tok, mode="promise_in_bounds",
                                                               unique_indices=True)
    gidx = jnp.arange(n_groups, dtype=jnp.int32)
    group_e = jnp.minimum(jnp.searchsorted(pends, gidx * gm, side='right'), n_experts - 1).astype(jnp.int32)
    n_used = pends[-1] // gm
    odd = ((gidx < n_used) & (gidx * gm + tm < (pstarts + counts)[group_e])).astype(jnp.int32)
    odd_pos = jnp.where(odd != 0, gidx, n_groups)
    later = jnp.flip(lax.cummin(jnp.flip(odd_pos)))
    first_odd = jnp.where(later[0] < n_groups, later[0], -1)
    nxt = jnp.concatenate([later[1:], jnp.full((1,), n_groups, jnp.int32)])
    next_odd = jnp.where(nxt < n_groups, nxt, -1).astype(jnp.int32)
    meta = jnp.stack([n_used, first_odd]).astype(jnp.int32)
    y = _moe_experts(x1b, rows, group_e, meta, odd, next_odd, layer, w_gate, w_up, w_down, b_gate, b_up, b_down, tm)
    yg = y.at[dest.reshape(n_tok, TOP_K).T].get(mode="promise_in_bounds")
    return _moe_combine(yg, topg, x1, ln_g, ln_b, alpha)


def kernel(x, dn_w_in, dn_conv_w, dn_a_log, dn_dt_bias, dn_norm_w, dn_w_out, mb_w_in, mb_w_out, ln_g, ln_b,
           router_w, router_b, w_gate, b_gate, w_up, b_up, w_down, b_down):
    bsz, t_len, d = x.shape
    depth = ln_g.shape[0]
    alpha = float((2 * depth) ** 0.25)
    n = bsz * t_len
    xf = x.reshape(n, d)
    xb = xf.astype(jnp.bfloat16)
    for i in range(depth):
        j = i // 2
        if i % 2 == 0:
            n_vheads = d // HEAD_DIM
            main = dn_w_in.shape[2] - 2 * n_vheads
            proj = _matmul(xb, dn_w_in[j].astype(jnp.bfloat16), main, jnp.bfloat16, 1024, 512)
            gates = _dn_gates(xf, dn_w_in[j][:, main:], dn_a_log[j], dn_dt_bias[j], n_vheads)
            gates_t = jnp.transpose(gates.reshape(bsz, t_len, LANES), (0, 2, 1))
            h = _dn_delta_rule(proj, gates, gates_t, dn_conv_w[j], dn_norm_w[j], bsz, t_len, d)
            w_out = dn_w_out[j]
        else:
            proj = _matmul(xb, mb_w_in[j].astype(jnp.bfloat16), mb_w_in.shape[2], jnp.bfloat16, 1024, 512)
            h = _moba_attention(proj, bsz, t_len, d // HEAD_DIM)
            w_out = mb_w_out[j]
        x1, x1b, topi, topg, rank, counts = _post_mixer(h, w_out.astype(jnp.bfloat16), xf, ln_g[i, 0], ln_b[i, 0],
                                                        router_w[i], router_b[i], alpha)
        xf, xb = _moe_layer(x1, x1b, topi, topg, rank, counts, ln_g[i, 1], ln_b[i, 1], i, w_gate, b_gate,
                            w_up, b_up, w_down, b_down, alpha)
    return xf.reshape(bsz, t_len, d)
```
